```python
import jax, jax.numpy as jnp
from jax import lax
import numpy as np

D_MODEL = 1024
BATCH = 8
SEQ = 2048
DEPTH = 4
DEC_BATCH = 128
DEC_SEQ = 4
PAST_LEN = 16384
PAGE_SIZE = 128

A_HEADS = 4
A_DK = D_MODEL // A_HEADS
A_DV = D_MODEL // A_HEADS
A_WIDTH = A_HEADS * A_DV
A_CHUNK = 128
B_WIDTH = D_MODEL
B_GROUPS = 4
B_CHUNK = 128
C_WIDTH = D_MODEL
C_BLOCKS = 8
C_BLK = C_WIDTH // C_BLOCKS
CONV_W = 4
LRU_C = 8.0
N_BRANCH = 3
ALPHA = float((2 * DEPTH) ** 0.25)
BETA = float((8 * DEPTH) ** -0.25)
LN_EPS = 1e-5
IN_WIDTH = 5 * A_WIDTH + 2 * A_HEADS + 3 * B_WIDTH + 2 * C_WIDTH + N_BRANCH * D_MODEL
F_OFF = 5 * A_WIDTH + A_HEADS

kernel_name = 'hybrid_mlstm_chunkmlp_rglru_step'


def _split_points():
    sizes = (A_WIDTH,) * 5 + (A_HEADS,) * 2 + (B_WIDTH,) * 3 + (C_WIDTH,) * 2 + (D_MODEL,) * N_BRANCH
    return [int(s) for s in np.cumsum(sizes)[:-1]]


def layer_norm(x, g, b):
    xf = x.astype(jnp.float32)
    mu = xf.mean(-1, keepdims=True)
    var = jnp.mean(jnp.square(xf - mu), -1, keepdims=True)
    return ((xf - mu) * lax.rsqrt(var + LN_EPS) * g.astype(jnp.float32) + b.astype(jnp.float32)).astype(x.dtype)


def head_norm(h, g):
    mu = h.mean(-1, keepdims=True)
    var = jnp.mean(jnp.square(h - mu), -1, keepdims=True)
    y = (h - mu) * lax.rsqrt(var + LN_EPS) * g.reshape(A_HEADS, A_DV).astype(jnp.float32)
    return y.reshape(h.shape[0], h.shape[1], A_WIDTH)


def mlstm_chunk(carry, inp):
    c0, n0, m0 = carry
    q, k, v, it, lf = inp
    L = q.shape[2]
    b = jnp.cumsum(lf, axis=-1)
    g = it - b
    m = b + jnp.maximum(m0[..., None], lax.cummax(g, axis=2))
    causal = jnp.tril(jnp.ones((L, L), dtype=bool))
    logd = b[..., :, None] + g[..., None, :] - m[..., :, None]
    dmat = jnp.exp(jnp.where(causal, logd, -jnp.inf))
    s = jnp.einsum('bhtd,bhsd->bhts', q, k) * dmat
    inter = jnp.exp(b + m0[..., None] - m)
    num = inter[..., None] * jnp.einsum('bhtd,bhde->bhte', q, c0) + jnp.einsum('bhts,bhse->bhte', s, v)
    den = inter * jnp.einsum('bhtd,bhd->bht', q, n0) + s.sum(-1)
    h = num / jnp.maximum(jnp.abs(den), jnp.exp(-m))[..., None]
    m_new = m[..., -1]
    w = jnp.exp(b[..., -1:] + g - m_new[..., None])
    decay = jnp.exp(b[..., -1] + m0 - m_new)
    c_new = decay[..., None, None] * c0 + jnp.einsum('bhs,bhsd,bhse->bhde', w, k, v)
    n_new = decay[..., None] * n0 + jnp.einsum('bhs,bhsd->bhd', w, k)
    return (c_new, n_new, m_new), h


def mlstm_seq(q, k, v, it, lf, c0, n0, m0):
    bsz, T = q.shape[0], q.shape[1]
    L = min(A_CHUNK, T)
    nch = T // L

    def to_chunks(a):
        a = jnp.swapaxes(a, 1, 2)
        a = a.reshape(a.shape[:2] + (nch, L) + a.shape[3:])
        return jnp.moveaxis(a, 2, 0)

    (c, n, m), h = lax.scan(mlstm_chunk, (c0, n0, m0),
                            (to_chunks(q), to_chunks(k), to_chunks(v), to_chunks(it), to_chunks(lf)))
    h = jnp.moveaxis(h, 0, 2).reshape(bsz, A_HEADS, T, A_DV)
    return jnp.swapaxes(h, 1, 2), c, n, m


def chunk_spatial_gate(u, v, ws, bs):
    bsz, T, _ = v.shape
    Tp = -(-T // B_CHUNK) * B_CHUNK
    vp = jnp.pad(v, ((0, 0), (0, Tp - T), (0, 0)))
    vc = vp.reshape(bsz, Tp // B_CHUNK, B_CHUNK, B_GROUPS, B_WIDTH // B_GROUPS)
    wm = ws * jnp.tril(jnp.ones((B_CHUNK, B_CHUNK), ws.dtype))
    mixed = jnp.einsum('gts,bnsgc->bntgc', wm, vc) + bs.T[None, None, :, :, None]
    mixed = mixed.reshape(bsz, Tp, B_WIDTH)[:, :T]
    return u * mixed


def causal_conv(x, buf, w, b):
    T = x.shape[1]
    xp = jnp.concatenate([buf.astype(x.dtype), x], axis=1)
    y = b + w[0] * xp[:, 0:T]
    for j in range(1, CONV_W):
        y = y + w[j] * xp[:, j:j + T]
    return y, xp[:, -(CONV_W - 1):]


def rg_lru(x, h0, wa, ba, wx, bx, lam, reset_first):
    bsz, T, _ = x.shape
    xf = x.astype(jnp.float32)
    xb = xf.reshape(bsz, T, C_BLOCKS, C_BLK)
    r = jax.nn.sigmoid(jnp.einsum('btni,nij->btnj', xb, wa.astype(jnp.float32)).reshape(bsz, T, C_WIDTH) + ba)
    i = jax.nn.sigmoid(jnp.einsum('btni,nij->btnj', xb, wx.astype(jnp.float32)).reshape(bsz, T, C_WIDTH) + bx)
    log_a = LRU_C * r * jax.nn.log_sigmoid(lam.astype(jnp.float32))
    a = jnp.exp(log_a)
    mult = jnp.sqrt(-jnp.expm1(2.0 * log_a))
    if reset_first:
        mult = mult.at[:, 0].set(1.0)
    bterm = mult * i * xf
    if h0 is not None:
        bterm = bterm.at[:, 0].add(a[:, 0] * h0.astype(jnp.float32))

    def combine(lhs, rhs):
        a1, b1 = lhs
        a2, b2 = rhs
        return a1 * a2, a2 * b1 + b2

    _, h = lax.associative_scan(combine, (a, bterm), axis=1)
    return h, h[:, -1]


def mixer_layer(x, p, c0, n0, m0, conv_buf, h0, reset_first):
    bsz, T, _ = x.shape
    proj = jnp.einsum('btd,de->bte', x, p['w_in']) + p['b_in']
    (q, k, v_a, o_a, z_a, i_pre, f_pre, u_b, v_b, z_b, x_c, z_c,
     g_a, g_b, g_c) = jnp.split(proj, _split_points(), axis=-1)
    f32 = jnp.float32
    qh = q.astype(f32).reshape(bsz, T, A_HEADS, A_DK)
    kh = k.astype(f32).reshape(bsz, T, A_HEADS, A_DK) * (A_DK ** -0.5)
    vh = v_a.astype(f32).reshape(bsz, T, A_HEADS, A_DV)
    it = i_pre.astype(f32)
    lf = jax.nn.log_sigmoid(f_pre.astype(f32))
    h_a, c_new, n_new, m_new = mlstm_seq(qh, kh, vh, it, lf, c0, n0, m0)
    h_a = jax.nn.sigmoid(o_a.astype(f32)).reshape(bsz, T, A_HEADS, A_DV) * h_a
    y_a = (head_norm(h_a, p['mlstm_norm_g']) * jax.nn.silu(z_a.astype(f32))).astype(x.dtype)
    vn = layer_norm(v_b, p['gmlp_ln_g'], p['gmlp_ln_b'])
    y_b = chunk_spatial_gate(u_b, vn, p['gmlp_ws'], p['gmlp_bs']) * jax.nn.silu(z_b)
    xc, conv_new = causal_conv(x_c, conv_buf, p['lru_conv_w'], p['lru_conv_b'])
    h_c, h_last = rg_lru(xc, h0, p['lru_wa'], p['lru_ba'], p['lru_wx'], p['lru_bx'], p['lru_lambda'], reset_first)
    y_c = (h_c * jax.nn.silu(z_c.astype(f32))).astype(x.dtype)
    merged = (jax.nn.sigmoid(g_a) * (y_a @ p['w_proj_a'])
              + jax.nn.sigmoid(g_b) * (y_b @ p['w_proj_b'])
              + jax.nn.sigmoid(g_c) * (y_c @ p['w_proj_c']))
    out = merged @ p['w_out']
    x_new = layer_norm(ALPHA * x + out, p['ln_g'], p['ln_b'])
    return x_new, c_new, n_new, m_new, conv_new, h_last, vn


def setup_inputs(seed: int = 0) -> dict:
    key = jax.random.key(seed)
    ks = jax.random.split(key, 32)
    f32 = jnp.float32
    nrm = lambda k, shape, s: jax.random.normal(k, shape, f32) * s
    b_in = nrm(ks[8], (DEPTH, IN_WIDTH), 0.01)
    f_bias = jnp.linspace(3.0, 6.0, A_HEADS, dtype=f32)[None, :] + nrm(ks[9], (DEPTH, A_HEADS), 0.01)
    b_in = b_in.at[:, F_OFF:F_OFF + A_HEADS].set(f_bias)
    u = jax.random.uniform(ks[10], (DEPTH, C_WIDTH), f32, 0.9, 0.999)
    s = u ** (1.0 / LRU_C)
    lam = jnp.log(s) - jnp.log1p(-s)
    return {
        'x_prompt': nrm(ks[0], (BATCH, SEQ, D_MODEL), 1.0),
        'x_sample': nrm(ks[1], (DEC_BATCH, DEC_SEQ, D_MODEL), 1.0),
        'state_mlstm_c': nrm(ks[2], (DEPTH, DEC_BATCH, A_HEADS, A_DK, A_DV), 0.1),
        'state_mlstm_n': nrm(ks[3], (DEPTH, DEC_BATCH, A_HEADS, A_DK), 0.5),
        'state_mlstm_m': nrm(ks[4], (DEPTH, DEC_BATCH, A_HEADS), 1.0),
        'state_lru_conv': nrm(ks[5], (DEPTH, DEC_BATCH, CONV_W - 1, C_WIDTH), 1.0),
        'state_lru_h': nrm(ks[6], (DEPTH, DEC_BATCH, C_WIDTH), 0.5),
        'w_in': nrm(ks[7], (DEPTH, D_MODEL, IN_WIDTH), D_MODEL ** -0.5),
        'b_in': b_in,
        'mlstm_norm_g': 1.0 + nrm(ks[11], (DEPTH, A_WIDTH), 0.01),
        'gmlp_ln_g': 1.0 + nrm(ks[12], (DEPTH, B_WIDTH), 0.01),
        'gmlp_ln_b': nrm(ks[13], (DEPTH, B_WIDTH), 0.01),
        'gmlp_ws': nrm(ks[14], (DEPTH, B_GROUPS, B_CHUNK, B_CHUNK), 0.5 * B_CHUNK ** -0.5),
        'gmlp_bs': 1.0 + nrm(ks[15], (DEPTH, B_GROUPS, B_CHUNK), 0.01),
        'lru_conv_w': nrm(ks[16], (DEPTH, CONV_W, C_WIDTH), CONV_W ** -0.5),
        'lru_conv_b': nrm(ks[17], (DEPTH, C_WIDTH), 0.01),
        'lru_wa': nrm(ks[18], (DEPTH, C_BLOCKS, C_BLK, C_BLK), C_BLK ** -0.5),
        'lru_ba': nrm(ks[19], (DEPTH, C_WIDTH), 0.01),
        'lru_wx': nrm(ks[20], (DEPTH, C_BLOCKS, C_BLK, C_BLK), C_BLK ** -0.5),
        'lru_bx': nrm(ks[21], (DEPTH, C_WIDTH), 0.01),
        'lru_lambda': lam,
        'w_proj_a': nrm(ks[22], (DEPTH, A_WIDTH, D_MODEL), BETA * A_WIDTH ** -0.5),
        'w_proj_b': nrm(ks[23], (DEPTH, B_WIDTH, D_MODEL), BETA * B_WIDTH ** -0.5),
        'w_proj_c': nrm(ks[24], (DEPTH, C_WIDTH, D_MODEL), BETA * C_WIDTH ** -0.5),
        'w_out': nrm(ks[25], (DEPTH, D_MODEL, D_MODEL), BETA * D_MODEL ** -0.5),
        'ln_g': 1.0 + nrm(ks[26], (DEPTH, D_MODEL), 0.01),
        'ln_b': nrm(ks[27], (DEPTH, D_MODEL), 0.01),
    }


def reference(x_prompt, x_sample, state_mlstm_c, state_mlstm_n, state_mlstm_m, state_lru_conv, state_lru_h,
              w_in, b_in, mlstm_norm_g, gmlp_ln_g, gmlp_ln_b, gmlp_ws, gmlp_bs, lru_conv_w, lru_conv_b,
              lru_wa, lru_ba, lru_wx, lru_bx, lru_lambda, w_proj_a, w_proj_b, w_proj_c, w_out, ln_g, ln_b):
    f32 = jnp.float32
    bp = x_prompt.shape[0]
    xp, xs = x_prompt, x_sample
    cp_l, np_l, mp_l, convp_l, hp_l = [], [], [], [], []
    cs_l, ns_l, ms_l, convs_l, hs_l, vs_l = [], [], [], [], [], []
    for l in range(DEPTH):
        p = {'w_in': w_in[l], 'b_in': b_in[l], 'mlstm_norm_g': mlstm_norm_g[l],
             'gmlp_ln_g': gmlp_ln_g[l], 'gmlp_ln_b': gmlp_ln_b[l], 'gmlp_ws': gmlp_ws[l], 'gmlp_bs': gmlp_bs[l],
             'lru_conv_w': lru_conv_w[l], 'lru_conv_b': lru_conv_b[l], 'lru_wa': lru_wa[l], 'lru_ba': lru_ba[l],
             'lru_wx': lru_wx[l], 'lru_bx': lru_bx[l], 'lru_lambda': lru_lambda[l],
             'w_proj_a': w_proj_a[l], 'w_proj_b': w_proj_b[l], 'w_proj_c': w_proj_c[l],
             'w_out': w_out[l], 'ln_g': ln_g[l], 'ln_b': ln_b[l]}
        xp, c, n, m, conv, h, _ = mixer_layer(
            xp, p,
            jnp.zeros((bp, A_HEADS, A_DK, A_DV), f32), jnp.zeros((bp, A_HEADS, A_DK), f32),
            jnp.zeros((bp, A_HEADS), f32), jnp.zeros((bp, CONV_W - 1, C_WIDTH), xp.dtype), None, True)
        cp_l.append(c); np_l.append(n); mp_l.append(m); convp_l.append(conv); hp_l.append(h)
        xs, c, n, m, conv, h, vn = mixer_layer(
            xs, p,
            state_mlstm_c[l].astype(f32), state_mlstm_n[l].astype(f32), state_mlstm_m[l].astype(f32),
            state_lru_conv[l], state_lru_h[l], False)
        cs_l.append(c); ns_l.append(n); ms_l.append(m); convs_l.append(conv); hs_l.append(h); vs_l.append(vn)
    return (xp, xs,
            jnp.stack(cp_l), jnp.stack(np_l), jnp.stack(mp_l), jnp.stack(convp_l), jnp.stack(hp_l),
            jnp.stack(cs_l), jnp.stack(ns_l), jnp.stack(ms_l), jnp.stack(convs_l), jnp.stack(hs_l),
            jnp.stack(vs_l))
```

```python
import functools

import jax
import jax.numpy as jnp
from jax import lax
from jax.experimental import pallas as pl
from jax.experimental.pallas import tpu as pltpu

D_MODEL = 1024
N_HEADS = 4
D_HEAD = D_MODEL // N_HEADS
CHUNK = 128
N_GROUPS = 4
D_GROUP = D_MODEL // N_GROUPS
LRU_BLOCKS = 8
LRU_BLK = D_MODEL // LRU_BLOCKS
CONV_W = 4
LRU_C = 8.0
LN_EPS = 1e-5
K_SCALE = D_HEAD ** -0.5
V7X_LANES = 128
V7X_SUBLANES = 8
VMEM_LIMIT_BYTES = 56 * 1024 * 1024

BF16 = jnp.bfloat16
F32 = jnp.float32

_NT = (((1,), (1,)), ((), ()))
_TN = (((0,), (0,)), ((), ()))


def _dot(a, b):
    return jnp.dot(a, b, preferred_element_type=F32)


def _dot_nt(a, b):
    return lax.dot_general(a, b, _NT, preferred_element_type=F32)


def _dot_tn(a, b):
    return lax.dot_general(a, b, _TN, preferred_element_type=F32)


def _layer_norm(x, g, b):
    mu = jnp.mean(x, axis=-1, keepdims=True)
    xc = x - mu
    var = jnp.mean(xc * xc, axis=-1, keepdims=True)
    return xc * lax.rsqrt(var + LN_EPS) * g + b


def _silu(x):
    return x * jax.nn.sigmoid(x)


def _scan_axis(x, axis, op, fill):
    n = x.shape[axis]
    idx = lax.broadcasted_iota(jnp.int32, x.shape, axis)
    s = 1
    while s < n:
        x = op(x, jnp.where(idx >= s, pltpu.roll(x, s, axis), fill))
        s *= 2
    return x


def _const_spec(shape):
    nd = len(shape)
    return pl.BlockSpec(shape, lambda *_: (0,) * nd, pipeline_mode=pl.Buffered(1))


def _layer_spec(shape):
    nd = len(shape)
    return pl.BlockSpec((None,) + shape, lambda *_: (0,) * (nd + 1), pipeline_mode=pl.Buffered(1))


def _mlstm_prompt_kernel(x_ref, w_ref, b_ref, wgc_ref, bgc_ref, wgr_ref, bgr_ref, ng_ref, wp_ref,
                         o_ref, c_out, n_out, m_out,
                         c_s, n_s, m_s, q_s, k_s, v_s, h_s, acc_s,
                         acol_s, inter_s, em_s, wcol_s, decay_s, grow_s, *, tt):
    j = pl.program_id(1)
    nch = tt // CHUNK

    @pl.when(j == 0)
    def _():
        c_s[...] = jnp.zeros_like(c_s)
        n_s[...] = jnp.zeros_like(n_s)
        m_s[...] = jnp.zeros_like(m_s)

    xb = x_ref[...].astype(BF16)

    gcol = _dot(xb, wgc_ref[...]) + bgc_ref[...]
    it_c = gcol[:, :V7X_LANES]
    lf_c = jax.nn.log_sigmoid(gcol[:, V7X_LANES:])
    grow = _dot_nt(wgr_ref[...], xb) + bgr_ref[...]
    it_r = grow[:V7X_SUBLANES]
    lf_r = jax.nn.log_sigmoid(grow[V7X_SUBLANES:])

    m0 = m_s[0:1, :]
    for c in range(nch):
        sl = slice(c * CHUNK, (c + 1) * CHUNK)
        bcol = _scan_axis(lf_c[sl], 0, jnp.add, 0.0)
        gcl = it_c[sl] - bcol
        mcol = bcol + jnp.maximum(m0, _scan_axis(gcl, 0, jnp.maximum, -jnp.inf))
        blast = bcol[CHUNK - 1:CHUNK]
        mlast = mcol[CHUNK - 1:CHUNK]
        acol_s[sl, :] = bcol - mcol
        inter_s[sl, :] = jnp.exp(bcol + m0 - mcol)
        em_s[sl, :] = jnp.exp(-mcol)
        wcol_s[sl, :] = jnp.exp(blast + gcl - mlast)
        decay_s[c:c + 1, :] = jnp.exp(blast + m0 - mlast)
        brow = _scan_axis(lf_r[:, sl], 1, jnp.add, 0.0)
        grow_s[:, sl] = it_r[:, sl] - brow
        m0 = mlast
    m_s[...] = jnp.broadcast_to(m0, m_s.shape)

    tri = (lax.broadcasted_iota(jnp.int32, (CHUNK, CHUNK), 0)
           >= lax.broadcasted_iota(jnp.int32, (CHUNK, CHUNK), 1))

    for h in range(N_HEADS):
        def col(block, h=h):
            lo = block * D_MODEL + h * D_HEAD
            return _dot(xb, w_ref[:, lo:lo + D_HEAD]) + b_ref[:, lo:lo + D_HEAD]

        q_s[...] = col(0).astype(BF16)
        k_s[...] = col(1) * K_SCALE
        v_s[...] = col(2).astype(BF16)

        def chunk_body(c, carry, h=h):
            r0 = pl.multiple_of(c * CHUNK, CHUNK)
            rows = pl.ds(r0, CHUNK)
            qc = q_s[rows, :]
            kc = k_s[rows, :]
            vc = v_s[rows, :]
            a_col = acol_s[rows, :][:, h:h + 1]
            inter = inter_s[rows, :][:, h:h + 1]
            em = em_s[rows, :][:, h:h + 1]
            wcol = wcol_s[rows, :][:, h:h + 1]
            g_row = grow_s[h:h + 1, rows]
            decay = decay_s[pl.ds(c, 1), :][:, h:h + 1]

            dmat = jnp.exp(jnp.where(tri, a_col + g_row, -jnp.inf))
            s = _dot_nt(qc, kc.astype(BF16)) * dmat
            c0 = c_s[h]
            n0 = n_s[h:h + 1, :]
            num = inter * _dot(qc, c0.astype(BF16)) + _dot(s.astype(BF16), vc)
            qn = jnp.sum(qc.astype(F32) * n0, axis=1, keepdims=True)
            den = inter * qn + jnp.sum(s, axis=1, keepdims=True)
            h_s[rows, :] = num / jnp.maximum(jnp.abs(den), em)

            kw = kc * wcol
            c_s[h] = decay * c0 + _dot_tn(kw.astype(BF16), vc)
            n_s[h:h + 1, :] = decay * n0 + jnp.sum(kw, axis=0, keepdims=True)
            return carry

        lax.fori_loop(0, nch, chunk_body, 0)

        hh = jax.nn.sigmoid(col(3)) * h_s[...]
        mu = jnp.mean(hh, axis=-1, keepdims=True)
        hc = hh - mu
        var = jnp.mean(hc * hc, axis=-1, keepdims=True)
        hn = hc * lax.rsqrt(var + LN_EPS) * ng_ref[:, h * D_HEAD:(h + 1) * D_HEAD]
        y = (hn * _silu(col(4))).astype(BF16)
        contrib = _dot(y, wp_ref[h * D_HEAD:(h + 1) * D_HEAD, :])
        if h == 0:
            acc_s[...] = contrib
        else:
            acc_s[...] += contrib

    gate = jax.nn.sigmoid(_dot(xb, w_ref[:, 5 * D_MODEL:]) + b_ref[:, 5 * D_MODEL:])
    o_ref[...] = gate * acc_s[...]

    @pl.when(j == pl.num_programs(1) - 1)
    def _():
        c_out[...] = c_s[...]
        n_out[...] = n_s[...]
        m_out[...] = m_s[...]


def _mlstm_prompt(x, lw, layer, tt):
    bsz, t, _ = x.shape
    kern = functools.partial(_mlstm_prompt_kernel, tt=tt)
    tile = pl.BlockSpec((None, tt, D_MODEL), lambda b, j: (b, j, 0))
    return pl.pallas_call(
        kern,
        grid=(bsz, t // tt),
        in_specs=[
            tile,
            _at_layer(lw["w_a"], layer), _at_layer(lw["b_a"], layer),
            _at_layer(lw["w_gc"], layer), _at_layer(lw["b_gc"], layer),
            _at_layer(lw["w_gr"], layer), _at_layer(lw["b_gr"], layer),
            _at_layer(lw["norm_g"], layer), _at_layer(lw["w_pa"], layer),
        ],
        out_specs=[
            tile,
            pl.BlockSpec((None, N_HEADS, D_HEAD, D_HEAD), lambda b, j: (b, 0, 0, 0)),
            pl.BlockSpec((None, V7X_SUBLANES, D_HEAD), lambda b, j: (b, 0, 0)),
            pl.BlockSpec((None, V7X_SUBLANES, V7X_LANES), lambda b, j: (b, 0, 0)),
        ],
        out_shape=[
            jax.ShapeDtypeStruct((bsz, t, D_MODEL), F32),
            jax.ShapeDtypeStruct((bsz, N_HEADS, D_HEAD, D_HEAD), F32),
            jax.ShapeDtypeStruct((bsz, V7X_SUBLANES, D_HEAD), F32),
            jax.ShapeDtypeStruct((bsz, V7X_SUBLANES, V7X_LANES), F32),
        ],
        scratch_shapes=[
            pltpu.VMEM((N_HEADS, D_HEAD, D_HEAD), F32),
            pltpu.VMEM((V7X_SUBLANES, D_HEAD), F32),
            pltpu.VMEM((V7X_SUBLANES, V7X_LANES), F32),
            pltpu.VMEM((tt, D_HEAD), BF16),
            pltpu.VMEM((tt, D_HEAD), F32),
            pltpu.VMEM((tt, D_HEAD), BF16),
            pltpu.VMEM((tt, D_HEAD), F32),
            pltpu.VMEM((tt, D_MODEL), F32),
            pltpu.VMEM((tt, V7X_LANES), F32),
            pltpu.VMEM((tt, V7X_LANES), F32),
            pltpu.VMEM((tt, V7X_LANES), F32),
            pltpu.VMEM((tt, V7X_LANES), F32),
            pltpu.VMEM((V7X_SUBLANES, V7X_LANES), F32),
            pltpu.VMEM((V7X_SUBLANES, tt), F32),
        ],
        compiler_params=pltpu.CompilerParams(
            dimension_semantics=("arbitrary", "arbitrary"), vmem_limit_bytes=VMEM_LIMIT_BYTES),
        name="mlstm_prompt",
    )(x, lw["w_a"], lw["b_a"], lw["w_gc"], lw["b_gc"], lw["w_gr"], lw["b_gr"], lw["norm_g"], lw["w_pa"])


def _at_layer(arr, layer):
    shape = arr.shape[1:]
    nd = len(shape)
    return pl.BlockSpec((None,) + shape, lambda *_: (layer,) + (0,) * nd, pipeline_mode=pl.Buffered(1))


def _gmlp_prompt_kernel(x_ref, ma_ref, w_ref, b_ref, lng_ref, lnb_ref, ws_ref, bs_ref, wp_ref,
                        o_ref, vn_s, acc_s, *, tt):
    nch = tt // CHUNK
    xb = x_ref[...].astype(BF16)

    def col(block, g=None):
        if g is None:
            lo, n = block * D_MODEL, D_MODEL
        else:
            lo, n = block * D_MODEL + g * D_GROUP, D_GROUP
        return _dot(xb, w_ref[:, lo:lo + n]) + b_ref[:, lo:lo + n]

    vn_s[...] = _layer_norm(col(1), lng_ref[...], lnb_ref[...])
    tri = (lax.broadcasted_iota(jnp.int32, (CHUNK, CHUNK), 0)
           >= lax.broadcasted_iota(jnp.int32, (CHUNK, CHUNK), 1))
    for g in range(N_GROUPS):
        wm = jnp.where(tri, ws_ref[g], 0.0).astype(BF16)
        bias = bs_ref[:, g:g + 1]
        cols = slice(g * D_GROUP, (g + 1) * D_GROUP)
        mixed = jnp.concatenate(
            [_dot(wm, vn_s[c * CHUNK:(c + 1) * CHUNK, cols].astype(BF16)) + bias for c in range(nch)],
            axis=0)
        y = (col(0, g) * mixed * _silu(col(2, g))).astype(BF16)
        contrib = _dot(y, wp_ref[cols, :])
        if g == 0:
            acc_s[...] = contrib
        else:
            acc_s[...] += contrib
    o_ref[...] = ma_ref[...] + jax.nn.sigmoid(col(3)) * acc_s[...]


def _gmlp_prompt(x, merged, lw, layer, tt):
    bsz, t, _ = x.shape
    kern = functools.partial(_gmlp_prompt_kernel, tt=tt)
    tile = pl.BlockSpec((None, tt, D_MODEL), lambda b, j: (b, j, 0))
    return pl.pallas_call(
        kern,
        grid=(bsz, t // tt),
        in_specs=[
            tile, tile,
            _at_layer(lw["w_b"], layer), _at_layer(lw["b_b"], layer),
            _at_layer(lw["gmlp_ln_g"], layer), _at_layer(lw["gmlp_ln_b"], layer),
            _at_layer(lw["gmlp_ws"], layer), _at_layer(lw["gmlp_bs_t"], layer),
            _at_layer(lw["w_pb"], layer),
        ],
        out_specs=tile,
        out_shape=jax.ShapeDtypeStruct((bsz, t, D_MODEL), F32),
        scratch_shapes=[
            pltpu.VMEM((tt, D_MODEL), F32),
            pltpu.VMEM((tt, D_MODEL), F32),
        ],
        compiler_params=pltpu.CompilerParams(
            dimension_semantics=("arbitrary", "arbitrary"), vmem_limit_bytes=VMEM_LIMIT_BYTES),
        name="gmlp_prompt",
    )(x, merged, lw["w_b"], lw["b_b"], lw["gmlp_ln_g"], lw["gmlp_ln_b"], lw["gmlp_ws"], lw["gmlp_bs_t"],
      lw["w_pb"])


def _lru_gates(xc, wax_ref, ba, bx, lam):
    xcb = xc.astype(BF16)
    pre = [_dot(xcb[:, n * LRU_BLK:(n + 1) * LRU_BLK], wax_ref[n]) for n in range(LRU_BLOCKS)]
    r = jax.nn.sigmoid(jnp.concatenate([p[:, :LRU_BLK] for p in pre], axis=1) + ba)
    i = jax.nn.sigmoid(jnp.concatenate([p[:, LRU_BLK:] for p in pre], axis=1) + bx)
    a = jnp.exp(LRU_C * r * jax.nn.log_sigmoid(lam))
    mult = jnp.sqrt(1.0 - a * a)
    return a, mult, i


def _lru_prompt_kernel(x_ref, mab_ref, w_ref, b_ref, cw_ref, cb_ref, wax_ref, ba_ref, bx_ref, lam_ref,
                       wp_ref, wo_ref, lng_ref, lnb_ref,
                       o_ref, conv_out, h_out,
                       xpad_s, a_s, b_s, hcar_s, *, tt, alpha):
    j = pl.program_id(1)
    ng = tt // V7X_SUBLANES

    @pl.when(j == 0)
    def _():
        xpad_s[0:V7X_SUBLANES, :] = jnp.zeros((V7X_SUBLANES, D_MODEL), F32)
        hcar_s[...] = jnp.zeros_like(hcar_s)

    x = x_ref[...]
    xb = x.astype(BF16)

    def col(block):
        lo = block * D_MODEL
        return _dot(xb, w_ref[:, lo:lo + D_MODEL]) + b_ref[:, lo:lo + D_MODEL]

    xpad_s[V7X_SUBLANES:, :] = col(0)
    xc = cb_ref[...] + cw_ref[0:1, :] * xpad_s[V7X_SUBLANES - 3:V7X_SUBLANES - 3 + tt, :]
    for jj in range(1, CONV_W):
        off = V7X_SUBLANES - 3 + jj
        xc = xc + cw_ref[jj:jj + 1, :] * xpad_s[off:off + tt, :]

    @pl.when(j == pl.num_programs(1) - 1)
    def _():
        conv_out[...] = xpad_s[V7X_SUBLANES + tt - (CONV_W - 1):, :]

    xpad_s[0:V7X_SUBLANES, :] = xpad_s[tt:tt + V7X_SUBLANES, :]

    a, mult, i = _lru_gates(xc, wax_ref, ba_ref[...], bx_ref[...], lam_ref[...])
    row = lax.broadcasted_iota(jnp.int32, (tt, 1), 0)
    mult = jnp.where(jnp.logical_and(row == 0, j == 0), 1.0, mult)
    bterm = mult * i * xc

    a3 = a.reshape(ng, V7X_SUBLANES, D_MODEL)
    b3 = bterm.reshape(ng, V7X_SUBLANES, D_MODEL)
    sub = lax.broadcasted_iota(jnp.int32, a3.shape, 1)
    s = 1
    while s < V7X_SUBLANES:
        keep = sub >= s
        a_sh = jnp.where(keep, pltpu.roll(a3, s, 1), 1.0)
        b_sh = jnp.where(keep, pltpu.roll(b3, s, 1), 0.0)
        b3 = a3 * b_sh + b3
        a3 = a3 * a_sh
        s *= 2
    a_s[...] = a3
    b_s[...] = b3

    def group_body(g, hprev):
        hg = a_s[g] * hprev + b_s[g]
        b_s[g] = hg
        return jnp.broadcast_to(hg[V7X_SUBLANES - 1:V7X_SUBLANES, :], hg.shape)

    hlast = lax.fori_loop(0, ng, group_body, hcar_s[...])
    hcar_s[...] = hlast
    hseq = b_s[...].reshape(tt, D_MODEL)

    @pl.when(j == pl.num_programs(1) - 1)
    def _():
        h_out[...] = hlast

    y = (hseq * _silu(col(1))).astype(BF16)
    merged = mab_ref[...] + jax.nn.sigmoid(col(2)) * _dot(y, wp_ref[...])
    out = _dot(merged.astype(BF16), wo_ref[...])
    o_ref[...] = _layer_norm(alpha * x + out, lng_ref[...], lnb_ref[...])


def _lru_prompt(x, merged, lw, layer, tt, alpha):
    bsz, t, _ = x.shape
    kern = functools.partial(_lru_prompt_kernel, tt=tt, alpha=alpha)
    tile = pl.BlockSpec((None, tt, D_MODEL), lambda b, j: (b, j, 0))
    names = ["w_c", "b_c", "lru_conv_w", "lru_conv_b", "w_ax", "lru_ba", "lru_bx", "lru_lambda",
             "w_pc", "w_out", "ln_g", "ln_b"]
    return pl.pallas_call(
        kern,
        grid=(bsz, t // tt),
        in_specs=[tile, tile] + [_at_layer(lw[n], layer) for n in names],
        out_specs=[
            tile,
            pl.BlockSpec((None, CONV_W - 1, D_MODEL), lambda b, j: (b, 0, 0)),
            pl.BlockSpec((None, V7X_SUBLANES, D_MODEL), lambda b, j: (b, 0, 0)),
        ],
        out_shape=[
            jax.ShapeDtypeStruct((bsz, t, D_MODEL), F32),
            jax.ShapeDtypeStruct((bsz, CONV_W - 1, D_MODEL), F32),
            jax.ShapeDtypeStruct((bsz, V7X_SUBLANES, D_MODEL), F32),
        ],
        scratch_shapes=[
            pltpu.VMEM((tt + V7X_SUBLANES, D_MODEL), F32),
            pltpu.VMEM((tt // V7X_SUBLANES, V7X_SUBLANES, D_MODEL), F32),
            pltpu.VMEM((tt // V7X_SUBLANES, V7X_SUBLANES, D_MODEL), F32),
            pltpu.VMEM((V7X_SUBLANES, D_MODEL), F32),
        ],
        compiler_params=pltpu.CompilerParams(
            dimension_semantics=("arbitrary", "arbitrary"), vmem_limit_bytes=VMEM_LIMIT_BYTES),
        name="lru_merge_prompt",
    )(x, merged, *[lw[n] for n in names])


def _proj_kernel(x_ref, w_ref, b_ref, o_ref):
    o_ref[...] = _dot(x_ref[...].astype(BF16), w_ref[...]) + b_ref[...]


def _proj(x, w, b, layer, col0, ncols, bn):
    rows = x.shape[0]
    off = col0 // bn
    return pl.pallas_call(
        _proj_kernel,
        grid=(ncols // bn,),
        in_specs=[
            pl.BlockSpec((rows, D_MODEL), lambda n: (0, 0)),
            pl.BlockSpec((None, D_MODEL, bn), lambda n: (layer, 0, off + n)),
            pl.BlockSpec((None, 1, bn), lambda n: (layer, 0, off + n)),
        ],
        out_specs=pl.BlockSpec((rows, bn), lambda n: (0, n)),
        out_shape=jax.ShapeDtypeStruct((rows, ncols), F32),
        compiler_params=pltpu.CompilerParams(
            dimension_semantics=("arbitrary",), vmem_limit_bytes=VMEM_LIMIT_BYTES),
        name="proj_sample",
    )(x, w, b)


def _mlstm_sample_kernel(p_ref, g_ref, c_ref, n_ref, m_ref, ng_ref,
                         y_ref, c_out, n_out, m_out, qc_s, *, bb, seq):
    rows = bb * seq
    t_idx = lax.broadcasted_iota(jnp.int32, (rows, 1), 0) % seq

    def down(x, d, fill=0.0):
        if d == 0:
            return x
        return jnp.where(t_idx >= d, pltpu.roll(x, d, 0), fill)

    def from_last(x):
        out = x
        for d in range(1, seq):
            out = jnp.where(t_idx == seq - 1 - d, pltpu.roll(x, rows - d, 0), out)
        return out

    it_all = g_ref[:, :V7X_LANES]
    lf_all = jax.nn.log_sigmoid(g_ref[:, V7X_LANES:])
    b_all = lf_all
    for d in range(1, seq):
        b_all = b_all + down(lf_all, d)
    g_all = it_all - b_all
    mx_all = g_all
    for d in range(1, seq):
        mx_all = jnp.maximum(mx_all, down(g_all, d, -jnp.inf))
    m0_all = m_ref[...]
    m_all = b_all + jnp.maximum(m0_all, mx_all)
    mlast_all = from_last(m_all)
    blast_all = from_last(b_all)
    inter_all = jnp.exp(b_all + m0_all - m_all)
    em_all = jnp.exp(-m_all)
    w_all = jnp.exp(blast_all + g_all - mlast_all)
    decay_all = jnp.exp(blast_all + m0_all - mlast_all)
    a_all = b_all - m_all
    m_out[...] = mlast_all

    for h in range(N_HEADS):
        hc = slice(h * D_HEAD, (h + 1) * D_HEAD)
        q = p_ref[:, h * D_HEAD:(h + 1) * D_HEAD]
        k = p_ref[:, D_MODEL + h * D_HEAD:D_MODEL + (h + 1) * D_HEAD] * K_SCALE
        v = p_ref[:, 2 * D_MODEL + h * D_HEAD:2 * D_MODEL + (h + 1) * D_HEAD]
        lane = slice(h, h + 1)
        a_col, g_col = a_all[:, lane], g_all[:, lane]
        inter, em = inter_all[:, lane], em_all[:, lane]
        wcol, decay = w_all[:, lane], decay_all[:, lane]

        kw = k * wcol
        per_slab = V7X_SUBLANES // seq
        grp = lax.broadcasted_iota(jnp.int32, (V7X_SUBLANES, 1), 0) // seq
        for slab in range(rows // V7X_SUBLANES):
            s0 = slab * V7X_SUBLANES
            q8 = q[s0:s0 + V7X_SUBLANES, :].astype(BF16)
            kw8 = kw[s0:s0 + V7X_SUBLANES, :]
            v8 = v[s0:s0 + V7X_SUBLANES, :].astype(BF16)
            qc8 = jnp.zeros((V7X_SUBLANES, D_HEAD), F32)
            for e in range(per_slab):
                b = slab * per_slab + e
                c0 = c_ref[b, h]
                qc8 = jnp.where(grp == e, _dot(q8, c0.astype(BF16)), qc8)
                kw_e = jnp.where(grp == e, kw8, 0.0).astype(BF16)
                c_out[b, h] = decay[b * seq:b * seq + 1, :] * c0 + _dot_tn(kw_e, v8)
            qc_s[s0:s0 + V7X_SUBLANES, :] = qc8

        n0 = n_ref[:, hc]
        num = inter * qc_s[...]
        den = inter * jnp.sum(q * n0, axis=1, keepdims=True)
        for d in range(seq):
            s_d = jnp.sum(q * down(k, d), axis=1, keepdims=True) * jnp.exp(a_col + down(g_col, d))
            s_d = jnp.where(t_idx >= d, s_d, 0.0)
            num = num + s_d * down(v, d)
            den = den + s_d
        hh = num / jnp.maximum(jnp.abs(den), em)

        ksum = kw
        for d in range(1, seq):
            ksum = ksum + down(kw, d)
        n_out[:, hc] = decay * n0 + ksum

        hh = jax.nn.sigmoid(p_ref[:, 3 * D_MODEL + h * D_HEAD:3 * D_MODEL + (h + 1) * D_HEAD]) * hh
        mu = jnp.mean(hh, axis=-1, keepdims=True)
        hcn = hh - mu
        var = jnp.mean(hcn * hcn, axis=-1, keepdims=True)
        hn = hcn * lax.rsqrt(var + LN_EPS) * ng_ref[:, hc]
        y_ref[:, hc] = hn * _silu(p_ref[:, 4 * D_MODEL + h * D_HEAD:4 * D_MODEL + (h + 1) * D_HEAD])


def _mlstm_sample(proj_a, gates, c0, n0_rows, m0_rows, lw, layer, bb, seq):
    nb = c0.shape[0]
    rows = bb * seq
    kern = functools.partial(_mlstm_sample_kernel, bb=bb, seq=seq)
    return pl.pallas_call(
        kern,
        grid=(nb // bb,),
        in_specs=[
            pl.BlockSpec((rows, 5 * D_MODEL), lambda i: (i, 0)),
            pl.BlockSpec((rows, 2 * V7X_LANES), lambda i: (i, 0)),
            pl.BlockSpec((bb, N_HEADS, D_HEAD, D_HEAD), lambda i: (i, 0, 0, 0)),
            pl.BlockSpec((rows, D_MODEL), lambda i: (i, 0)),
            pl.BlockSpec((rows, V7X_LANES), lambda i: (i, 0)),
            _at_layer(lw["norm_g"], layer),
        ],
        out_specs=[
            pl.BlockSpec((rows, D_MODEL), lambda i: (i, 0)),
            pl.BlockSpec((bb, N_HEADS, D_HEAD, D_HEAD), lambda i: (i, 0, 0, 0)),
            pl.BlockSpec((rows, D_MODEL), lambda i: (i, 0)),
            pl.BlockSpec((rows, V7X_LANES), lambda i: (i, 0)),
        ],
        out_shape=[
            jax.ShapeDtypeStruct((nb * seq, D_MODEL), F32),
            jax.ShapeDtypeStruct((nb, N_HEADS, D_HEAD, D_HEAD), F32),
            jax.ShapeDtypeStruct((nb * seq, D_MODEL), F32),
            jax.ShapeDtypeStruct((nb * seq, V7X_LANES), F32),
        ],
        scratch_shapes=[pltpu.VMEM((rows, D_HEAD), F32)],
        compiler_params=pltpu.CompilerParams(
            dimension_semantics=("arbitrary",), vmem_limit_bytes=VMEM_LIMIT_BYTES),
        name="mlstm_sample",
    )(proj_a, gates, c0, n0_rows, m0_rows, lw["norm_g"])


def _rest_sample_kernel(x_ref, ya_ref, ga_ref, pb_ref, pc_ref, buf_ref, h0_ref,
                        wsm_ref, bsm_ref,
                        lng1_ref, lnb1_ref, cw_ref, cb_ref, wax_ref, ba_ref, bx_ref, lam_ref,
                        wpa_ref, wpb_ref, wpc_ref, wo_ref, lng_ref, lnb_ref,
                        o_ref, vn_out, conv_out, h_out, *, seq, bbs, alpha):
    rows = seq * bbs

    def flat(ref, lo, n):
        return ref[:, :, lo:lo + n].reshape(rows, n)

    vn = _layer_norm(flat(pb_ref, D_MODEL, D_MODEL), lng1_ref[...], lnb1_ref[...])
    vn_out[...] = vn.reshape(seq, bbs, D_MODEL)
    mixed_t = []
    for t in range(seq):
        parts = []
        for g in range(N_GROUPS):
            cols = slice(g * D_GROUP, (g + 1) * D_GROUP)
            acc = jnp.full((bbs, D_GROUP), bsm_ref[g * seq + t], F32)
            for s in range(t + 1):
                acc = acc + wsm_ref[(g * seq + t) * seq + s] * vn[s * bbs:(s + 1) * bbs, cols]
            parts.append(acc)
        mixed_t.append(jnp.concatenate(parts, axis=1))
    mixed = jnp.concatenate(mixed_t, axis=0)
    yb = (flat(pb_ref, 0, D_MODEL) * mixed * _silu(flat(pb_ref, 2 * D_MODEL, D_MODEL))).astype(BF16)

    xc_raw = flat(pc_ref, 0, D_MODEL)
    xp = [buf_ref[t] for t in range(CONV_W - 1)] + [xc_raw[t * bbs:(t + 1) * bbs] for t in range(seq)]
    for t in range(CONV_W - 1):
        conv_out[t] = xp[seq + t]
    xc_t = []
    for t in range(seq):
        acc = cb_ref[...] + cw_ref[0:1, :] * xp[t]
        for jj in range(1, CONV_W):
            acc = acc + cw_ref[jj:jj + 1, :] * xp[t + jj]
        xc_t.append(acc)
    xc = jnp.concatenate(xc_t, axis=0)
    a, mult, i = _lru_gates(xc, wax_ref, ba_ref[...], bx_ref[...], lam_ref[...])
    bterm = mult * i * xc
    h = h0_ref[...]
    hs = []
    for t in range(seq):
        h = a[t * bbs:(t + 1) * bbs] * h + bterm[t * bbs:(t + 1) * bbs]
        hs.append(h)
    h_out[...] = h
    yc = (jnp.concatenate(hs, axis=0) * _silu(flat(pc_ref, D_MODEL, D_MODEL))).astype(BF16)

    ya = ya_ref[...].reshape(rows, D_MODEL).astype(BF16)
    merged = (jax.nn.sigmoid(ga_ref[...].reshape(rows, D_MODEL)) * _dot(ya, wpa_ref[...])
              + jax.nn.sigmoid(flat(pb_ref, 3 * D_MODEL, D_MODEL)) * _dot(yb, wpb_ref[...])
              + jax.nn.sigmoid(flat(pc_ref, 2 * D_MODEL, D_MODEL)) * _dot(yc, wpc_ref[...]))
    out = _dot(merged.astype(BF16), wo_ref[...])
    x = x_ref[...].reshape(rows, D_MODEL)
    o_ref[...] = _layer_norm(alpha * x + out, lng_ref[...], lnb_ref[...]).reshape(seq, bbs, D_MODEL)


def _rest_sample(x_tm, ya_tm, ga_tm, pb_tm, pc_tm, buf_tm, h0, lw, layer, bbs, alpha):
    seq, nb, _ = x_tm.shape
    kern = functools.partial(_rest_sample_kernel, seq=seq, bbs=bbs, alpha=alpha)

    def tm(n, lead=seq):
        return pl.BlockSpec((lead, bbs, n), lambda i: (0, i, 0))

    smem = pl.BlockSpec(memory_space=pltpu.SMEM)
    names = ["gmlp_ln_g", "gmlp_ln_b", "lru_conv_w", "lru_conv_b", "w_ax", "lru_ba", "lru_bx", "lru_lambda",
             "w_pa", "w_pb", "w_pc", "w_out", "ln_g", "ln_b"]
    return pl.pallas_call(
        kern,
        grid=(nb // bbs,),
        in_specs=[tm(D_MODEL), tm(D_MODEL), tm(D_MODEL), tm(4 * D_MODEL), tm(3 * D_MODEL),
                  tm(D_MODEL, CONV_W - 1), pl.BlockSpec((bbs, D_MODEL), lambda i: (i, 0)),
                  smem, smem] + [_at_layer(lw[n], layer) for n in names],
        out_specs=[tm(D_MODEL), tm(D_MODEL), tm(D_MODEL, CONV_W - 1),
                   pl.BlockSpec((bbs, D_MODEL), lambda i: (i, 0))],
        out_shape=[
            jax.ShapeDtypeStruct((seq, nb, D_MODEL), F32),
            jax.ShapeDtypeStruct((seq, nb, D_MODEL), F32),
            jax.ShapeDtypeStruct((CONV_W - 1, nb, D_MODEL), F32),
            jax.ShapeDtypeStruct((nb, D_MODEL), F32),
        ],
        compiler_params=pltpu.CompilerParams(
            dimension_semantics=("arbitrary",), vmem_limit_bytes=VMEM_LIMIT_BYTES),
        name="rest_sample",
    )(x_tm, ya_tm, ga_tm, pb_tm, pc_tm, buf_tm, h0, lw["ws_small"][layer], lw["bs_small"][layer],
      *[lw[n] for n in names])


def _pack_params(w_in, b_in, mlstm_norm_g, gmlp_ln_g, gmlp_ln_b, gmlp_ws, gmlp_bs, lru_conv_w, lru_conv_b,
                 lru_wa, lru_ba, lru_wx, lru_bx, lru_lambda, w_proj_a, w_proj_b, w_proj_c, w_out, ln_g, ln_b,
                 seq_s):
    depth = w_in.shape[0]
    d = D_MODEL
    o_i = 5 * d
    o_f = o_i + N_HEADS
    o_b = o_f + N_HEADS
    o_c = o_b + 3 * d
    o_g = o_c + 2 * d

    def cols(a, *ranges):
        return jnp.concatenate([a[..., lo:hi] for lo, hi in ranges], axis=-1)

    def row(a):
        return a[:, None, :]

    def pad_lanes(a, n):
        return jnp.pad(a, [(0, 0)] * (a.ndim - 1) + [(0, n - a.shape[-1])])

    wi, wf = w_in[..., o_i:o_f], w_in[..., o_f:o_b]
    bi, bf = b_in[..., o_i:o_f], b_in[..., o_f:o_b]
    pad_rows = [(0, 0), (0, V7X_SUBLANES - N_HEADS), (0, 0)]
    lw = {
        "w_a": cols(w_in, (0, 5 * d), (o_g, o_g + d)).astype(BF16),
        "b_a": row(cols(b_in, (0, 5 * d), (o_g, o_g + d))),
        "w_b": cols(w_in, (o_b, o_b + 3 * d), (o_g + d, o_g + 2 * d)).astype(BF16),
        "b_b": row(cols(b_in, (o_b, o_b + 3 * d), (o_g + d, o_g + 2 * d))),
        "w_c": cols(w_in, (o_c, o_c + 2 * d), (o_g + 2 * d, o_g + 3 * d)).astype(BF16),
        "b_c": row(cols(b_in, (o_c, o_c + 2 * d), (o_g + 2 * d, o_g + 3 * d))),
        "w_gc": jnp.concatenate([pad_lanes(wi, V7X_LANES), pad_lanes(wf, V7X_LANES)], axis=-1).astype(BF16),
        "b_gc": row(jnp.concatenate([pad_lanes(bi, V7X_LANES), pad_lanes(bf, V7X_LANES)], axis=-1)),
        "w_gr": jnp.concatenate([jnp.pad(jnp.swapaxes(wi, 1, 2), pad_rows),
                                 jnp.pad(jnp.swapaxes(wf, 1, 2), pad_rows)], axis=1).astype(BF16),
        "b_gr": jnp.concatenate([jnp.pad(bi[:, :, None], pad_rows),
                                 jnp.pad(bf[:, :, None], pad_rows)], axis=1),
        "norm_g": row(mlstm_norm_g),
        "gmlp_ln_g": row(gmlp_ln_g), "gmlp_ln_b": row(gmlp_ln_b),
        "gmlp_ws": gmlp_ws,
        "gmlp_bs_t": jnp.swapaxes(gmlp_bs, 1, 2),
        "ws_small": gmlp_ws[:, :, :seq_s, :seq_s].reshape(depth, -1),
        "bs_small": gmlp_bs[:, :, :seq_s].reshape(depth, -1),
        "lru_conv_w": lru_conv_w, "lru_conv_b": row(lru_conv_b),
        "w_ax": jnp.concatenate([lru_wa, lru_wx], axis=-1).astype(BF16),
        "lru_ba": row(lru_ba), "lru_bx": row(lru_bx), "lru_lambda": row(lru_lambda),
        "w_pa": w_proj_a.astype(BF16), "w_pb": w_proj_b.astype(BF16), "w_pc": w_proj_c.astype(BF16),
        "w_out": w_out.astype(BF16),
        "ln_g": row(ln_g), "ln_b": row(ln_b),
    }
    return lw


def _prompt_layer(x, lw, layer, alpha, tt_a, tt_b, tt_c):
    merged, c, n, m = _mlstm_prompt(x, lw, layer, tt_a)
    merged = _gmlp_prompt(x, merged, lw, layer, tt_b)
    x_new, conv, h = _lru_prompt(x, merged, lw, layer, tt_c, alpha)
    return x_new, c, n[:, :N_HEADS], m[:, 0, :N_HEADS], conv, h[:, 0]


def _sample_layer(x_bm, c0, n0, m0, conv_buf, h0, lw, layer, alpha, bb, bbs):
    nb, seq, d = x_bm.shape
    x_rows = x_bm.reshape(nb * seq, d)
    x_tm = jnp.swapaxes(x_bm, 0, 1)
    x_tm_rows = x_tm.reshape(seq * nb, d)
    proj_a = _proj(x_rows, lw["w_a"], lw["b_a"], layer, 0, 5 * d, d)
    gates = _proj(x_rows, lw["w_gc"], lw["b_gc"], layer, 0, 2 * V7X_LANES, 2 * V7X_LANES)
    ga_tm = _proj(x_tm_rows, lw["w_a"], lw["b_a"], layer, 5 * d, d, d).reshape(seq, nb, d)
    pb_tm = _proj(x_tm_rows, lw["w_b"], lw["b_b"], layer, 0, 4 * d, d).reshape(seq, nb, 4 * d)
    pc_tm = _proj(x_tm_rows, lw["w_c"], lw["b_c"], layer, 0, 3 * d, d).reshape(seq, nb, 3 * d)

    n0_rows = jnp.repeat(n0.reshape(nb, d), seq, axis=0)
    m0_rows = jnp.repeat(jnp.pad(m0, ((0, 0), (0, V7X_LANES - N_HEADS))), seq, axis=0)
    ya, c_new, n_rows, m_rows = _mlstm_sample(proj_a, gates, c0, n0_rows, m0_rows, lw, layer, bb, seq)
    n_new = n_rows.reshape(nb, seq, N_HEADS, D_HEAD)[:, seq - 1]
    m_new = m_rows.reshape(nb, seq, V7X_LANES)[:, seq - 1, :N_HEADS]

    ya_tm = jnp.swapaxes(ya.reshape(nb, seq, d), 0, 1)
    buf_tm = jnp.swapaxes(conv_buf, 0, 1)
    x_new_tm, vn_tm, conv_tm, h_new = _rest_sample(
        x_tm, ya_tm, ga_tm, pb_tm, pc_tm, buf_tm, h0, lw, layer, bbs, alpha)
    return (jnp.swapaxes(x_new_tm, 0, 1), c_new, n_new, m_new, jnp.swapaxes(conv_tm, 0, 1), h_new,
            jnp.swapaxes(vn_tm, 0, 1))


def kernel(x_prompt, x_sample, state_mlstm_c, state_mlstm_n, state_mlstm_m, state_lru_conv, state_lru_h,
           w_in, b_in, mlstm_norm_g, gmlp_ln_g, gmlp_ln_b, gmlp_ws, gmlp_bs, lru_conv_w, lru_conv_b,
           lru_wa, lru_ba, lru_wx, lru_bx, lru_lambda, w_proj_a, w_proj_b, w_proj_c, w_out, ln_g, ln_b):
    depth = w_in.shape[0]
    alpha = float((2 * depth) ** 0.25)
    seq_s = x_sample.shape[1]
    lw = _pack_params(w_in, b_in, mlstm_norm_g, gmlp_ln_g, gmlp_ln_b, gmlp_ws, gmlp_bs, lru_conv_w,
                      lru_conv_b, lru_wa, lru_ba, lru_wx, lru_bx, lru_lambda, w_proj_a, w_proj_b, w_proj_c,
                      w_out, ln_g, ln_b, seq_s)
    xp, xs = x_prompt, x_sample
    outs_p = [[] for _ in range(5)]
    outs_s = [[] for _ in range(6)]
    for layer in range(depth):
        xp, *st = _prompt_layer(xp, lw, layer, alpha, tt_a=512, tt_b=512, tt_c=256)
        for acc, val in zip(outs_p, st):
            acc.append(val)
        xs, *st = _sample_layer(xs, state_mlstm_c[layer], state_mlstm_n[layer], state_mlstm_m[layer],
                                state_lru_conv[layer], state_lru_h[layer], lw, layer, alpha, bb=8, bbs=64)
        for acc, val in zip(outs_s, st):
            acc.append(val)
    return (xp, xs, *[jnp.stack(v) for v in outs_p], *[jnp.stack(v) for v in outs_s])
```

```python
import functools

import jax
import jax.numpy as jnp
from jax import lax
from jax.experimental import pallas as pl
from jax.experimental.pallas import tpu as pltpu

D_MODEL = 1024
N_HEADS = 4
D_HEAD = D_MODEL // N_HEADS
CHUNK = 128
N_GROUPS = 4
D_GROUP = D_MODEL // N_GROUPS
LRU_BLOCKS = 8
LRU_BLK = D_MODEL // LRU_BLOCKS
CONV_W = 4
LRU_C = 8.0
LN_EPS = 1e-5
K_SCALE = D_HEAD ** -0.5
V7X_LANES = 128
V7X_SUBLANES = 8
VMEM_LIMIT_BYTES = 56 * 1024 * 1024

TT_MLSTM = 512
TT_GMLP = 512
TT_LRU = 256
BB_MLSTM_SAMPLE = 8
BB_REST_SAMPLE = 64

BF16 = jnp.bfloat16
F32 = jnp.float32

_NT = (((1,), (1,)), ((), ()))
_TN = (((0,), (0,)), ((), ()))


def _dot(a, b):
    return jnp.dot(a, b, preferred_element_type=F32)


def _dot_nt(a, b):
    return lax.dot_general(a, b, _NT, preferred_element_type=F32)


def _dot_tn(a, b):
    return lax.dot_general(a, b, _TN, preferred_element_type=F32)


def _layer_norm(x, g, b):
    mu = jnp.mean(x, axis=-1, keepdims=True)
    xc = x - mu
    var = jnp.mean(xc * xc, axis=-1, keepdims=True)
    return xc * lax.rsqrt(var + LN_EPS) * g + b


def _silu(x):
    return x * jax.nn.sigmoid(x)


def _scan_axis(x, axis, op, fill):
    n = x.shape[axis]
    idx = lax.broadcasted_iota(jnp.int32, x.shape, axis)
    s = 1
    while s < n:
        x = op(x, jnp.where(idx >= s, pltpu.roll(x, s, axis), fill))
        s *= 2
    return x


def _at_layer(arr, layer):
    shape = arr.shape[1:]
    nd = len(shape)
    return pl.BlockSpec((None,) + shape, lambda *_: (layer,) + (0,) * nd, pipeline_mode=pl.Buffered(1))


def _mlstm_prompt_kernel(x_ref, w_ref, b_ref, wkt_ref, bkt_ref, wgc_ref, bgc_ref, wgr_ref, bgr_ref,
                         ng_ref, wp_ref,
                         o_ref, c_out, n_out, m_out,
                         c_s, n_s, mc_s, mr_s, q_s, kt_s, v_s, o_s, h_s,
                         acol_s, inter_s, em_s, decay_s, grow_s, wrow_s, *, tt):
    j = pl.program_id(1)
    nch = tt // CHUNK

    @pl.when(j == 0)
    def _():
        c_s[...] = jnp.zeros_like(c_s)
        n_s[...] = jnp.zeros_like(n_s)
        mc_s[...] = jnp.zeros_like(mc_s)
        mr_s[...] = jnp.zeros_like(mr_s)

    xb = x_ref[...].astype(BF16)

    def col(block):
        lo = block * D_MODEL
        return _dot(xb, w_ref[:, lo:lo + D_MODEL]) + b_ref[:, lo:lo + D_MODEL]

    def project_q():
        q_s[...] = col(0).astype(BF16)

    def project_k():
        kt_s[...] = (_dot_nt(wkt_ref[...], xb) + bkt_ref[...]) * K_SCALE

    def project_v():
        v_s[...] = col(2).astype(BF16)

    def project_o():
        o_s[...] = jax.nn.sigmoid(col(3))

    projections = [project_q, project_k, project_v, project_o]

    gcol = _dot(xb, wgc_ref[...]) + bgc_ref[...]
    it_c = gcol[:, :V7X_LANES]
    lf_c = jax.nn.log_sigmoid(gcol[:, V7X_LANES:])
    grow = _dot_nt(wgr_ref[...], xb) + bgr_ref[...]
    it_r = grow[:V7X_SUBLANES]
    lf_r = jax.nn.log_sigmoid(grow[V7X_SUBLANES:])

    m0c = mc_s[0:1, :]
    m0r = mr_s[...]
    for c in range(nch):
        sl = slice(c * CHUNK, (c + 1) * CHUNK)
        bcol = _scan_axis(lf_c[sl], 0, jnp.add, 0.0)
        gcl = it_c[sl] - bcol
        mcol = bcol + jnp.maximum(m0c, _scan_axis(gcl, 0, jnp.maximum, -jnp.inf))
        blast = bcol[CHUNK - 1:CHUNK]
        mlast = mcol[CHUNK - 1:CHUNK]
        acol = bcol - mcol
        inter = jnp.exp(bcol + m0c - mcol)
        em = jnp.exp(-mcol)
        decay_s[c:c + 1, :] = jnp.exp(blast + m0c - mlast)
        for h in range(N_HEADS):
            acol_s[h, sl, :] = jnp.broadcast_to(acol[:, h:h + 1], (CHUNK, V7X_LANES))
            inter_s[h, sl, :] = jnp.broadcast_to(inter[:, h:h + 1], (CHUNK, V7X_LANES))
            em_s[h, sl, :] = jnp.broadcast_to(em[:, h:h + 1], (CHUNK, V7X_LANES))
        brow = _scan_axis(lf_r[:, sl], 1, jnp.add, 0.0)
        grw = it_r[:, sl] - brow
        mrow = brow + jnp.maximum(m0r, _scan_axis(grw, 1, jnp.maximum, -jnp.inf))
        mlast_r = mrow[:, CHUNK - 1:CHUNK]
        grow_s[:, sl] = grw
        wrow_s[:, sl] = jnp.exp(brow[:, CHUNK - 1:CHUNK] + grw - mlast_r)
        m0c = mlast
        m0r = jnp.broadcast_to(mlast_r, m0r.shape)
        if c < len(projections):
            projections[c]()
    for project in projections[nch:]:
        project()
    mc_s[...] = jnp.broadcast_to(m0c, mc_s.shape)
    mr_s[...] = m0r

    tri = (lax.broadcasted_iota(jnp.int32, (CHUNK, CHUNK), 0)
           >= lax.broadcasted_iota(jnp.int32, (CHUNK, CHUNK), 1))
    ones_rows = jnp.ones((V7X_SUBLANES, CHUNK), BF16)

    for c in range(nch):
        rows = slice(c * CHUNK, (c + 1) * CHUNK)
        decay_c = decay_s[c:c + 1, :]
        for h in range(N_HEADS):
            hc = slice(h * D_HEAD, (h + 1) * D_HEAD)
            qc = q_s[rows, hc]
            kt = kt_s[hc, rows]
            vc = v_s[rows, hc]
            inter = inter_s[h, rows, :]
            decay = decay_c[:, h:h + 1]

            dmat = jnp.exp(jnp.where(tri, acol_s[h, rows, :] + grow_s[h:h + 1, rows], -jnp.inf))
            s = _dot(qc, kt.astype(BF16)) * dmat
            c0 = c_s[h]
            n0 = n_s[h:h + 1, :]
            num = (jnp.concatenate([inter, inter], axis=1) * _dot(qc, c0.astype(BF16))
                   + _dot(s.astype(BF16), vc))
            qn = qc.astype(F32) * n0
            den = jnp.sum(inter * (qn[:, :V7X_LANES] + qn[:, V7X_LANES:]) + s, axis=1, keepdims=True)
            rden = 1.0 / jnp.maximum(jnp.abs(den), em_s[h, rows, :])
            h_s[rows, hc] = num * jnp.concatenate([rden, rden], axis=1)

            kwt = (kt * wrow_s[h:h + 1, rows]).astype(BF16)
            c_s[h] = decay * c0 + _dot(kwt, vc)
            n_s[h:h + 1, :] = decay * n0 + _dot_nt(ones_rows, kwt)[0:1, :]

    hh = o_s[...] * h_s[...]
    parts = []
    for h in range(N_HEADS):
        hc = slice(h * D_HEAD, (h + 1) * D_HEAD)
        hd = hh[:, hc]
        mu = jnp.mean(hd, axis=-1, keepdims=True)
        hd = hd - mu
        var = jnp.mean(hd * hd, axis=-1, keepdims=True)
        parts.append(hd * lax.rsqrt(var + LN_EPS) * ng_ref[:, hc])
    y = (jnp.concatenate(parts, axis=1) * _silu(col(4))).astype(BF16)
    o_ref[...] = jax.nn.sigmoid(col(5)) * _dot(y, wp_ref[...])

    @pl.when(j == pl.num_programs(1) - 1)
    def _():
        c_out[...] = c_s[...]
        n_out[...] = n_s[...]
        m_out[...] = mc_s[...]


def _mlstm_prompt(x, lw, layer):
    bsz, t, _ = x.shape
    tt = TT_MLSTM
    kern = functools.partial(_mlstm_prompt_kernel, tt=tt)
    tile = pl.BlockSpec((None, tt, D_MODEL), lambda b, j: (b, j, 0))
    names = ["w_a", "b_a", "w_kt", "b_kt", "w_gc", "b_gc", "w_gr", "b_gr", "norm_g", "w_pa"]
    head_lanes = (N_HEADS, tt, V7X_LANES)
    return pl.pallas_call(
        kern,
        grid=(bsz, t // tt),
        in_specs=[tile] + [_at_layer(lw[n], layer) for n in names],
        out_specs=[
            tile,
            pl.BlockSpec((None, N_HEADS, D_HEAD, D_HEAD), lambda b, j: (b, 0, 0, 0)),
            pl.BlockSpec((None, V7X_SUBLANES, D_HEAD), lambda b, j: (b, 0, 0)),
            pl.BlockSpec((None, V7X_SUBLANES, V7X_LANES), lambda b, j: (b, 0, 0)),
        ],
        out_shape=[
            jax.ShapeDtypeStruct((bsz, t, D_MODEL), F32),
            jax.ShapeDtypeStruct((bsz, N_HEADS, D_HEAD, D_HEAD), F32),
            jax.ShapeDtypeStruct((bsz, V7X_SUBLANES, D_HEAD), F32),
            jax.ShapeDtypeStruct((bsz, V7X_SUBLANES, V7X_LANES), F32),
        ],
        scratch_shapes=[
            pltpu.VMEM((N_HEADS, D_HEAD, D_HEAD), F32),
            pltpu.VMEM((V7X_SUBLANES, D_HEAD), F32),
            pltpu.VMEM((V7X_SUBLANES, V7X_LANES), F32),
            pltpu.VMEM((V7X_SUBLANES, V7X_LANES), F32),
            pltpu.VMEM((tt, D_MODEL), BF16),
            pltpu.VMEM((D_MODEL, tt), F32),
            pltpu.VMEM((tt, D_MODEL), BF16),
            pltpu.VMEM((tt, D_MODEL), F32),
            pltpu.VMEM((tt, D_MODEL), F32),
            pltpu.VMEM(head_lanes, F32),
            pltpu.VMEM(head_lanes, F32),
            pltpu.VMEM(head_lanes, F32),
            pltpu.VMEM((V7X_SUBLANES, V7X_LANES), F32),
            pltpu.VMEM((V7X_SUBLANES, tt), F32),
            pltpu.VMEM((V7X_SUBLANES, tt), F32),
        ],
        compiler_params=pltpu.CompilerParams(
            dimension_semantics=("arbitrary", "arbitrary"), vmem_limit_bytes=VMEM_LIMIT_BYTES),
        name="mlstm_prompt",
    )(x, *[lw[n] for n in names])


def _gmlp_prompt_kernel(x_ref, ma_ref, w_ref, b_ref, lng_ref, lnb_ref, ws_ref, bs_ref, wp_ref,
                        o_ref, vn_s, acc_s, *, tt):
    nch = tt // CHUNK
    xb = x_ref[...].astype(BF16)

    def col(block, g=None):
        if g is None:
            lo, n = block * D_MODEL, D_MODEL
        else:
            lo, n = block * D_MODEL + g * D_GROUP, D_GROUP
        return _dot(xb, w_ref[:, lo:lo + n]) + b_ref[:, lo:lo + n]

    vn_s[...] = _layer_norm(col(1), lng_ref[...], lnb_ref[...])
    tri = (lax.broadcasted_iota(jnp.int32, (CHUNK, CHUNK), 0)
           >= lax.broadcasted_iota(jnp.int32, (CHUNK, CHUNK), 1))
    for g in range(N_GROUPS):
        wm = jnp.where(tri, ws_ref[g], 0.0).astype(BF16)
        bias = bs_ref[:, g:g + 1]
        cols = slice(g * D_GROUP, (g + 1) * D_GROUP)
        mixed = jnp.concatenate(
            [_dot(wm, vn_s[c * CHUNK:(c + 1) * CHUNK, cols].astype(BF16)) + bias for c in range(nch)],
            axis=0)
        y = (col(0, g) * mixed * _silu(col(2, g))).astype(BF16)
        contrib = _dot(y, wp_ref[cols, :])
        if g == 0:
            acc_s[...] = contrib
        else:
            acc_s[...] += contrib
    o_ref[...] = ma_ref[...] + jax.nn.sigmoid(col(3)) * acc_s[...]


def _gmlp_prompt(x, merged, lw, layer):
    bsz, t, _ = x.shape
    tt = TT_GMLP
    kern = functools.partial(_gmlp_prompt_kernel, tt=tt)
    tile = pl.BlockSpec((None, tt, D_MODEL), lambda b, j: (b, j, 0))
    names = ["w_b", "b_b", "gmlp_ln_g", "gmlp_ln_b", "gmlp_ws", "gmlp_bs_t", "w_pb"]
    return pl.pallas_call(
        kern,
        grid=(bsz, t // tt),
        in_specs=[tile, tile] + [_at_layer(lw[n], layer) for n in names],
        out_specs=tile,
        out_shape=jax.ShapeDtypeStruct((bsz, t, D_MODEL), F32),
        scratch_shapes=[
            pltpu.VMEM((tt, D_MODEL), F32),
            pltpu.VMEM((tt, D_MODEL), F32),
        ],
        compiler_params=pltpu.CompilerParams(
            dimension_semantics=("arbitrary", "arbitrary"), vmem_limit_bytes=VMEM_LIMIT_BYTES),
        name="gmlp_prompt",
    )(x, merged, *[lw[n] for n in names])


def _lru_gates(xc, wax_ref, ba, bx, lam):
    xcb = xc.astype(BF16)
    pre = [_dot(xcb[:, n * LRU_BLK:(n + 1) * LRU_BLK], wax_ref[n]) for n in range(LRU_BLOCKS)]
    r = jax.nn.sigmoid(jnp.concatenate([p[:, :LRU_BLK] for p in pre], axis=1) + ba)
    i = jax.nn.sigmoid(jnp.concatenate([p[:, LRU_BLK:] for p in pre], axis=1) + bx)
    a = jnp.exp(LRU_C * r * jax.nn.log_sigmoid(lam))
    mult = jnp.sqrt(1.0 - a * a)
    return a, mult, i


def _lru_prompt_kernel(x_ref, mab_ref, w_ref, b_ref, cw_ref, cb_ref, wax_ref, ba_ref, bx_ref, lam_ref,
                       wp_ref, wo_ref, lng_ref, lnb_ref,
                       o_ref, conv_out, h_out,
                       xpad_s, a_s, b_s, hcar_s, *, tt, alpha):
    j = pl.program_id(1)
    ng = tt // V7X_SUBLANES

    @pl.when(j == 0)
    def _():
        xpad_s[0:V7X_SUBLANES, :] = jnp.zeros((V7X_SUBLANES, D_MODEL), F32)
        hcar_s[...] = jnp.zeros_like(hcar_s)

    x = x_ref[...]
    xb = x.astype(BF16)

    def col(block):
        lo = block * D_MODEL
        return _dot(xb, w_ref[:, lo:lo + D_MODEL]) + b_ref[:, lo:lo + D_MODEL]

    xpad_s[V7X_SUBLANES:, :] = col(0)
    xc = cb_ref[...] + cw_ref[0:1, :] * xpad_s[V7X_SUBLANES - 3:V7X_SUBLANES - 3 + tt, :]
    for jj in range(1, CONV_W):
        off = V7X_SUBLANES - 3 + jj
        xc = xc + cw_ref[jj:jj + 1, :] * xpad_s[off:off + tt, :]

    @pl.when(j == pl.num_programs(1) - 1)
    def _():
        conv_out[...] = xpad_s[V7X_SUBLANES + tt - (CONV_W - 1):, :]

    xpad_s[0:V7X_SUBLANES, :] = xpad_s[tt:tt + V7X_SUBLANES, :]

    a, mult, i = _lru_gates(xc, wax_ref, ba_ref[...], bx_ref[...], lam_ref[...])
    row = lax.broadcasted_iota(jnp.int32, (tt, 1), 0)
    mult = jnp.where(jnp.logical_and(row == 0, j == 0), 1.0, mult)
    bterm = mult * i * xc

    a3 = a.reshape(ng, V7X_SUBLANES, D_MODEL)
    b3 = bterm.reshape(ng, V7X_SUBLANES, D_MODEL)
    sub = lax.broadcasted_iota(jnp.int32, a3.shape, 1)
    s = 1
    while s < V7X_SUBLANES:
        keep = sub >= s
        a_sh = jnp.where(keep, pltpu.roll(a3, s, 1), 1.0)
        b_sh = jnp.where(keep, pltpu.roll(b3, s, 1), 0.0)
        b3 = a3 * b_sh + b3
        a3 = a3 * a_sh
        s *= 2
    a_s[...] = a3
    b_s[...] = b3

    def group_body(g, hprev):
        hg = a_s[g] * hprev + b_s[g]
        b_s[g] = hg
        return jnp.broadcast_to(hg[V7X_SUBLANES - 1:V7X_SUBLANES, :], hg.shape)

    hlast = lax.fori_loop(0, ng, group_body, hcar_s[...])
    hcar_s[...] = hlast
    hseq = b_s[...].reshape(tt, D_MODEL)

    @pl.when(j == pl.num_programs(1) - 1)
    def _():
        h_out[...] = hlast

    y = (hseq * _silu(col(1))).astype(BF16)
    merged = mab_ref[...] + jax.nn.sigmoid(col(2)) * _dot(y, wp_ref[...])
    out = _dot(merged.astype(BF16), wo_ref[...])
    o_ref[...] = _layer_norm(alpha * x + out, lng_ref[...], lnb_ref[...])


def _lru_prompt(x, merged, lw, layer, alpha):
    bsz, t, _ = x.shape
    tt = TT_LRU
    kern = functools.partial(_lru_prompt_kernel, tt=tt, alpha=alpha)
    tile = pl.BlockSpec((None, tt, D_MODEL), lambda b, j: (b, j, 0))
    names = ["w_c", "b_c", "lru_conv_w", "lru_conv_b", "w_ax", "lru_ba", "lru_bx", "lru_lambda",
             "w_pc", "w_out", "ln_g", "ln_b"]
    return pl.pallas_call(
        kern,
        grid=(bsz, t // tt),
        in_specs=[tile, tile] + [_at_layer(lw[n], layer) for n in names],
        out_specs=[
            tile,
            pl.BlockSpec((None, CONV_W - 1, D_MODEL), lambda b, j: (b, 0, 0)),
            pl.BlockSpec((None, V7X_SUBLANES, D_MODEL), lambda b, j: (b, 0, 0)),
        ],
        out_shape=[
            jax.ShapeDtypeStruct((bsz, t, D_MODEL), F32),
            jax.ShapeDtypeStruct((bsz, CONV_W - 1, D_MODEL), F32),
            jax.ShapeDtypeStruct((bsz, V7X_SUBLANES, D_MODEL), F32),
        ],
        scratch_shapes=[
            pltpu.VMEM((tt + V7X_SUBLANES, D_MODEL), F32),
            pltpu.VMEM((tt // V7X_SUBLANES, V7X_SUBLANES, D_MODEL), F32),
            pltpu.VMEM((tt // V7X_SUBLANES, V7X_SUBLANES, D_MODEL), F32),
            pltpu.VMEM((V7X_SUBLANES, D_MODEL), F32),
        ],
        compiler_params=pltpu.CompilerParams(
            dimension_semantics=("arbitrary", "arbitrary"), vmem_limit_bytes=VMEM_LIMIT_BYTES),
        name="lru_merge_prompt",
    )(x, merged, *[lw[n] for n in names])


def _proj_kernel(x_ref, w_ref, b_ref, o_ref):
    o_ref[...] = _dot(x_ref[...].astype(BF16), w_ref[...]) + b_ref[...]


def _proj(x, w, b, layer, col0, ncols, bn):
    rows = x.shape[0]
    off = col0 // bn
    return pl.pallas_call(
        _proj_kernel,
        grid=(ncols // bn,),
        in_specs=[
            pl.BlockSpec((rows, D_MODEL), lambda n: (0, 0)),
            pl.BlockSpec((None, D_MODEL, bn), lambda n: (layer, 0, off + n)),
            pl.BlockSpec((None, 1, bn), lambda n: (layer, 0, off + n)),
        ],
        out_specs=pl.BlockSpec((rows, bn), lambda n: (0, n)),
        out_shape=jax.ShapeDtypeStruct((rows, ncols), F32),
        compiler_params=pltpu.CompilerParams(
            dimension_semantics=("arbitrary",), vmem_limit_bytes=VMEM_LIMIT_BYTES),
        name="proj_sample",
    )(x, w, b)


def _mlstm_sample_kernel(p_ref, g_ref, c_ref, n_ref, m_ref, ng_ref, *rest, bb, seq):
    y_ref, c_out, n_out, m_out, qc_s = rest[-5:]
    rows = bb * seq
    t_idx = lax.broadcasted_iota(jnp.int32, (rows, 1), 0) % seq

    def down(x, d, fill=0.0):
        if d == 0:
            return x
        return jnp.where(t_idx >= d, pltpu.roll(x, d, 0), fill)

    def from_last(x):
        out = x
        for d in range(1, seq):
            out = jnp.where(t_idx == seq - 1 - d, pltpu.roll(x, rows - d, 0), out)
        return out

    it_all = g_ref[:, :V7X_LANES]
    lf_all = jax.nn.log_sigmoid(g_ref[:, V7X_LANES:])
    b_all = lf_all
    for d in range(1, seq):
        b_all = b_all + down(lf_all, d)
    g_all = it_all - b_all
    mx_all = g_all
    for d in range(1, seq):
        mx_all = jnp.maximum(mx_all, down(g_all, d, -jnp.inf))
    m0_all = m_ref[...]
    m_all = b_all + jnp.maximum(m0_all, mx_all)
    mlast_all = from_last(m_all)
    blast_all = from_last(b_all)
    inter_all = jnp.exp(b_all + m0_all - m_all)
    em_all = jnp.exp(-m_all)
    w_all = jnp.exp(blast_all + g_all - mlast_all)
    decay_all = jnp.exp(blast_all + m0_all - mlast_all)
    a_all = b_all - m_all
    m_out[...] = mlast_all

    for h in range(N_HEADS):
        hc = slice(h * D_HEAD, (h + 1) * D_HEAD)
        q = p_ref[:, h * D_HEAD:(h + 1) * D_HEAD]
        k = p_ref[:, D_MODEL + h * D_HEAD:D_MODEL + (h + 1) * D_HEAD] * K_SCALE
        v = p_ref[:, 2 * D_MODEL + h * D_HEAD:2 * D_MODEL + (h + 1) * D_HEAD]
        lane = slice(h, h + 1)
        a_col, g_col = a_all[:, lane], g_all[:, lane]
        inter, em = inter_all[:, lane], em_all[:, lane]
        wcol, decay = w_all[:, lane], decay_all[:, lane]

        kw = k * wcol
        per_slab = V7X_SUBLANES // seq
        grp = lax.broadcasted_iota(jnp.int32, (V7X_SUBLANES, 1), 0) // seq
        for slab in range(rows // V7X_SUBLANES):
            s0 = slab * V7X_SUBLANES
            q8 = q[s0:s0 + V7X_SUBLANES, :].astype(BF16)
            kw8 = kw[s0:s0 + V7X_SUBLANES, :]
            v8 = v[s0:s0 + V7X_SUBLANES, :].astype(BF16)
            qc8 = jnp.zeros((V7X_SUBLANES, D_HEAD), F32)
            for e in range(per_slab):
                b = slab * per_slab + e
                c0 = c_ref[b, h]
                qc8 = jnp.where(grp == e, _dot(q8, c0.astype(BF16)), qc8)
                kw_e = jnp.where(grp == e, kw8, 0.0).astype(BF16)
                c_out[b, h] = decay[b * seq:b * seq + 1, :] * c0 + _dot_tn(kw_e, v8)
            qc_s[s0:s0 + V7X_SUBLANES, :] = qc8

        n0 = n_ref[:, hc]
        num = inter * qc_s[...]
        den = inter * jnp.sum(q * n0, axis=1, keepdims=True)
        for d in range(seq):
            s_d = jnp.sum(q * down(k, d), axis=1, keepdims=True) * jnp.exp(a_col + down(g_col, d))
            s_d = jnp.where(t_idx >= d, s_d, 0.0)
            num = num + s_d * down(v, d)
            den = den + s_d
        hh = num / jnp.maximum(jnp.abs(den), em)

        ksum = kw
        for d in range(1, seq):
            ksum = ksum + down(kw, d)
        n_out[:, hc] = decay * n0 + ksum

        hh = jax.nn.sigmoid(p_ref[:, 3 * D_MODEL + h * D_HEAD:3 * D_MODEL + (h + 1) * D_HEAD]) * hh
        mu = jnp.mean(hh, axis=-1, keepdims=True)
        hcn = hh - mu
        var = jnp.mean(hcn * hcn, axis=-1, keepdims=True)
        hn = hcn * lax.rsqrt(var + LN_EPS) * ng_ref[:, hc]
        y_ref[:, hc] = hn * _silu(p_ref[:, 4 * D_MODEL + h * D_HEAD:4 * D_MODEL + (h + 1) * D_HEAD])


def _mlstm_sample(proj_a, gates, c_all, c_new_all, n0_rows, m0_rows, lw, layer, seq):
    nb = c_all.shape[1]
    bb = BB_MLSTM_SAMPLE
    rows = bb * seq
    kern = functools.partial(_mlstm_sample_kernel, bb=bb, seq=seq)
    c_spec = pl.BlockSpec((None, bb, N_HEADS, D_HEAD, D_HEAD), lambda i: (layer, i, 0, 0, 0))
    in_specs = [
        pl.BlockSpec((rows, 5 * D_MODEL), lambda i: (i, 0)),
        pl.BlockSpec((rows, 2 * V7X_LANES), lambda i: (i, 0)),
        c_spec,
        pl.BlockSpec((rows, D_MODEL), lambda i: (i, 0)),
        pl.BlockSpec((rows, V7X_LANES), lambda i: (i, 0)),
        _at_layer(lw["norm_g"], layer),
    ]
    args = [proj_a, gates, c_all, n0_rows, m0_rows, lw["norm_g"]]
    aliases = {}
    if c_new_all is not None:
        in_specs.append(pl.BlockSpec(memory_space=pl.ANY))
        args.append(c_new_all)
        aliases = {len(args) - 1: 1}
    return pl.pallas_call(
        kern,
        grid=(nb // bb,),
        in_specs=in_specs,
        out_specs=[
            pl.BlockSpec((rows, D_MODEL), lambda i: (i, 0)),
            c_spec,
            pl.BlockSpec((rows, D_MODEL), lambda i: (i, 0)),
            pl.BlockSpec((rows, V7X_LANES), lambda i: (i, 0)),
        ],
        out_shape=[
            jax.ShapeDtypeStruct((nb * seq, D_MODEL), F32),
            jax.ShapeDtypeStruct(c_all.shape, F32),
            jax.ShapeDtypeStruct((nb * seq, D_MODEL), F32),
            jax.ShapeDtypeStruct((nb * seq, V7X_LANES), F32),
        ],
        scratch_shapes=[pltpu.VMEM((rows, D_HEAD), F32)],
        input_output_aliases=aliases,
        compiler_params=pltpu.CompilerParams(
            dimension_semantics=("arbitrary",), vmem_limit_bytes=VMEM_LIMIT_BYTES),
        name="mlstm_sample",
    )(*args)


def _rest_sample_kernel(x_ref, ya_ref, ga_ref, pb_ref, pc_ref, buf_ref, h0_ref,
                        wsm_ref, bsm_ref,
                        lng1_ref, lnb1_ref, cw_ref, cb_ref, wax_ref, ba_ref, bx_ref, lam_ref,
                        wpa_ref, wpb_ref, wpc_ref, wo_ref, lng_ref, lnb_ref,
                        o_ref, vn_out, conv_out, h_out, *, seq, bbs, alpha):
    rows = seq * bbs

    def flat(ref, lo, n):
        return ref[:, :, lo:lo + n].reshape(rows, n)

    vn = _layer_norm(flat(pb_ref, D_MODEL, D_MODEL), lng1_ref[...], lnb1_ref[...])
    vn_out[...] = vn.reshape(seq, bbs, D_MODEL)
    mixed_t = []
    for t in range(seq):
        parts = []
        for g in range(N_GROUPS):
            cols = slice(g * D_GROUP, (g + 1) * D_GROUP)
            acc = jnp.full((bbs, D_GROUP), bsm_ref[g * seq + t], F32)
            for s in range(t + 1):
                acc = acc + wsm_ref[(g * seq + t) * seq + s] * vn[s * bbs:(s + 1) * bbs, cols]
            parts.append(acc)
        mixed_t.append(jnp.concatenate(parts, axis=1))
    mixed = jnp.concatenate(mixed_t, axis=0)
    yb = (flat(pb_ref, 0, D_MODEL) * mixed * _silu(flat(pb_ref, 2 * D_MODEL, D_MODEL))).astype(BF16)

    xc_raw = flat(pc_ref, 0, D_MODEL)
    xp = [buf_ref[t] for t in range(CONV_W - 1)] + [xc_raw[t * bbs:(t + 1) * bbs] for t in range(seq)]
    for t in range(CONV_W - 1):
        conv_out[t] = xp[seq + t]
    xc_t = []
    for t in range(seq):
        acc = cb_ref[...] + cw_ref[0:1, :] * xp[t]
        for jj in range(1, CONV_W):
            acc = acc + cw_ref[jj:jj + 1, :] * xp[t + jj]
        xc_t.append(acc)
    xc = jnp.concatenate(xc_t, axis=0)
    a, mult, i = _lru_gates(xc, wax_ref, ba_ref[...], bx_ref[...], lam_ref[...])
    bterm = mult * i * xc
    h = h0_ref[...]
    hs = []
    for t in range(seq):
        h = a[t * bbs:(t + 1) * bbs] * h + bterm[t * bbs:(t + 1) * bbs]
        hs.append(h)
    h_out[...] = h
    yc = (jnp.concatenate(hs, axis=0) * _silu(flat(pc_ref, D_MODEL, D_MODEL))).astype(BF16)

    ya = ya_ref[...].reshape(rows, D_MODEL).astype(BF16)
    merged = (jax.nn.sigmoid(ga_ref[...].reshape(rows, D_MODEL)) * _dot(ya, wpa_ref[...])
              + jax.nn.sigmoid(flat(pb_ref, 3 * D_MODEL, D_MODEL)) * _dot(yb, wpb_ref[...])
              + jax.nn.sigmoid(flat(pc_ref, 2 * D_MODEL, D_MODEL)) * _dot(yc, wpc_ref[...]))
    out = _dot(merged.astype(BF16), wo_ref[...])
    x = x_ref[...].reshape(rows, D_MODEL)
    o_ref[...] = _layer_norm(alpha * x + out, lng_ref[...], lnb_ref[...]).reshape(seq, bbs, D_MODEL)


def _rest_sample(x_tm, ya_tm, ga_tm, pb_tm, pc_tm, buf_tm, h0, lw, layer, alpha):
    seq, nb, _ = x_tm.shape
    bbs = min(BB_REST_SAMPLE, nb)
    kern = functools.partial(_rest_sample_kernel, seq=seq, bbs=bbs, alpha=alpha)

    def tm(n, lead=seq):
        return pl.BlockSpec((lead, bbs, n), lambda i: (0, i, 0))

    smem = pl.BlockSpec(memory_space=pltpu.SMEM)
    names = ["gmlp_ln_g", "gmlp_ln_b", "lru_conv_w", "lru_conv_b", "w_ax", "lru_ba", "lru_bx", "lru_lambda",
             "w_pa", "w_pb", "w_pc", "w_out", "ln_g", "ln_b"]
    return pl.pallas_call(
        kern,
        grid=(nb // bbs,),
        in_specs=[tm(D_MODEL), tm(D_MODEL), tm(D_MODEL), tm(4 * D_MODEL), tm(3 * D_MODEL),
                  tm(D_MODEL, CONV_W - 1), pl.BlockSpec((bbs, D_MODEL), lambda i: (i, 0)),
                  smem, smem] + [_at_layer(lw[n], layer) for n in names],
        out_specs=[tm(D_MODEL), tm(D_MODEL), tm(D_MODEL, CONV_W - 1),
                   pl.BlockSpec((bbs, D_MODEL), lambda i: (i, 0))],
        out_shape=[
            jax.ShapeDtypeStruct((seq, nb, D_MODEL), F32),
            jax.ShapeDtypeStruct((seq, nb, D_MODEL), F32),
            jax.ShapeDtypeStruct((CONV_W - 1, nb, D_MODEL), F32),
            jax.ShapeDtypeStruct((nb, D_MODEL), F32),
        ],
        compiler_params=pltpu.CompilerParams(
            dimension_semantics=("arbitrary",), vmem_limit_bytes=VMEM_LIMIT_BYTES),
        name="rest_sample",
    )(x_tm, ya_tm, ga_tm, pb_tm, pc_tm, buf_tm, h0, lw["ws_small"][layer], lw["bs_small"][layer],
      *[lw[n] for n in names])


def _pack_params(w_in, b_in, mlstm_norm_g, gmlp_ln_g, gmlp_ln_b, gmlp_ws, gmlp_bs, lru_conv_w, lru_conv_b,
                 lru_wa, lru_ba, lru_wx, lru_bx, lru_lambda, w_proj_a, w_proj_b, w_proj_c, w_out, ln_g, ln_b,
                 seq_s):
    depth = w_in.shape[0]
    d = D_MODEL
    o_i = 5 * d
    o_f = o_i + N_HEADS
    o_b = o_f + N_HEADS
    o_c = o_b + 3 * d
    o_g = o_c + 2 * d

    def cols(a, *ranges):
        return jnp.concatenate([a[..., lo:hi] for lo, hi in ranges], axis=-1)

    def row(a):
        return a[:, None, :]

    def pad_lanes(a, n):
        return jnp.pad(a, [(0, 0)] * (a.ndim - 1) + [(0, n - a.shape[-1])])

    wi, wf = w_in[..., o_i:o_f], w_in[..., o_f:o_b]
    bi, bf = b_in[..., o_i:o_f], b_in[..., o_f:o_b]
    pad_rows = [(0, 0), (0, V7X_SUBLANES - N_HEADS), (0, 0)]
    lw = {
        "w_a": cols(w_in, (0, 5 * d), (o_g, o_g + d)).astype(BF16),
        "b_a": row(cols(b_in, (0, 5 * d), (o_g, o_g + d))),
        "w_kt": jnp.swapaxes(w_in[..., d:2 * d], 1, 2).astype(BF16),
        "b_kt": b_in[:, d:2 * d, None],
        "w_b": cols(w_in, (o_b, o_b + 3 * d), (o_g + d, o_g + 2 * d)).astype(BF16),
        "b_b": row(cols(b_in, (o_b, o_b + 3 * d), (o_g + d, o_g + 2 * d))),
        "w_c": cols(w_in, (o_c, o_c + 2 * d), (o_g + 2 * d, o_g + 3 * d)).astype(BF16),
        "b_c": row(cols(b_in, (o_c, o_c + 2 * d), (o_g + 2 * d, o_g + 3 * d))),
        "w_gc": jnp.concatenate([pad_lanes(wi, V7X_LANES), pad_lanes(wf, V7X_LANES)], axis=-1).astype(BF16),
        "b_gc": row(jnp.concatenate([pad_lanes(bi, V7X_LANES), pad_lanes(bf, V7X_LANES)], axis=-1)),
        "w_gr": jnp.concatenate([jnp.pad(jnp.swapaxes(wi, 1, 2), pad_rows),
                                 jnp.pad(jnp.swapaxes(wf, 1, 2), pad_rows)], axis=1).astype(BF16),
        "b_gr": jnp.concatenate([jnp.pad(bi[:, :, None], pad_rows),
                                 jnp.pad(bf[:, :, None], pad_rows)], axis=1),
        "norm_g": row(mlstm_norm_g),
        "gmlp_ln_g": row(gmlp_ln_g), "gmlp_ln_b": row(gmlp_ln_b),
        "gmlp_ws": gmlp_ws,
        "gmlp_bs_t": jnp.swapaxes(gmlp_bs, 1, 2),
        "ws_small": gmlp_ws[:, :, :seq_s, :seq_s].reshape(depth, -1),
        "bs_small": gmlp_bs[:, :, :seq_s].reshape(depth, -1),
        "lru_conv_w": lru_conv_w, "lru_conv_b": row(lru_conv_b),
        "w_ax": jnp.concatenate([lru_wa, lru_wx], axis=-1).astype(BF16),
        "lru_ba": row(lru_ba), "lru_bx": row(lru_bx), "lru_lambda": row(lru_lambda),
        "w_pa": w_proj_a.astype(BF16), "w_pb": w_proj_b.astype(BF16), "w_pc": w_proj_c.astype(BF16),
        "w_out": w_out.astype(BF16),
        "ln_g": row(ln_g), "ln_b": row(ln_b),
    }
    return lw


def _prompt_layer(x, lw, layer, alpha):
    merged, c, n, m = _mlstm_prompt(x, lw, layer)
    merged = _gmlp_prompt(x, merged, lw, layer)
    x_new, conv, h = _lru_prompt(x, merged, lw, layer, alpha)
    return x_new, c, n[:, :N_HEADS], m[:, 0, :N_HEADS], conv, h[:, 0]


def _sample_layer(x_bm, c_all, c_new_all, n0, m0, conv_buf, h0, lw, layer, alpha):
    nb, seq, d = x_bm.shape
    x_rows = x_bm.reshape(nb * seq, d)
    x_tm = jnp.swapaxes(x_bm, 0, 1)
    x_tm_rows = x_tm.reshape(seq * nb, d)
    proj_a = _proj(x_rows, lw["w_a"], lw["b_a"], layer, 0, 5 * d, d)
    gates = _proj(x_rows, lw["w_gc"], lw["b_gc"], layer, 0, 2 * V7X_LANES, 2 * V7X_LANES)
    ga_tm = _proj(x_tm_rows, lw["w_a"], lw["b_a"], layer, 5 * d, d, d).reshape(seq, nb, d)
    pb_tm = _proj(x_tm_rows, lw["w_b"], lw["b_b"], layer, 0, 4 * d, d).reshape(seq, nb, 4 * d)
    pc_tm = _proj(x_tm_rows, lw["w_c"], lw["b_c"], layer, 0, 3 * d, d).reshape(seq, nb, 3 * d)

    n0_rows = jnp.repeat(n0.reshape(nb, d), seq, axis=0)
    m0_rows = jnp.repeat(jnp.pad(m0, ((0, 0), (0, V7X_LANES - N_HEADS))), seq, axis=0)
    ya, c_new_all, n_rows, m_rows = _mlstm_sample(
        proj_a, gates, c_all, c_new_all, n0_rows, m0_rows, lw, layer, seq)
    n_new = n_rows.reshape(nb, seq, N_HEADS, D_HEAD)[:, seq - 1]
    m_new = m_rows.reshape(nb, seq, V7X_LANES)[:, seq - 1, :N_HEADS]

    ya_tm = jnp.swapaxes(ya.reshape(nb, seq, d), 0, 1)
    buf_tm = jnp.swapaxes(conv_buf, 0, 1)
    x_new_tm, vn_tm, conv_tm, h_new = _rest_sample(
        x_tm, ya_tm, ga_tm, pb_tm, pc_tm, buf_tm, h0, lw, layer, alpha)
    return (jnp.swapaxes(x_new_tm, 0, 1), c_new_all, n_new, m_new, jnp.swapaxes(conv_tm, 0, 1), h_new,
            jnp.swapaxes(vn_tm, 0, 1))


def kernel(x_prompt, x_sample, state_mlstm_c, state_mlstm_n, state_mlstm_m, state_lru_conv, state_lru_h,
           w_in, b_in, mlstm_norm_g, gmlp_ln_g, gmlp_ln_b, gmlp_ws, gmlp_bs, lru_conv_w, lru_conv_b,
           lru_wa, lru_ba, lru_wx, lru_bx, lru_lambda, w_proj_a, w_proj_b, w_proj_c, w_out, ln_g, ln_b):
    depth = w_in.shape[0]
    alpha = float((2 * depth) ** 0.25)
    seq_s = x_sample.shape[1]
    assert V7X_SUBLANES % seq_s == 0 and x_prompt.shape[1] % TT_MLSTM == 0
    lw = _pack_params(w_in, b_in, mlstm_norm_g, gmlp_ln_g, gmlp_ln_b, gmlp_ws, gmlp_bs, lru_conv_w,
                      lru_conv_b, lru_wa, lru_ba, lru_wx, lru_bx, lru_lambda, w_proj_a, w_proj_b, w_proj_c,
                      w_out, ln_g, ln_b, seq_s)
    xp, xs = x_prompt, x_sample
    outs_p = [[] for _ in range(5)]
    outs_s = [[] for _ in range(5)]
    c_new_all = None
    for layer in range(depth):
        xp, *st = _prompt_layer(xp, lw, layer, alpha)
        for acc, val in zip(outs_p, st):
            acc.append(val)
        xs, c_new_all, *st = _sample_layer(
            xs, state_mlstm_c, c_new_all, state_mlstm_n[layer], state_mlstm_m[layer],
            state_lru_conv[layer], state_lru_h[layer], lw, layer, alpha)
        for acc, val in zip(outs_s, st):
            acc.append(val)
    stacked_p = [jnp.stack(v) for v in outs_p]
    stacked_s = [jnp.stack(v) for v in outs_s]
    return (xp, xs, *stacked_p, c_new_all, *stacked_s)
```

```python
import functools

import jax
import jax.numpy as jnp
from jax import lax
from jax.experimental import pallas as pl
from jax.experimental.pallas import tpu as pltpu

D_MODEL = 1024
N_HEADS = 4
D_HEAD = D_MODEL // N_HEADS
CHUNK = 128
N_GROUPS = 4
D_GROUP = D_MODEL // N_GROUPS
LRU_BLOCKS = 8
LRU_BLK = D_MODEL // LRU_BLOCKS
CONV_W = 4
LRU_C = 8.0
LN_EPS = 1e-5
K_SCALE = D_HEAD ** -0.5
V7X_LANES = 128
V7X_SUBLANES = 8
VMEM_LIMIT_BYTES = 56 * 1024 * 1024

TT_MLSTM = 512
TT_GMLP = 512
TT_LRU = 256
TT_MIX = 512
BB_MLSTM_SAMPLE = 8
BB_REST_SAMPLE = 64

BF16 = jnp.bfloat16
F32 = jnp.float32

_NT = (((1,), (1,)), ((), ()))
_TN = (((0,), (0,)), ((), ()))


def _dot(a, b):
    return jnp.dot(a, b, preferred_element_type=F32)


def _dot_nt(a, b):
    return lax.dot_general(a, b, _NT, preferred_element_type=F32)


def _dot_tn(a, b):
    return lax.dot_general(a, b, _TN, preferred_element_type=F32)


def _layer_norm(x, g, b):
    mu = jnp.mean(x, axis=-1, keepdims=True)
    xc = x - mu
    var = jnp.mean(xc * xc, axis=-1, keepdims=True)
    return xc * lax.rsqrt(var + LN_EPS) * g + b


def _silu(x):
    return x * jax.nn.sigmoid(x)


def _scan_axis(x, axis, op, fill):
    n = x.shape[axis]
    idx = lax.broadcasted_iota(jnp.int32, x.shape, axis)
    s = 1
    while s < n:
        x = op(x, jnp.where(idx >= s, pltpu.roll(x, s, axis), fill))
        s *= 2
    return x


def _at_layer(arr, layer):
    shape = arr.shape[1:]
    nd = len(shape)
    return pl.BlockSpec((None,) + shape, lambda *_: (layer,) + (0,) * nd, pipeline_mode=pl.Buffered(1))


def _mlstm_prompt_kernel(x_ref, w_ref, b_ref, wkt_ref, bkt_ref, wgc_ref, bgc_ref, wgr_ref, bgr_ref,
                         ng_ref, wp_ref,
                         o_ref, c_out, n_out, m_out,
                         c_s, n_s, mc_s, mr_s, q_s, kt_s, v_s, o_s, h_s,
                         acol_s, inter_s, em_s, decay_s, grow_s, wrow_s, *, tt):
    j = pl.program_id(1)
    nch = tt // CHUNK

    @pl.when(j == 0)
    def _():
        c_s[...] = jnp.zeros_like(c_s)
        n_s[...] = jnp.zeros_like(n_s)
        mc_s[...] = jnp.zeros_like(mc_s)
        mr_s[...] = jnp.zeros_like(mr_s)

    xb = x_ref[...].astype(BF16)

    def col(block):
        lo = block * D_MODEL
        return _dot(xb, w_ref[:, lo:lo + D_MODEL]) + b_ref[:, lo:lo + D_MODEL]

    def project_q():
        q_s[...] = col(0).astype(BF16)

    def project_k():
        kt_s[...] = (_dot_nt(wkt_ref[...], xb) + bkt_ref[...]) * K_SCALE

    def project_v():
        v_s[...] = col(2).astype(BF16)

    def project_o():
        o_s[...] = jax.nn.sigmoid(col(3))

    projections = [project_q, project_k, project_v, project_o]

    gcol = _dot(xb, wgc_ref[...]) + bgc_ref[...]
    it_c = gcol[:, :V7X_LANES]
    lf_c = jax.nn.log_sigmoid(gcol[:, V7X_LANES:])
    grow = _dot_nt(wgr_ref[...], xb) + bgr_ref[...]
    it_r = grow[:V7X_SUBLANES]
    lf_r = jax.nn.log_sigmoid(grow[V7X_SUBLANES:])

    m0c = mc_s[0:1, :]
    m0r = mr_s[...]
    for c in range(nch):
        sl = slice(c * CHUNK, (c + 1) * CHUNK)
        bcol = _scan_axis(lf_c[sl], 0, jnp.add, 0.0)
        gcl = it_c[sl] - bcol
        mcol = bcol + jnp.maximum(m0c, _scan_axis(gcl, 0, jnp.maximum, -jnp.inf))
        blast = bcol[CHUNK - 1:CHUNK]
        mlast = mcol[CHUNK - 1:CHUNK]
        acol = bcol - mcol
        inter = jnp.exp(bcol + m0c - mcol)
        em = jnp.exp(-mcol)
        decay_s[c:c + 1, :] = jnp.exp(blast + m0c - mlast)
        for h in range(N_HEADS):
            acol_s[h, sl, :] = jnp.broadcast_to(acol[:, h:h + 1], (CHUNK, V7X_LANES))
            inter_s[h, sl, :] = jnp.broadcast_to(inter[:, h:h + 1], (CHUNK, V7X_LANES))
            em_s[h, sl, :] = jnp.broadcast_to(em[:, h:h + 1], (CHUNK, V7X_LANES))
        brow = _scan_axis(lf_r[:, sl], 1, jnp.add, 0.0)
        grw = it_r[:, sl] - brow
        mrow = brow + jnp.maximum(m0r, _scan_axis(grw, 1, jnp.maximum, -jnp.inf))
        mlast_r = mrow[:, CHUNK - 1:CHUNK]
        grow_s[:, sl] = grw
        wrow_s[:, sl] = jnp.exp(brow[:, CHUNK - 1:CHUNK] + grw - mlast_r)
        m0c = mlast
        m0r = jnp.broadcast_to(mlast_r, m0r.shape)
        if c < len(projections):
            projections[c]()
    for project in projections[nch:]:
        project()
    mc_s[...] = jnp.broadcast_to(m0c, mc_s.shape)
    mr_s[...] = m0r

    tri = (lax.broadcasted_iota(jnp.int32, (CHUNK, CHUNK), 0)
           >= lax.broadcasted_iota(jnp.int32, (CHUNK, CHUNK), 1))
    ones_rows = jnp.ones((V7X_SUBLANES, CHUNK), BF16)

    for c in range(nch):
        rows = slice(c * CHUNK, (c + 1) * CHUNK)
        decay_c = decay_s[c:c + 1, :]
        for h in range(N_HEADS):
            hc = slice(h * D_HEAD, (h + 1) * D_HEAD)
            qc = q_s[rows, hc]
            kt = kt_s[hc, rows]
            vc = v_s[rows, hc]
            inter = inter_s[h, rows, :]
            decay = decay_c[:, h:h + 1]

            dmat = jnp.exp(jnp.where(tri, acol_s[h, rows, :] + grow_s[h:h + 1, rows], -jnp.inf))
            s = _dot(qc, kt.astype(BF16)) * dmat
            c0 = c_s[h]
            n0 = n_s[h:h + 1, :]
            num = (jnp.concatenate([inter, inter], axis=1) * _dot(qc, c0.astype(BF16))
                   + _dot(s.astype(BF16), vc))
            qn = qc.astype(F32) * n0
            den = jnp.sum(inter * (qn[:, :V7X_LANES] + qn[:, V7X_LANES:]) + s, axis=1, keepdims=True)
            rden = 1.0 / jnp.maximum(jnp.abs(den), em_s[h, rows, :])
            h_s[rows, hc] = num * jnp.concatenate([rden, rden], axis=1)

            kwt = (kt * wrow_s[h:h + 1, rows]).astype(BF16)
            c_s[h] = decay * c0 + _dot(kwt, vc)
            n_s[h:h + 1, :] = decay * n0 + _dot_nt(ones_rows, kwt)[0:1, :]

    hh = o_s[...] * h_s[...]
    parts = []
    for h in range(N_HEADS):
        hc = slice(h * D_HEAD, (h + 1) * D_HEAD)
        hd = hh[:, hc]
        mu = jnp.mean(hd, axis=-1, keepdims=True)
        hd = hd - mu
        var = jnp.mean(hd * hd, axis=-1, keepdims=True)
        parts.append(hd * lax.rsqrt(var + LN_EPS) * ng_ref[:, hc])
    y = (jnp.concatenate(parts, axis=1) * _silu(col(4))).astype(BF16)
    o_ref[...] = jax.nn.sigmoid(col(5)) * _dot(y, wp_ref[...])

    @pl.when(j == pl.num_programs(1) - 1)
    def _():
        c_out[...] = c_s[...]
        n_out[...] = n_s[...]
        m_out[...] = mc_s[...]


def _mlstm_prompt(x, lw, layer):
    bsz, t, _ = x.shape
    tt = TT_MLSTM
    kern = functools.partial(_mlstm_prompt_kernel, tt=tt)
    tile = pl.BlockSpec((None, tt, D_MODEL), lambda b, j: (b, j, 0))
    names = ["w_a", "b_a", "w_kt", "b_kt", "w_gc", "b_gc", "w_gr", "b_gr", "norm_g", "w_pa"]
    head_lanes = (N_HEADS, tt, V7X_LANES)
    return pl.pallas_call(
        kern,
        grid=(bsz, t // tt),
        in_specs=[tile] + [_at_layer(lw[n], layer) for n in names],
        out_specs=[
            tile,
            pl.BlockSpec((None, N_HEADS, D_HEAD, D_HEAD), lambda b, j: (b, 0, 0, 0)),
            pl.BlockSpec((None, V7X_SUBLANES, D_HEAD), lambda b, j: (b, 0, 0)),
            pl.BlockSpec((None, V7X_SUBLANES, V7X_LANES), lambda b, j: (b, 0, 0)),
        ],
        out_shape=[
            jax.ShapeDtypeStruct((bsz, t, D_MODEL), F32),
            jax.ShapeDtypeStruct((bsz, N_HEADS, D_HEAD, D_HEAD), F32),
            jax.ShapeDtypeStruct((bsz, V7X_SUBLANES, D_HEAD), F32),
            jax.ShapeDtypeStruct((bsz, V7X_SUBLANES, V7X_LANES), F32),
        ],
        scratch_shapes=[
            pltpu.VMEM((N_HEADS, D_HEAD, D_HEAD), F32),
            pltpu.VMEM((V7X_SUBLANES, D_HEAD), F32),
            pltpu.VMEM((V7X_SUBLANES, V7X_LANES), F32),
            pltpu.VMEM((V7X_SUBLANES, V7X_LANES), F32),
            pltpu.VMEM((tt, D_MODEL), BF16),
            pltpu.VMEM((D_MODEL, tt), F32),
            pltpu.VMEM((tt, D_MODEL), BF16),
            pltpu.VMEM((tt, D_MODEL), F32),
            pltpu.VMEM((tt, D_MODEL), F32),
            pltpu.VMEM(head_lanes, F32),
            pltpu.VMEM(head_lanes, F32),
            pltpu.VMEM(head_lanes, F32),
            pltpu.VMEM((V7X_SUBLANES, V7X_LANES), F32),
            pltpu.VMEM((V7X_SUBLANES, tt), F32),
            pltpu.VMEM((V7X_SUBLANES, tt), F32),
        ],
        compiler_params=pltpu.CompilerParams(
            dimension_semantics=("arbitrary", "arbitrary"), vmem_limit_bytes=VMEM_LIMIT_BYTES),
        name="mlstm_prompt",
    )(x, *[lw[n] for n in names])


def _gmlp_prompt_kernel(x_ref, ma_ref, w_ref, b_ref, lng_ref, lnb_ref, ws_ref, bs_ref, wp_ref,
                        o_ref, vn_s, acc_s, *, tt):
    nch = tt // CHUNK
    xb = x_ref[...].astype(BF16)

    def col(block, g=None):
        if g is None:
            lo, n = block * D_MODEL, D_MODEL
        else:
            lo, n = block * D_MODEL + g * D_GROUP, D_GROUP
        return _dot(xb, w_ref[:, lo:lo + n]) + b_ref[:, lo:lo + n]

    vn_s[...] = _layer_norm(col(1), lng_ref[...], lnb_ref[...])
    tri = (lax.broadcasted_iota(jnp.int32, (CHUNK, CHUNK), 0)
           >= lax.broadcasted_iota(jnp.int32, (CHUNK, CHUNK), 1))
    for g in range(N_GROUPS):
        wm = jnp.where(tri, ws_ref[g], 0.0).astype(BF16)
        bias = bs_ref[:, g:g + 1]
        cols = slice(g * D_GROUP, (g + 1) * D_GROUP)
        mixed = jnp.concatenate(
            [_dot(wm, vn_s[c * CHUNK:(c + 1) * CHUNK, cols].astype(BF16)) + bias for c in range(nch)],
            axis=0)
        y = (col(0, g) * mixed * _silu(col(2, g))).astype(BF16)
        contrib = _dot(y, wp_ref[cols, :])
        if g == 0:
            acc_s[...] = contrib
        else:
            acc_s[...] += contrib
    o_ref[...] = ma_ref[...] + jax.nn.sigmoid(col(3)) * acc_s[...]


def _gmlp_prompt(x, merged, lw, layer):
    bsz, t, _ = x.shape
    tt = TT_GMLP
    kern = functools.partial(_gmlp_prompt_kernel, tt=tt)
    tile = pl.BlockSpec((None, tt, D_MODEL), lambda b, j: (b, j, 0))
    names = ["w_b", "b_b", "gmlp_ln_g", "gmlp_ln_b", "gmlp_ws", "gmlp_bs_t", "w_pb"]
    return pl.pallas_call(
        kern,
        grid=(bsz, t // tt),
        in_specs=[tile, tile] + [_at_layer(lw[n], layer) for n in names],
        out_specs=tile,
        out_shape=jax.ShapeDtypeStruct((bsz, t, D_MODEL), F32),
        scratch_shapes=[
            pltpu.VMEM((tt, D_MODEL), F32),
            pltpu.VMEM((tt, D_MODEL), F32),
        ],
        compiler_params=pltpu.CompilerParams(
            dimension_semantics=("arbitrary", "arbitrary"), vmem_limit_bytes=VMEM_LIMIT_BYTES),
        name="gmlp_prompt",
    )(x, merged, *[lw[n] for n in names])


def _lru_gates(xc, wax_ref, ba, bx, lam):
    xcb = xc.astype(BF16)
    pre = [_dot(xcb[:, n * LRU_BLK:(n + 1) * LRU_BLK], wax_ref[n]) for n in range(LRU_BLOCKS)]
    r = jax.nn.sigmoid(jnp.concatenate([p[:, :LRU_BLK] for p in pre], axis=1) + ba)
    i = jax.nn.sigmoid(jnp.concatenate([p[:, LRU_BLK:] for p in pre], axis=1) + bx)
    a = jnp.exp(LRU_C * r * jax.nn.log_sigmoid(lam))
    v = 1.0 - a * a
    mult = jnp.where(v > 0.0, v * lax.rsqrt(v), 0.0)
    return a, mult, i


def _lru_prompt_kernel(x_ref, mab_ref, w_ref, b_ref, cw_ref, cb_ref, wax_ref, ba_ref, bx_ref, lam_ref,
                       wp_ref, wo_ref, lng_ref, lnb_ref,
                       o_ref, conv_out, h_out,
                       xpad_s, a_s, b_s, hcar_s, *, tt, alpha):
    j = pl.program_id(1)
    ng = tt // V7X_SUBLANES

    @pl.when(j == 0)
    def _():
        xpad_s[0:V7X_SUBLANES, :] = jnp.zeros((V7X_SUBLANES, D_MODEL), F32)
        hcar_s[...] = jnp.zeros_like(hcar_s)

    x = x_ref[...]
    xb = x.astype(BF16)

    def col(block):
        lo = block * D_MODEL
        return _dot(xb, w_ref[:, lo:lo + D_MODEL]) + b_ref[:, lo:lo + D_MODEL]

    xpad_s[V7X_SUBLANES:, :] = col(0)
    xc = cb_ref[...] + cw_ref[0:1, :] * xpad_s[V7X_SUBLANES - 3:V7X_SUBLANES - 3 + tt, :]
    for jj in range(1, CONV_W):
        off = V7X_SUBLANES - 3 + jj
        xc = xc + cw_ref[jj:jj + 1, :] * xpad_s[off:off + tt, :]

    @pl.when(j == pl.num_programs(1) - 1)
    def _():
        conv_out[...] = xpad_s[V7X_SUBLANES + tt - (CONV_W - 1):, :]

    xpad_s[0:V7X_SUBLANES, :] = xpad_s[tt:tt + V7X_SUBLANES, :]

    a, mult, i = _lru_gates(xc, wax_ref, ba_ref[...], bx_ref[...], lam_ref[...])
    row = lax.broadcasted_iota(jnp.int32, (tt, 1), 0)
    mult = jnp.where(jnp.logical_and(row == 0, j == 0), 1.0, mult)
    bterm = mult * i * xc

    a3 = a.reshape(ng, V7X_SUBLANES, D_MODEL)
    b3 = bterm.reshape(ng, V7X_SUBLANES, D_MODEL)
    sub = lax.broadcasted_iota(jnp.int32, a3.shape, 1)
    s = 1
    while s < V7X_SUBLANES:
        keep = sub >= s
        a_sh = jnp.where(keep, pltpu.roll(a3, s, 1), 1.0)
        b_sh = jnp.where(keep, pltpu.roll(b3, s, 1), 0.0)
        b3 = a3 * b_sh + b3
        a3 = a3 * a_sh
        s *= 2
    a_s[...] = a3
    b_s[...] = b3

    def group_body(g, hprev):
        hg = a_s[g] * hprev + b_s[g]
        b_s[g] = hg
        return jnp.broadcast_to(hg[V7X_SUBLANES - 1:V7X_SUBLANES, :], hg.shape)

    hlast = lax.fori_loop(0, ng, group_body, hcar_s[...])
    hcar_s[...] = hlast
    hseq = b_s[...].reshape(tt, D_MODEL)

    @pl.when(j == pl.num_programs(1) - 1)
    def _():
        h_out[...] = hlast

    y = (hseq * _silu(col(1))).astype(BF16)
    merged = mab_ref[...] + jax.nn.sigmoid(col(2)) * _dot(y, wp_ref[...])
    out = _dot(merged.astype(BF16), wo_ref[...])
    o_ref[...] = _layer_norm(alpha * x + out, lng_ref[...], lnb_ref[...])


def _lru_prompt(x, merged, lw, layer, alpha):
    bsz, t, _ = x.shape
    tt = TT_LRU
    kern = functools.partial(_lru_prompt_kernel, tt=tt, alpha=alpha)
    tile = pl.BlockSpec((None, tt, D_MODEL), lambda b, j: (b, j, 0))
    names = ["w_c", "b_c", "lru_conv_w", "lru_conv_b", "w_ax", "lru_ba", "lru_bx", "lru_lambda",
             "w_pc", "w_out", "ln_g", "ln_b"]
    return pl.pallas_call(
        kern,
        grid=(bsz, t // tt),
        in_specs=[tile, tile] + [_at_layer(lw[n], layer) for n in names],
        out_specs=[
            tile,
            pl.BlockSpec((None, CONV_W - 1, D_MODEL), lambda b, j: (b, 0, 0)),
            pl.BlockSpec((None, V7X_SUBLANES, D_MODEL), lambda b, j: (b, 0, 0)),
        ],
        out_shape=[
            jax.ShapeDtypeStruct((bsz, t, D_MODEL), F32),
            jax.ShapeDtypeStruct((bsz, CONV_W - 1, D_MODEL), F32),
            jax.ShapeDtypeStruct((bsz, V7X_SUBLANES, D_MODEL), F32),
        ],
        scratch_shapes=[
            pltpu.VMEM((tt + V7X_SUBLANES, D_MODEL), F32),
            pltpu.VMEM((tt // V7X_SUBLANES, V7X_SUBLANES, D_MODEL), F32),
            pltpu.VMEM((tt // V7X_SUBLANES, V7X_SUBLANES, D_MODEL), F32),
            pltpu.VMEM((V7X_SUBLANES, D_MODEL), F32),
        ],
        compiler_params=pltpu.CompilerParams(
            dimension_semantics=("arbitrary", "arbitrary"), vmem_limit_bytes=VMEM_LIMIT_BYTES),
        name="lru_merge_prompt",
    )(x, merged, *[lw[n] for n in names])


def _mix_prompt_kernel(x_ref, ma_ref,
                       wb_ref, bb_ref, lng1_ref, lnb1_ref, ws_ref, bs_ref, wpb_ref,
                       wc_ref, bc_ref, cw_ref, cb_ref, wax_ref, ba_ref, bx_ref, lam_ref, wpc_ref,
                       wo_ref, lng_ref, lnb_ref,
                       o_ref, conv_out, h_out,
                       xpad_s, vn_s, a_s, b_s, hcar_s, *, tt, alpha):
    j = pl.program_id(1)
    nch = tt // CHUNK
    ng = tt // V7X_SUBLANES

    @pl.when(j == 0)
    def _():
        xpad_s[0:V7X_SUBLANES, :] = jnp.zeros((V7X_SUBLANES, D_MODEL), F32)
        hcar_s[...] = jnp.zeros_like(hcar_s)

    x = x_ref[...]
    xb = x.astype(BF16)

    def col(w_ref, b_ref, lo, n):
        return _dot(xb, w_ref[:, lo:lo + n]) + b_ref[:, lo:lo + n]

    xpad_s[V7X_SUBLANES:, :] = col(wc_ref, bc_ref, 0, D_MODEL)
    v_b = col(wb_ref, bb_ref, D_MODEL, D_MODEL)
    xc = cb_ref[...] + cw_ref[0:1, :] * xpad_s[V7X_SUBLANES - 3:V7X_SUBLANES - 3 + tt, :]
    for jj in range(1, CONV_W):
        off = V7X_SUBLANES - 3 + jj
        xc = xc + cw_ref[jj:jj + 1, :] * xpad_s[off:off + tt, :]

    @pl.when(j == pl.num_programs(1) - 1)
    def _():
        conv_out[...] = xpad_s[V7X_SUBLANES + tt - (CONV_W - 1):, :]

    xpad_s[0:V7X_SUBLANES, :] = xpad_s[tt:tt + V7X_SUBLANES, :]

    vn_s[...] = _layer_norm(v_b, lng1_ref[...], lnb1_ref[...])

    a, mult, i = _lru_gates(xc, wax_ref, ba_ref[...], bx_ref[...], lam_ref[...])
    z_c = col(wc_ref, bc_ref, D_MODEL, D_MODEL)
    row = lax.broadcasted_iota(jnp.int32, (tt, 1), 0)
    mult = jnp.where(jnp.logical_and(row == 0, j == 0), 1.0, mult)
    bterm = mult * i * xc
    a3 = a.reshape(ng, V7X_SUBLANES, D_MODEL)
    b3 = bterm.reshape(ng, V7X_SUBLANES, D_MODEL)
    sub = lax.broadcasted_iota(jnp.int32, a3.shape, 1)
    s = 1
    while s < V7X_SUBLANES:
        keep = sub >= s
        a_sh = jnp.where(keep, pltpu.roll(a3, s, 1), 1.0)
        b_sh = jnp.where(keep, pltpu.roll(b3, s, 1), 0.0)
        b3 = a3 * b_sh + b3
        a3 = a3 * a_sh
        s *= 2
    a_s[...] = a3
    b_s[...] = b3

    tri = (lax.broadcasted_iota(jnp.int32, (CHUNK, CHUNK), 0)
           >= lax.broadcasted_iota(jnp.int32, (CHUNK, CHUNK), 1))
    acc = None
    for g in range(N_GROUPS):
        wm = jnp.where(tri, ws_ref[g], 0.0).astype(BF16)
        bias = bs_ref[:, g:g + 1]
        cols = slice(g * D_GROUP, (g + 1) * D_GROUP)
        mixed = jnp.concatenate(
            [_dot(wm, vn_s[c * CHUNK:(c + 1) * CHUNK, cols].astype(BF16)) + bias for c in range(nch)],
            axis=0)
        u = col(wb_ref, bb_ref, g * D_GROUP, D_GROUP)
        z_b = col(wb_ref, bb_ref, 2 * D_MODEL + g * D_GROUP, D_GROUP)
        y = (u * mixed * _silu(z_b)).astype(BF16)
        contrib = _dot(y, wpb_ref[cols, :])
        acc = contrib if acc is None else acc + contrib
    merged = ma_ref[...] + jax.nn.sigmoid(col(wb_ref, bb_ref, 3 * D_MODEL, D_MODEL)) * acc

    hprev = hcar_s[...]
    for g in range(ng):
        hg = a_s[g] * hprev + b_s[g]
        b_s[g] = hg
        hprev = jnp.broadcast_to(hg[V7X_SUBLANES - 1:V7X_SUBLANES, :], hg.shape)
    hcar_s[...] = hprev

    @pl.when(j == pl.num_programs(1) - 1)
    def _():
        h_out[...] = hprev

    y_c = (b_s[...].reshape(tt, D_MODEL) * _silu(z_c)).astype(BF16)
    merged = merged + jax.nn.sigmoid(col(wc_ref, bc_ref, 2 * D_MODEL, D_MODEL)) * _dot(y_c, wpc_ref[...])
    out = _dot(merged.astype(BF16), wo_ref[...])
    o_ref[...] = _layer_norm(alpha * x + out, lng_ref[...], lnb_ref[...])


def _mix_prompt(x, merged, lw, layer, alpha):
    bsz, t, _ = x.shape
    tt = TT_MIX
    kern = functools.partial(_mix_prompt_kernel, tt=tt, alpha=alpha)
    tile = pl.BlockSpec((None, tt, D_MODEL), lambda b, j: (b, j, 0))
    names = ["w_b", "b_b", "gmlp_ln_g", "gmlp_ln_b", "gmlp_ws", "gmlp_bs_t", "w_pb",
             "w_c", "b_c", "lru_conv_w", "lru_conv_b", "w_ax", "lru_ba", "lru_bx", "lru_lambda", "w_pc",
             "w_out", "ln_g", "ln_b"]
    return pl.pallas_call(
        kern,
        grid=(bsz, t // tt),
        in_specs=[tile, tile] + [_at_layer(lw[n], layer) for n in names],
        out_specs=[
            tile,
            pl.BlockSpec((None, CONV_W - 1, D_MODEL), lambda b, j: (b, 0, 0)),
            pl.BlockSpec((None, V7X_SUBLANES, D_MODEL), lambda b, j: (b, 0, 0)),
        ],
        out_shape=[
            jax.ShapeDtypeStruct((bsz, t, D_MODEL), F32),
            jax.ShapeDtypeStruct((bsz, CONV_W - 1, D_MODEL), F32),
            jax.ShapeDtypeStruct((bsz, V7X_SUBLANES, D_MODEL), F32),
        ],
        scratch_shapes=[
            pltpu.VMEM((tt + V7X_SUBLANES, D_MODEL), F32),
            pltpu.VMEM((tt, D_MODEL), F32),
            pltpu.VMEM((tt // V7X_SUBLANES, V7X_SUBLANES, D_MODEL), F32),
            pltpu.VMEM((tt // V7X_SUBLANES, V7X_SUBLANES, D_MODEL), F32),
            pltpu.VMEM((V7X_SUBLANES, D_MODEL), F32),
        ],
        compiler_params=pltpu.CompilerParams(
            dimension_semantics=("arbitrary", "arbitrary"), vmem_limit_bytes=VMEM_LIMIT_BYTES),
        name="mix_prompt",
    )(x, merged, *[lw[n] for n in names])


def _proj_kernel(x_ref, w_ref, b_ref, o_ref):
    o_ref[...] = _dot(x_ref[...].astype(BF16), w_ref[...]) + b_ref[...]


def _proj(x, w, b, layer, col0, ncols, bn):
    rows = x.shape[0]
    off = col0 // bn
    return pl.pallas_call(
        _proj_kernel,
        grid=(ncols // bn,),
        in_specs=[
            pl.BlockSpec((rows, D_MODEL), lambda n: (0, 0)),
            pl.BlockSpec((None, D_MODEL, bn), lambda n: (layer, 0, off + n)),
            pl.BlockSpec((None, 1, bn), lambda n: (layer, 0, off + n)),
        ],
        out_specs=pl.BlockSpec((rows, bn), lambda n: (0, n)),
        out_shape=jax.ShapeDtypeStruct((rows, ncols), F32),
        compiler_params=pltpu.CompilerParams(
            dimension_semantics=("arbitrary",), vmem_limit_bytes=VMEM_LIMIT_BYTES),
        name="proj_sample",
    )(x, w, b)


def _mlstm_sample_kernel(p_ref, g_ref, c_ref, n_ref, m_ref, ng_ref, *rest, bb, seq):
    y_ref, c_out, n_out, m_out, qc_s = rest[-5:]
    rows = bb * seq
    t_idx = lax.broadcasted_iota(jnp.int32, (rows, 1), 0) % seq

    def down(x, d, fill=0.0):
        if d == 0:
            return x
        return jnp.where(t_idx >= d, pltpu.roll(x, d, 0), fill)

    def from_last(x):
        out = x
        for d in range(1, seq):
            out = jnp.where(t_idx == seq - 1 - d, pltpu.roll(x, rows - d, 0), out)
        return out

    it_all = g_ref[:, :V7X_LANES]
    lf_all = jax.nn.log_sigmoid(g_ref[:, V7X_LANES:])
    b_all = lf_all
    for d in range(1, seq):
        b_all = b_all + down(lf_all, d)
    g_all = it_all - b_all
    mx_all = g_all
    for d in range(1, seq):
        mx_all = jnp.maximum(mx_all, down(g_all, d, -jnp.inf))
    m0_all = m_ref[...]
    m_all = b_all + jnp.maximum(m0_all, mx_all)
    mlast_all = from_last(m_all)
    blast_all = from_last(b_all)
    inter_all = jnp.exp(b_all + m0_all - m_all)
    em_all = jnp.exp(-m_all)
    w_all = jnp.exp(blast_all + g_all - mlast_all)
    decay_all = jnp.exp(blast_all + m0_all - mlast_all)
    a_all = b_all - m_all
    m_out[...] = mlast_all

    for h in range(N_HEADS):
        hc = slice(h * D_HEAD, (h + 1) * D_HEAD)
        q = p_ref[:, h * D_HEAD:(h + 1) * D_HEAD]
        k = p_ref[:, D_MODEL + h * D_HEAD:D_MODEL + (h + 1) * D_HEAD] * K_SCALE
        v = p_ref[:, 2 * D_MODEL + h * D_HEAD:2 * D_MODEL + (h + 1) * D_HEAD]
        lane = slice(h, h + 1)
        a_col, g_col = a_all[:, lane], g_all[:, lane]
        inter, em = inter_all[:, lane], em_all[:, lane]
        wcol, decay = w_all[:, lane], decay_all[:, lane]

        kw = k * wcol
        per_slab = V7X_SUBLANES // seq
        grp = lax.broadcasted_iota(jnp.int32, (V7X_SUBLANES, 1), 0) // seq
        for slab in range(rows // V7X_SUBLANES):
            s0 = slab * V7X_SUBLANES
            q8 = q[s0:s0 + V7X_SUBLANES, :].astype(BF16)
            kw8 = kw[s0:s0 + V7X_SUBLANES, :]
            v8 = v[s0:s0 + V7X_SUBLANES, :].astype(BF16)
            qc8 = jnp.zeros((V7X_SUBLANES, D_HEAD), F32)
            for e in range(per_slab):
                b = slab * per_slab + e
                c0 = c_ref[b, h]
                qc8 = jnp.where(grp == e, _dot(q8, c0.astype(BF16)), qc8)
                kw_e = jnp.where(grp == e, kw8, 0.0).astype(BF16)
                c_out[b, h] = decay[b * seq:b * seq + 1, :] * c0 + _dot_tn(kw_e, v8)
            qc_s[s0:s0 + V7X_SUBLANES, :] = qc8

        n0 = n_ref[:, hc]
        num = inter * qc_s[...]
        den = inter * jnp.sum(q * n0, axis=1, keepdims=True)
        for d in range(seq):
            s_d = jnp.sum(q * down(k, d), axis=1, keepdims=True) * jnp.exp(a_col + down(g_col, d))
            s_d = jnp.where(t_idx >= d, s_d, 0.0)
            num = num + s_d * down(v, d)
            den = den + s_d
        hh = num / jnp.maximum(jnp.abs(den), em)

        ksum = kw
        for d in range(1, seq):
            ksum = ksum + down(kw, d)
        n_out[:, hc] = decay * n0 + ksum

        hh = jax.nn.sigmoid(p_ref[:, 3 * D_MODEL + h * D_HEAD:3 * D_MODEL + (h + 1) * D_HEAD]) * hh
        mu = jnp.mean(hh, axis=-1, keepdims=True)
        hcn = hh - mu
        var = jnp.mean(hcn * hcn, axis=-1, keepdims=True)
        hn = hcn * lax.rsqrt(var + LN_EPS) * ng_ref[:, hc]
        y_ref[:, hc] = hn * _silu(p_ref[:, 4 * D_MODEL + h * D_HEAD:4 * D_MODEL + (h + 1) * D_HEAD])


def _mlstm_sample(proj_a, gates, c_all, c_new_all, n0_rows, m0_rows, lw, layer, seq):
    nb = c_all.shape[1]
    bb = BB_MLSTM_SAMPLE
    rows = bb * seq
    kern = functools.partial(_mlstm_sample_kernel, bb=bb, seq=seq)
    c_spec = pl.BlockSpec((None, bb, N_HEADS, D_HEAD, D_HEAD), lambda i: (layer, i, 0, 0, 0))
    in_specs = [
        pl.BlockSpec((rows, 5 * D_MODEL), lambda i: (i, 0)),
        pl.BlockSpec((rows, 2 * V7X_LANES), lambda i: (i, 0)),
        c_spec,
        pl.BlockSpec((rows, D_MODEL), lambda i: (i, 0)),
        pl.BlockSpec((rows, V7X_LANES), lambda i: (i, 0)),
        _at_layer(lw["norm_g"], layer),
    ]
    args = [proj_a, gates, c_all, n0_rows, m0_rows, lw["norm_g"]]
    aliases = {}
    if c_new_all is not None:
        in_specs.append(pl.BlockSpec(memory_space=pl.ANY))
        args.append(c_new_all)
        aliases = {len(args) - 1: 1}
    return pl.pallas_call(
        kern,
        grid=(nb // bb,),
        in_specs=in_specs,
        out_specs=[
            pl.BlockSpec((rows, D_MODEL), lambda i: (i, 0)),
            c_spec,
            pl.BlockSpec((rows, D_MODEL), lambda i: (i, 0)),
            pl.BlockSpec((rows, V7X_LANES), lambda i: (i, 0)),
        ],
        out_shape=[
            jax.ShapeDtypeStruct((nb * seq, D_MODEL), F32),
            jax.ShapeDtypeStruct(c_all.shape, F32),
            jax.ShapeDtypeStruct((nb * seq, D_MODEL), F32),
            jax.ShapeDtypeStruct((nb * seq, V7X_LANES), F32),
        ],
        scratch_shapes=[pltpu.VMEM((rows, D_HEAD), F32)],
        input_output_aliases=aliases,
        compiler_params=pltpu.CompilerParams(
            dimension_semantics=("arbitrary",), vmem_limit_bytes=VMEM_LIMIT_BYTES),
        name="mlstm_sample",
    )(*args)


def _rest_sample_kernel(x_ref, ya_ref, ga_ref, pb_ref, pc_ref, buf_ref, h0_ref,
                        wsm_ref, bsm_ref,
                        lng1_ref, lnb1_ref, cw_ref, cb_ref, wax_ref, ba_ref, bx_ref, lam_ref,
                        wpa_ref, wpb_ref, wpc_ref, wo_ref, lng_ref, lnb_ref,
                        o_ref, vn_out, conv_out, h_out, *, seq, bbs, alpha):
    rows = seq * bbs

    def flat(ref, lo, n):
        return ref[:, :, lo:lo + n].reshape(rows, n)

    vn = _layer_norm(flat(pb_ref, D_MODEL, D_MODEL), lng1_ref[...], lnb1_ref[...])
    vn_out[...] = vn.reshape(seq, bbs, D_MODEL)
    mixed_t = []
    for t in range(seq):
        parts = []
        for g in range(N_GROUPS):
            cols = slice(g * D_GROUP, (g + 1) * D_GROUP)
            acc = jnp.full((bbs, D_GROUP), bsm_ref[g * seq + t], F32)
            for s in range(t + 1):
                acc = acc + wsm_ref[(g * seq + t) * seq + s] * vn[s * bbs:(s + 1) * bbs, cols]
            parts.append(acc)
        mixed_t.append(jnp.concatenate(parts, axis=1))
    mixed = jnp.concatenate(mixed_t, axis=0)
    yb = (flat(pb_ref, 0, D_MODEL) * mixed * _silu(flat(pb_ref, 2 * D_MODEL, D_MODEL))).astype(BF16)

    xc_raw = flat(pc_ref, 0, D_MODEL)
    xp = [buf_ref[t] for t in range(CONV_W - 1)] + [xc_raw[t * bbs:(t + 1) * bbs] for t in range(seq)]
    for t in range(CONV_W - 1):
        conv_out[t] = xp[seq + t]
    xc_t = []
    for t in range(seq):
        acc = cb_ref[...] + cw_ref[0:1, :] * xp[t]
        for jj in range(1, CONV_W):
            acc = acc + cw_ref[jj:jj + 1, :] * xp[t + jj]
        xc_t.append(acc)
    xc = jnp.concatenate(xc_t, axis=0)
    a, mult, i = _lru_gates(xc, wax_ref, ba_ref[...], bx_ref[...], lam_ref[...])
    bterm = mult * i * xc
    h = h0_ref[...]
    hs = []
    for t in range(seq):
        h = a[t * bbs:(t + 1) * bbs] * h + bterm[t * bbs:(t + 1) * bbs]
        hs.append(h)
    h_out[...] = h
    yc = (jnp.concatenate(hs, axis=0) * _silu(flat(pc_ref, D_MODEL, D_MODEL))).astype(BF16)

    ya = ya_ref[...].reshape(rows, D_MODEL).astype(BF16)
    merged = (jax.nn.sigmoid(ga_ref[...].reshape(rows, D_MODEL)) * _dot(ya, wpa_ref[...])
              + jax.nn.sigmoid(flat(pb_ref, 3 * D_MODEL, D_MODEL)) * _dot(yb, wpb_ref[...])
              + jax.nn.sigmoid(flat(pc_ref, 2 * D_MODEL, D_MODEL)) * _dot(yc, wpc_ref[...]))
    out = _dot(merged.astype(BF16), wo_ref[...])
    x = x_ref[...].reshape(rows, D_MODEL)
    o_ref[...] = _layer_norm(alpha * x + out, lng_ref[...], lnb_ref[...]).reshape(seq, bbs, D_MODEL)


def _rest_sample(x_tm, ya_tm, ga_tm, pb_tm, pc_tm, buf_tm, h0, lw, layer, alpha):
    seq, nb, _ = x_tm.shape
    bbs = min(BB_REST_SAMPLE, nb)
    kern = functools.partial(_rest_sample_kernel, seq=seq, bbs=bbs, alpha=alpha)

    def tm(n, lead=seq):
        return pl.BlockSpec((lead, bbs, n), lambda i: (0, i, 0))

    smem = pl.BlockSpec(memory_space=pltpu.SMEM)
    names = ["gmlp_ln_g", "gmlp_ln_b", "lru_conv_w", "lru_conv_b", "w_ax", "lru_ba", "lru_bx", "lru_lambda",
             "w_pa", "w_pb", "w_pc", "w_out", "ln_g", "ln_b"]
    return pl.pallas_call(
        kern,
        grid=(nb // bbs,),
        in_specs=[tm(D_MODEL), tm(D_MODEL), tm(D_MODEL), tm(4 * D_MODEL), tm(3 * D_MODEL),
                  tm(D_MODEL, CONV_W - 1), pl.BlockSpec((bbs, D_MODEL), lambda i: (i, 0)),
                  smem, smem] + [_at_layer(lw[n], layer) for n in names],
        out_specs=[tm(D_MODEL), tm(D_MODEL), tm(D_MODEL, CONV_W - 1),
                   pl.BlockSpec((bbs, D_MODEL), lambda i: (i, 0))],
        out_shape=[
            jax.ShapeDtypeStruct((seq, nb, D_MODEL), F32),
            jax.ShapeDtypeStruct((seq, nb, D_MODEL), F32),
            jax.ShapeDtypeStruct((CONV_W - 1, nb, D_MODEL), F32),
            jax.ShapeDtypeStruct((nb, D_MODEL), F32),
        ],
        compiler_params=pltpu.CompilerParams(
            dimension_semantics=("arbitrary",), vmem_limit_bytes=VMEM_LIMIT_BYTES),
        name="rest_sample",
    )(x_tm, ya_tm, ga_tm, pb_tm, pc_tm, buf_tm, h0, lw["ws_small"][layer], lw["bs_small"][layer],
      *[lw[n] for n in names])


def _pack_params(w_in, b_in, mlstm_norm_g, gmlp_ln_g, gmlp_ln_b, gmlp_ws, gmlp_bs, lru_conv_w, lru_conv_b,
                 lru_wa, lru_ba, lru_wx, lru_bx, lru_lambda, w_proj_a, w_proj_b, w_proj_c, w_out, ln_g, ln_b,
                 seq_s):
    depth = w_in.shape[0]
    d = D_MODEL
    o_i = 5 * d
    o_f = o_i + N_HEADS
    o_b = o_f + N_HEADS
    o_c = o_b + 3 * d
    o_g = o_c + 2 * d

    def cols(a, *ranges):
        return jnp.concatenate([a[..., lo:hi] for lo, hi in ranges], axis=-1)

    def row(a):
        return a[:, None, :]

    def pad_lanes(a, n):
        return jnp.pad(a, [(0, 0)] * (a.ndim - 1) + [(0, n - a.shape[-1])])

    wi, wf = w_in[..., o_i:o_f], w_in[..., o_f:o_b]
    bi, bf = b_in[..., o_i:o_f], b_in[..., o_f:o_b]
    pad_rows = [(0, 0), (0, V7X_SUBLANES - N_HEADS), (0, 0)]
    lw = {
        "w_a": cols(w_in, (0, 5 * d), (o_g, o_g + d)).astype(BF16),
        "b_a": row(cols(b_in, (0, 5 * d), (o_g, o_g + d))),
        "w_kt": jnp.swapaxes(w_in[..., d:2 * d], 1, 2).astype(BF16),
        "b_kt": b_in[:, d:2 * d, None],
        "w_b": cols(w_in, (o_b, o_b + 3 * d), (o_g + d, o_g + 2 * d)).astype(BF16),
        "b_b": row(cols(b_in, (o_b, o_b + 3 * d), (o_g + d, o_g + 2 * d))),
        "w_c": cols(w_in, (o_c, o_c + 2 * d), (o_g + 2 * d, o_g + 3 * d)).astype(BF16),
        "b_c": row(cols(b_in, (o_c, o_c + 2 * d), (o_g + 2 * d, o_g + 3 * d))),
        "w_gc": jnp.concatenate([pad_lanes(wi, V7X_LANES), pad_lanes(wf, V7X_LANES)], axis=-1).astype(BF16),
        "b_gc": row(jnp.concatenate([pad_lanes(bi, V7X_LANES), pad_lanes(bf, V7X_LANES)], axis=-1)),
        "w_gr": jnp.concatenate([jnp.pad(jnp.swapaxes(wi, 1, 2), pad_rows),
                                 jnp.pad(jnp.swapaxes(wf, 1, 2), pad_rows)], axis=1).astype(BF16),
        "b_gr": jnp.concatenate([jnp.pad(bi[:, :, None], pad_rows),
                                 jnp.pad(bf[:, :, None], pad_rows)], axis=1),
        "norm_g": row(mlstm_norm_g),
        "gmlp_ln_g": row(gmlp_ln_g), "gmlp_ln_b": row(gmlp_ln_b),
        "gmlp_ws": gmlp_ws,
        "gmlp_bs_t": jnp.swapaxes(gmlp_bs, 1, 2),
        "ws_small": gmlp_ws[:, :, :seq_s, :seq_s].reshape(depth, -1),
        "bs_small": gmlp_bs[:, :, :seq_s].reshape(depth, -1),
        "lru_conv_w": lru_conv_w, "lru_conv_b": row(lru_conv_b),
        "w_ax": jnp.concatenate([lru_wa, lru_wx], axis=-1).astype(BF16),
        "lru_ba": row(lru_ba), "lru_bx": row(lru_bx), "lru_lambda": row(lru_lambda),
        "w_pa": w_proj_a.astype(BF16), "w_pb": w_proj_b.astype(BF16), "w_pc": w_proj_c.astype(BF16),
        "w_out": w_out.astype(BF16),
        "ln_g": row(ln_g), "ln_b": row(ln_b),
    }
    return lw


def _prompt_layer(x, lw, layer, alpha):
    merged, c, n, m = _mlstm_prompt(x, lw, layer)
    x_new, conv, h = _mix_prompt(x, merged, lw, layer, alpha)
    return x_new, c, n[:, :N_HEADS], m[:, 0, :N_HEADS], conv, h[:, 0]


def _sample_layer(x_bm, c_all, c_new_all, n0, m0, conv_buf, h0, lw, layer, alpha):
    nb, seq, d = x_bm.shape
    x_rows = x_bm.reshape(nb * seq, d)
    x_tm = jnp.swapaxes(x_bm, 0, 1)
    x_tm_rows = x_tm.reshape(seq * nb, d)
    proj_a = _proj(x_rows, lw["w_a"], lw["b_a"], layer, 0, 5 * d, d)
    gates = _proj(x_rows, lw["w_gc"], lw["b_gc"], layer, 0, 2 * V7X_LANES, 2 * V7X_LANES)
    ga_tm = _proj(x_tm_rows, lw["w_a"], lw["b_a"], layer, 5 * d, d, d).reshape(seq, nb, d)
    pb_tm = _proj(x_tm_rows, lw["w_b"], lw["b_b"], layer, 0, 4 * d, d).reshape(seq, nb, 4 * d)
    pc_tm = _proj(x_tm_rows, lw["w_c"], lw["b_c"], layer, 0, 3 * d, d).reshape(seq, nb, 3 * d)

    n0_rows = jnp.repeat(n0.reshape(nb, d), seq, axis=0)
    m0_rows = jnp.repeat(jnp.pad(m0, ((0, 0), (0, V7X_LANES - N_HEADS))), seq, axis=0)
    ya, c_new_all, n_rows, m_rows = _mlstm_sample(
        proj_a, gates, c_all, c_new_all, n0_rows, m0_rows, lw, layer, seq)
    n_new = n_rows.reshape(nb, seq, N_HEADS, D_HEAD)[:, seq - 1]
    m_new = m_rows.reshape(nb, seq, V7X_LANES)[:, seq - 1, :N_HEADS]

    ya_tm = jnp.swapaxes(ya.reshape(nb, seq, d), 0, 1)
    buf_tm = jnp.swapaxes(conv_buf, 0, 1)
    x_new_tm, vn_tm, conv_tm, h_new = _rest_sample(
        x_tm, ya_tm, ga_tm, pb_tm, pc_tm, buf_tm, h0, lw, layer, alpha)
    return (jnp.swapaxes(x_new_tm, 0, 1), c_new_all, n_new, m_new, jnp.swapaxes(conv_tm, 0, 1), h_new,
            jnp.swapaxes(vn_tm, 0, 1))


def kernel(x_prompt, x_sample, state_mlstm_c, state_mlstm_n, state_mlstm_m, state_lru_conv, state_lru_h,
           w_in, b_in, mlstm_norm_g, gmlp_ln_g, gmlp_ln_b, gmlp_ws, gmlp_bs, lru_conv_w, lru_conv_b,
           lru_wa, lru_ba, lru_wx, lru_bx, lru_lambda, w_proj_a, w_proj_b, w_proj_c, w_out, ln_g, ln_b):
    depth = w_in.shape[0]
    alpha = float((2 * depth) ** 0.25)
    seq_s = x_sample.shape[1]
    assert V7X_SUBLANES % seq_s == 0 and x_prompt.shape[1] % TT_MLSTM == 0
    lw = _pack_params(w_in, b_in, mlstm_norm_g, gmlp_ln_g, gmlp_ln_b, gmlp_ws, gmlp_bs, lru_conv_w,
                      lru_conv_b, lru_wa, lru_ba, lru_wx, lru_bx, lru_lambda, w_proj_a, w_proj_b, w_proj_c,
                      w_out, ln_g, ln_b, seq_s)
    xp, xs = x_prompt, x_sample
    outs_p = [[] for _ in range(5)]
    outs_s = [[] for _ in range(5)]
    c_new_all = None
    for layer in range(depth):
        xp, *st = _prompt_layer(xp, lw, layer, alpha)
        for acc, val in zip(outs_p, st):
            acc.append(val)
        xs, c_new_all, *st = _sample_layer(
            xs, state_mlstm_c, c_new_all, state_mlstm_n[layer], state_mlstm_m[layer],
            state_lru_conv[layer], state_lru_h[layer], lw, layer, alpha)
        for acc, val in zip(outs_s, st):
            acc.append(val)
    stacked_p = [jnp.stack(v) for v in outs_p]
    stacked_s = [jnp.stack(v) for v in outs_s]
    return (xp, xs, *stacked_p, c_new_all, *stacked_s)
```

```python
import functools

import jax
import jax.numpy as jnp
from jax import lax
from jax.experimental import pallas as pl
from jax.experimental.pallas import tpu as pltpu

D_MODEL = 1024
N_HEADS = 4
D_HEAD = D_MODEL // N_HEADS
CHUNK = 128
N_GROUPS = 4
D_GROUP = D_MODEL // N_GROUPS
LRU_BLOCKS = 8
LRU_BLK = D_MODEL // LRU_BLOCKS
CONV_W = 4
LRU_C = 8.0
LN_EPS = 1e-5
K_SCALE = D_HEAD ** -0.5
LOG2_E = 1.4426950408889634
V7X_LANES = 128
V7X_SUBLANES = 8
VMEM_LIMIT_BYTES = 56 * 1024 * 1024

TT_MLSTM = 512
TT_MIX = 512
SB_MIX = 128
BB_MLSTM_SAMPLE = 8
BB_REST_SAMPLE = 64

BLK_Q, BLK_K, BLK_V, BLK_O, BLK_ZA = 0, 1, 2, 3, 4
BLK_U, BLK_VB, BLK_ZB = 5, 6, 7
BLK_XC, BLK_ZC = 8, 9
BLK_GA, BLK_GB, BLK_GC = 10, 11, 12
N_BLOCKS = 13

BF16 = jnp.bfloat16
F32 = jnp.float32

_NT = (((1,), (1,)), ((), ()))
_TN = (((0,), (0,)), ((), ()))


def _dot(a, b):
    return jnp.dot(a, b, preferred_element_type=F32)


def _dot_nt(a, b):
    return lax.dot_general(a, b, _NT, preferred_element_type=F32)


def _dot_tn(a, b):
    return lax.dot_general(a, b, _TN, preferred_element_type=F32)


def _layer_norm(x, g, b):
    mu = jnp.mean(x, axis=-1, keepdims=True)
    xc = x - mu
    var = jnp.mean(xc * xc, axis=-1, keepdims=True)
    return xc * lax.rsqrt(var + LN_EPS) * g + b


def _silu(x):
    return x * jax.nn.sigmoid(x)


def _scan_axis(x, axis, op, fill):
    n = x.shape[axis]
    idx = lax.broadcasted_iota(jnp.int32, x.shape, axis)
    s = 1
    while s < n:
        x = op(x, jnp.where(idx >= s, pltpu.roll(x, s, axis), fill))
        s *= 2
    return x


def _at_layer(arr, layer):
    shape = arr.shape[1:]
    nd = len(shape)
    return pl.BlockSpec((None,) + shape, lambda *_: (layer,) + (0,) * nd, pipeline_mode=pl.Buffered(1))


def _w_block(layer, blk):
    return pl.BlockSpec((None, D_MODEL, D_MODEL), lambda *_: (layer, 0, blk), pipeline_mode=pl.Buffered(1))


def _b_block(layer, blk):
    return pl.BlockSpec((None, 1, D_MODEL), lambda *_: (layer, 0, blk), pipeline_mode=pl.Buffered(1))


def _mlstm_prompt_kernel(x_ref, wq_ref, wv_ref, wo_ref, wz_ref, wg_ref, bq_ref, bv_ref, bo_ref, bz_ref, bg_ref,
                         wkt_ref, bkt_ref, wgc_ref, bgc_ref, wgr_ref, bgr_ref, ng_ref, wp_ref,
                         o_ref, c_out, n_out, m_out,
                         c_s, n_s, mc_s, mr_s, q_s, kt_s, v_s, o_s, h_s,
                         acol_s, inter_s, em_s, decay_s, grow_s, wrow_s, *, tt):
    j = pl.program_id(1)
    nch = tt // CHUNK

    @pl.when(j == 0)
    def _():
        c_s[...] = jnp.zeros_like(c_s)
        n_s[...] = jnp.zeros_like(n_s)
        mc_s[...] = jnp.zeros_like(mc_s)
        mr_s[...] = jnp.zeros_like(mr_s)

    xb = x_ref[...].astype(BF16)

    def col(w_ref, b_ref):
        return _dot(xb, w_ref[...]) + b_ref[...]

    def project_q():
        q_s[...] = col(wq_ref, bq_ref).astype(BF16)

    def project_k():
        kt_s[...] = (_dot_nt(wkt_ref[...], xb) + bkt_ref[...]) * K_SCALE

    def project_v():
        v_s[...] = col(wv_ref, bv_ref).astype(BF16)

    def project_o():
        o_s[...] = jax.nn.sigmoid(col(wo_ref, bo_ref))

    projections = [project_q, project_k, project_v, project_o]

    gcol = _dot(xb, wgc_ref[...]) + bgc_ref[...]
    it_c = gcol[:, :V7X_LANES]
    lf_c = jax.nn.log_sigmoid(gcol[:, V7X_LANES:])
    grow = _dot_nt(wgr_ref[...], xb) + bgr_ref[...]
    it_r = grow[:V7X_SUBLANES]
    lf_r = jax.nn.log_sigmoid(grow[V7X_SUBLANES:])

    m0c = mc_s[0:1, :]
    m0r = mr_s[...]
    for c in range(nch):
        sl = slice(c * CHUNK, (c + 1) * CHUNK)
        bcol = _scan_axis(lf_c[sl], 0, jnp.add, 0.0)
        gcl = it_c[sl] - bcol
        mcol = bcol + jnp.maximum(m0c, _scan_axis(gcl, 0, jnp.maximum, -jnp.inf))
        blast = bcol[CHUNK - 1:CHUNK]
        mlast = mcol[CHUNK - 1:CHUNK]
        acol = bcol - mcol
        inter = jnp.exp(bcol + m0c - mcol)
        em = jnp.exp(-mcol)
        decay_s[c:c + 1, :] = jnp.exp(blast + m0c - mlast)
        for h in range(N_HEADS):
            acol_s[h, sl, :] = jnp.broadcast_to(acol[:, h:h + 1], (CHUNK, V7X_LANES))
            inter_s[h, sl, :] = jnp.broadcast_to(inter[:, h:h + 1], (CHUNK, V7X_LANES))
            em_s[h, sl, :] = jnp.broadcast_to(em[:, h:h + 1], (CHUNK, V7X_LANES))
        brow = _scan_axis(lf_r[:, sl], 1, jnp.add, 0.0)
        grw = it_r[:, sl] - brow
        mrow = brow + jnp.maximum(m0r, _scan_axis(grw, 1, jnp.maximum, -jnp.inf))
        mlast_r = mrow[:, CHUNK - 1:CHUNK]
        grow_s[:, sl] = grw
        wrow_s[:, sl] = jnp.exp(brow[:, CHUNK - 1:CHUNK] + grw - mlast_r)
        m0c = mlast
        m0r = jnp.broadcast_to(mlast_r, m0r.shape)
        if c < len(projections):
            projections[c]()
    for project in projections[nch:]:
        project()
    mc_s[...] = jnp.broadcast_to(m0c, mc_s.shape)
    mr_s[...] = m0r

    tri = (lax.broadcasted_iota(jnp.int32, (CHUNK, CHUNK), 0)
           >= lax.broadcasted_iota(jnp.int32, (CHUNK, CHUNK), 1))
    ones_rows = jnp.ones((V7X_SUBLANES, CHUNK), BF16)

    for c in range(nch):
        rows = slice(c * CHUNK, (c + 1) * CHUNK)
        decay_c = decay_s[c:c + 1, :]
        for h in range(N_HEADS):
            hc = slice(h * D_HEAD, (h + 1) * D_HEAD)
            qc = q_s[rows, hc]
            kt = kt_s[hc, rows]
            vc = v_s[rows, hc]
            inter = inter_s[h, rows, :]
            decay = decay_c[:, h:h + 1]

            dmat = jnp.exp(jnp.where(tri, acol_s[h, rows, :] + grow_s[h:h + 1, rows], -jnp.inf))
            s = _dot(qc, kt.astype(BF16)) * dmat
            c0 = c_s[h]
            n0 = n_s[h:h + 1, :]
            num = (jnp.concatenate([inter, inter], axis=1) * _dot(qc, c0.astype(BF16))
                   + _dot(s.astype(BF16), vc))
            qn = qc.astype(F32) * n0
            den = jnp.sum(inter * (qn[:, :V7X_LANES] + qn[:, V7X_LANES:]) + s, axis=1, keepdims=True)
            rden = 1.0 / jnp.maximum(jnp.abs(den), em_s[h, rows, :])
            h_s[rows, hc] = num * jnp.concatenate([rden, rden], axis=1)

            kwt = (kt * wrow_s[h:h + 1, rows]).astype(BF16)
            c_s[h] = decay * c0 + _dot(kwt, vc)
            n_s[h:h + 1, :] = decay * n0 + _dot_nt(ones_rows, kwt)[0:1, :]

    hh = o_s[...] * h_s[...]
    parts = []
    for h in range(N_HEADS):
        hc = slice(h * D_HEAD, (h + 1) * D_HEAD)
        hd = hh[:, hc]
        mu = jnp.mean(hd, axis=-1, keepdims=True)
        hd = hd - mu
        var = jnp.mean(hd * hd, axis=-1, keepdims=True)
        parts.append(hd * lax.rsqrt(var + LN_EPS) * ng_ref[:, hc])
    y = (jnp.concatenate(parts, axis=1) * _silu(col(wz_ref, bz_ref))).astype(BF16)
    o_ref[...] = jax.nn.sigmoid(col(wg_ref, bg_ref)) * _dot(y, wp_ref[...])

    @pl.when(j == pl.num_programs(1) - 1)
    def _():
        c_out[...] = c_s[...]
        n_out[...] = n_s[...]
        m_out[...] = mc_s[...]


def _mlstm_prompt(x, lw, layer):
    bsz, t, _ = x.shape
    tt = TT_MLSTM
    kern = functools.partial(_mlstm_prompt_kernel, tt=tt)
    tile = pl.BlockSpec((None, tt, D_MODEL), lambda b, j: (b, j, 0))
    blocks = [BLK_Q, BLK_V, BLK_O, BLK_ZA, BLK_GA]
    names = ["w_kt", "b_kt", "w_gc", "b_gc", "w_gr", "b_gr", "norm_g", "w_pa"]
    head_lanes = (N_HEADS, tt, V7X_LANES)
    return pl.pallas_call(
        kern,
        grid=(bsz, t // tt),
        in_specs=([tile] + [_w_block(layer, blk) for blk in blocks] + [_b_block(layer, blk) for blk in blocks]
                  + [_at_layer(lw[n], layer) for n in names]),
        out_specs=[
            tile,
            pl.BlockSpec((None, N_HEADS, D_HEAD, D_HEAD), lambda b, j: (b, 0, 0, 0)),
            pl.BlockSpec((None, V7X_SUBLANES, D_HEAD), lambda b, j: (b, 0, 0)),
            pl.BlockSpec((None, V7X_SUBLANES, V7X_LANES), lambda b, j: (b, 0, 0)),
        ],
        out_shape=[
            jax.ShapeDtypeStruct((bsz, t, D_MODEL), F32),
            jax.ShapeDtypeStruct((bsz, N_HEADS, D_HEAD, D_HEAD), F32),
            jax.ShapeDtypeStruct((bsz, V7X_SUBLANES, D_HEAD), F32),
            jax.ShapeDtypeStruct((bsz, V7X_SUBLANES, V7X_LANES), F32),
        ],
        scratch_shapes=[
            pltpu.VMEM((N_HEADS, D_HEAD, D_HEAD), F32),
            pltpu.VMEM((V7X_SUBLANES, D_HEAD), F32),
            pltpu.VMEM((V7X_SUBLANES, V7X_LANES), F32),
            pltpu.VMEM((V7X_SUBLANES, V7X_LANES), F32),
            pltpu.VMEM((tt, D_MODEL), BF16),
            pltpu.VMEM((D_MODEL, tt), F32),
            pltpu.VMEM((tt, D_MODEL), BF16),
            pltpu.VMEM((tt, D_MODEL), F32),
            pltpu.VMEM((tt, D_MODEL), F32),
            pltpu.VMEM(head_lanes, F32),
            pltpu.VMEM(head_lanes, F32),
            pltpu.VMEM(head_lanes, F32),
            pltpu.VMEM((V7X_SUBLANES, V7X_LANES), F32),
            pltpu.VMEM((V7X_SUBLANES, tt), F32),
            pltpu.VMEM((V7X_SUBLANES, tt), F32),
        ],
        compiler_params=pltpu.CompilerParams(
            dimension_semantics=("arbitrary", "arbitrary"), vmem_limit_bytes=VMEM_LIMIT_BYTES),
        name="mlstm_prompt",
    )(x, *([lw["w_all"]] * len(blocks)), *([lw["b_all"]] * len(blocks)), *[lw[n] for n in names])


def _lru_gates(xc, wax_ref, ba, bx, lam):
    xcb = xc.astype(BF16)
    pre = [_dot(xcb[:, n * LRU_BLK:(n + 1) * LRU_BLK], wax_ref[n]) for n in range(LRU_BLOCKS)]
    r = jax.nn.sigmoid(jnp.concatenate([p[:, :LRU_BLK] for p in pre], axis=1) + ba)
    i = jax.nn.sigmoid(jnp.concatenate([p[:, LRU_BLK:] for p in pre], axis=1) + bx)
    a = jnp.exp2(r * (LRU_C * LOG2_E * jax.nn.log_sigmoid(lam)))
    v = 1.0 - a * a
    mult = jnp.where(v > 0.0, v * lax.rsqrt(v), 0.0)
    return a, mult, i


def _mix_kernel(x_ref, ma_ref,
                wu_ref, wvb_ref, wzb_ref, wgb_ref, wxc_ref, wzc_ref, wgc_ref,
                bu_ref, bvb_ref, bzb_ref, bgb_ref, bxc_ref, bzc_ref, bgc_ref,
                lng1_ref, lnb1_ref, ws_ref, bs_ref, wpb_ref,
                cw_ref, cb_ref, wax_ref, ba_ref, bx_ref, lam_ref, wpc_ref,
                wo_ref, lng_ref, lnb_ref,
                o_ref, conv_out, h_out,
                ucar_s, vn_s, h_s, hcar_s, *, tt, sb, alpha):
    j = pl.program_id(1)
    nch = tt // CHUNK
    nsb = tt // sb

    @pl.when(j == 0)
    def _():
        ucar_s[...] = jnp.broadcast_to(cb_ref[...], ucar_s.shape)
        hcar_s[...] = jnp.zeros_like(hcar_s)

    x = x_ref[...]
    xb = x.astype(BF16)
    state = {}

    def col(w_ref, b_ref, cols=slice(None), rows=slice(None)):
        return _dot(xb[rows], w_ref[:, cols]) + b_ref[:, cols]

    ngs = sb // V7X_SUBLANES
    sub = lax.broadcasted_iota(jnp.int32, (ngs, V7X_SUBLANES, D_MODEL), 1)
    conv_tail = [ucar_s[jj] for jj in range(CONV_W - 1)]

    def lru_input(k):
        state["x_c", k] = col(wxc_ref, bxc_ref, rows=slice(k * sb, (k + 1) * sb))

    def lru_block(k, hprev):
        r0 = k * sb
        x3 = state["x_c", k].reshape(ngs, V7X_SUBLANES, D_MODEL)
        u = cb_ref[...] + cw_ref[0:1, :] * x3
        for jj in range(1, CONV_W):
            rot = pltpu.roll(u, 1, 1)
            prev = jnp.concatenate([conv_tail[jj - 1][None], rot[:-1]], axis=0)
            conv_tail[jj - 1] = rot[ngs - 1]
            u = jnp.where(sub == 0, prev, rot) + cw_ref[jj:jj + 1, :] * x3
        xc = u.reshape(sb, D_MODEL)
        a, mult, i = _lru_gates(xc, wax_ref, ba_ref[...], bx_ref[...], lam_ref[...])
        if k == 0:
            row = lax.broadcasted_iota(jnp.int32, (sb, 1), 0)
            mult = jnp.where(jnp.logical_and(row == 0, j == 0), 1.0, mult)
        bterm = mult * i * xc
        a3 = a.reshape(ngs, V7X_SUBLANES, D_MODEL)
        b3 = bterm.reshape(ngs, V7X_SUBLANES, D_MODEL)
        s = 1
        while s < V7X_SUBLANES:
            keep = sub >= s
            a_sh = jnp.where(keep, pltpu.roll(a3, s, 1), 1.0)
            b_sh = jnp.where(keep, pltpu.roll(b3, s, 1), 0.0)
            b3 = a3 * b_sh + b3
            a3 = a3 * a_sh
            s *= 2
        for g in range(ngs):
            hg = a3[g] * hprev + b3[g]
            h_s[r0 + g * V7X_SUBLANES:r0 + (g + 1) * V7X_SUBLANES, :] = hg
            hprev = jnp.broadcast_to(hg[V7X_SUBLANES - 1:V7X_SUBLANES, :], hg.shape)
        return hprev

    tri = (lax.broadcasted_iota(jnp.int32, (CHUNK, CHUNK), 0)
           >= lax.broadcasted_iota(jnp.int32, (CHUNK, CHUNK), 1))

    def gmlp_norm():
        vn_s[...] = _layer_norm(col(wvb_ref, bvb_ref), lng1_ref[...], lnb1_ref[...])

    def gmlp_group(g):
        wm = jnp.where(tri, ws_ref[g], 0.0).astype(BF16)
        bias = bs_ref[:, g:g + 1]
        cols = slice(g * D_GROUP, (g + 1) * D_GROUP)
        mixed = jnp.concatenate(
            [_dot(wm, vn_s[c * CHUNK:(c + 1) * CHUNK, cols].astype(BF16)) + bias for c in range(nch)],
            axis=0)
        y = (col(wu_ref, bu_ref, cols) * mixed * _silu(col(wzb_ref, bzb_ref, cols))).astype(BF16)
        contrib = _dot(y, wpb_ref[cols, :])
        state["acc"] = contrib if g == 0 else state["acc"] + contrib

    def lru_silu_z():
        state["sz_c"] = _silu(col(wzc_ref, bzc_ref))

    def gmlp_gate():
        state["merged"] = ma_ref[...] + jax.nn.sigmoid(col(wgb_ref, bgb_ref)) * state["acc"]

    def lru_gate():
        state["g_c"] = jax.nn.sigmoid(col(wgc_ref, bgc_ref))

    steps = ([gmlp_norm, lru_silu_z] + [functools.partial(gmlp_group, g) for g in range(N_GROUPS)]
             + [gmlp_gate, lru_gate])
    bounds = [round(k * len(steps) / nsb) for k in range(nsb + 1)]
    hprev = hcar_s[...]
    lru_input(0)
    for k in range(nsb):
        if k + 1 < nsb:
            lru_input(k + 1)
        for step in steps[bounds[k]:bounds[k + 1]]:
            step()
        hprev = lru_block(k, hprev)
    hcar_s[...] = hprev
    for jj in range(CONV_W - 1):
        ucar_s[jj] = conv_tail[jj]

    @pl.when(j == pl.num_programs(1) - 1)
    def _():
        conv_out[...] = state["x_c", nsb - 1][sb - (CONV_W - 1):, :]
        h_out[...] = hprev

    y_c = (h_s[...] * state["sz_c"]).astype(BF16)
    merged = state["merged"] + state["g_c"] * _dot(y_c, wpc_ref[...])
    out = _dot(merged.astype(BF16), wo_ref[...])
    o_ref[...] = _layer_norm(alpha * x + out, lng_ref[...], lnb_ref[...])


def _mix_prompt(x, merged, lw, layer, alpha):
    bsz, t, _ = x.shape
    tt = TT_MIX
    kern = functools.partial(_mix_kernel, tt=tt, sb=SB_MIX, alpha=alpha)
    tile = pl.BlockSpec((None, tt, D_MODEL), lambda b, j: (b, j, 0))
    blocks = [BLK_U, BLK_VB, BLK_ZB, BLK_GB, BLK_XC, BLK_ZC, BLK_GC]
    names = ["gmlp_ln_g", "gmlp_ln_b", "gmlp_ws", "gmlp_bs_t", "w_pb",
             "lru_conv_w", "lru_conv_b", "w_ax", "lru_ba", "lru_bx", "lru_lambda", "w_pc",
             "w_out", "ln_g", "ln_b"]
    return pl.pallas_call(
        kern,
        grid=(bsz, t // tt),
        in_specs=([tile, tile] + [_w_block(layer, blk) for blk in blocks]
                  + [_b_block(layer, blk) for blk in blocks] + [_at_layer(lw[n], layer) for n in names]),
        out_specs=[
            tile,
            pl.BlockSpec((None, CONV_W - 1, D_MODEL), lambda b, j: (b, 0, 0)),
            pl.BlockSpec((None, V7X_SUBLANES, D_MODEL), lambda b, j: (b, 0, 0)),
        ],
        out_shape=[
            jax.ShapeDtypeStruct((bsz, t, D_MODEL), F32),
            jax.ShapeDtypeStruct((bsz, CONV_W - 1, D_MODEL), F32),
            jax.ShapeDtypeStruct((bsz, V7X_SUBLANES, D_MODEL), F32),
        ],
        scratch_shapes=[
            pltpu.VMEM((CONV_W - 1, V7X_SUBLANES, D_MODEL), F32),
            pltpu.VMEM((tt, D_MODEL), F32),
            pltpu.VMEM((tt, D_MODEL), F32),
            pltpu.VMEM((V7X_SUBLANES, D_MODEL), F32),
        ],
        compiler_params=pltpu.CompilerParams(
            dimension_semantics=("arbitrary", "arbitrary"), vmem_limit_bytes=VMEM_LIMIT_BYTES),
        name="mix_prompt",
    )(x, merged, *([lw["w_all"]] * len(blocks)), *([lw["b_all"]] * len(blocks)), *[lw[n] for n in names])


def _proj_kernel(x_ref, w_ref, b_ref, o_ref):
    o_ref[...] = _dot(x_ref[...].astype(BF16), w_ref[...]) + b_ref[...]


def _proj(x, w, b, layer, col0, ncols, bn):
    rows = x.shape[0]
    off = col0 // bn
    return pl.pallas_call(
        _proj_kernel,
        grid=(ncols // bn,),
        in_specs=[
            pl.BlockSpec((rows, D_MODEL), lambda n: (0, 0)),
            pl.BlockSpec((None, D_MODEL, bn), lambda n: (layer, 0, off + n)),
            pl.BlockSpec((None, 1, bn), lambda n: (layer, 0, off + n)),
        ],
        out_specs=pl.BlockSpec((rows, bn), lambda n: (0, n)),
        out_shape=jax.ShapeDtypeStruct((rows, ncols), F32),
        compiler_params=pltpu.CompilerParams(
            dimension_semantics=("arbitrary",), vmem_limit_bytes=VMEM_LIMIT_BYTES),
        name="proj_sample",
    )(x, w, b)


def _mlstm_sample_kernel(p_ref, g_ref, c_ref, n_ref, m_ref, ng_ref, *rest, bb, seq):
    y_ref, c_out, n_out, m_out, qc_s = rest[-5:]
    rows = bb * seq
    t_idx = lax.broadcasted_iota(jnp.int32, (rows, 1), 0) % seq

    def down(x, d, fill=0.0):
        if d == 0:
            return x
        return jnp.where(t_idx >= d, pltpu.roll(x, d, 0), fill)

    def from_last(x):
        out = x
        for d in range(1, seq):
            out = jnp.where(t_idx == seq - 1 - d, pltpu.roll(x, rows - d, 0), out)
        return out

    it_all = g_ref[:, :V7X_LANES]
    lf_all = jax.nn.log_sigmoid(g_ref[:, V7X_LANES:])
    b_all = lf_all
    for d in range(1, seq):
        b_all = b_all + down(lf_all, d)
    g_all = it_all - b_all
    mx_all = g_all
    for d in range(1, seq):
        mx_all = jnp.maximum(mx_all, down(g_all, d, -jnp.inf))
    m0_all = m_ref[...]
    m_all = b_all + jnp.maximum(m0_all, mx_all)
    mlast_all = from_last(m_all)
    blast_all = from_last(b_all)
    inter_all = jnp.exp(b_all + m0_all - m_all)
    em_all = jnp.exp(-m_all)
    w_all = jnp.exp(blast_all + g_all - mlast_all)
    decay_all = jnp.exp(blast_all + m0_all - mlast_all)
    a_all = b_all - m_all
    m_out[...] = mlast_all

    def head_cols(blk, h):
        return slice(blk * D_MODEL + h * D_HEAD, blk * D_MODEL + (h + 1) * D_HEAD)

    for h in range(N_HEADS):
        hc = slice(h * D_HEAD, (h + 1) * D_HEAD)
        q = p_ref[:, head_cols(BLK_Q, h)]
        k = p_ref[:, head_cols(BLK_K, h)] * K_SCALE
        v = p_ref[:, head_cols(BLK_V, h)]
        lane = slice(h, h + 1)
        a_col, g_col = a_all[:, lane], g_all[:, lane]
        inter, em = inter_all[:, lane], em_all[:, lane]
        wcol, decay = w_all[:, lane], decay_all[:, lane]

        kw = k * wcol
        per_slab = V7X_SUBLANES // seq
        grp = lax.broadcasted_iota(jnp.int32, (V7X_SUBLANES, 1), 0) // seq
        for slab in range(rows // V7X_SUBLANES):
            s0 = slab * V7X_SUBLANES
            q8 = q[s0:s0 + V7X_SUBLANES, :].astype(BF16)
            kw8 = kw[s0:s0 + V7X_SUBLANES, :]
            v8 = v[s0:s0 + V7X_SUBLANES, :].astype(BF16)
            qc8 = jnp.zeros((V7X_SUBLANES, D_HEAD), F32)
            for e in range(per_slab):
                b = slab * per_slab + e
                c0 = c_ref[b, h]
                qc8 = jnp.where(grp == e, _dot(q8, c0.astype(BF16)), qc8)
                kw_e = jnp.where(grp == e, kw8, 0.0).astype(BF16)
                c_out[b, h] = decay[b * seq:b * seq + 1, :] * c0 + _dot_tn(kw_e, v8)
            qc_s[s0:s0 + V7X_SUBLANES, :] = qc8

        n0 = n_ref[:, hc]
        num = inter * qc_s[...]
        den = inter * jnp.sum(q * n0, axis=1, keepdims=True)
        for d in range(seq):
            s_d = jnp.sum(q * down(k, d), axis=1, keepdims=True) * jnp.exp(a_col + down(g_col, d))
            s_d = jnp.where(t_idx >= d, s_d, 0.0)
            num = num + s_d * down(v, d)
            den = den + s_d
        hh = num / jnp.maximum(jnp.abs(den), em)

        ksum = kw
        for d in range(1, seq):
            ksum = ksum + down(kw, d)
        n_out[:, hc] = decay * n0 + ksum

        hh = jax.nn.sigmoid(p_ref[:, head_cols(BLK_O, h)]) * hh
        mu = jnp.mean(hh, axis=-1, keepdims=True)
        hcn = hh - mu
        var = jnp.mean(hcn * hcn, axis=-1, keepdims=True)
        hn = hcn * lax.rsqrt(var + LN_EPS) * ng_ref[:, hc]
        y_ref[:, hc] = hn * _silu(p_ref[:, head_cols(BLK_ZA, h)])


def _mlstm_sample(proj_a, gates, c_all, c_new_all, n0_rows, m0_rows, lw, layer, seq):
    nb = c_all.shape[1]
    bb = BB_MLSTM_SAMPLE
    rows = bb * seq
    kern = functools.partial(_mlstm_sample_kernel, bb=bb, seq=seq)
    c_spec = pl.BlockSpec((None, bb, N_HEADS, D_HEAD, D_HEAD), lambda i: (layer, i, 0, 0, 0))
    in_specs = [
        pl.BlockSpec((rows, 5 * D_MODEL), lambda i: (i, 0)),
        pl.BlockSpec((rows, 2 * V7X_LANES), lambda i: (i, 0)),
        c_spec,
        pl.BlockSpec((rows, D_MODEL), lambda i: (i, 0)),
        pl.BlockSpec((rows, V7X_LANES), lambda i: (i, 0)),
        _at_layer(lw["norm_g"], layer),
    ]
    args = [proj_a, gates, c_all, n0_rows, m0_rows, lw["norm_g"]]
    aliases = {}
    if c_new_all is not None:
        in_specs.append(pl.BlockSpec(memory_space=pl.ANY))
        args.append(c_new_all)
        aliases = {len(args) - 1: 1}
    return pl.pallas_call(
        kern,
        grid=(nb // bb,),
        in_specs=in_specs,
        out_specs=[
            pl.BlockSpec((rows, D_MODEL), lambda i: (i, 0)),
            c_spec,
            pl.BlockSpec((rows, D_MODEL), lambda i: (i, 0)),
            pl.BlockSpec((rows, V7X_LANES), lambda i: (i, 0)),
        ],
        out_shape=[
            jax.ShapeDtypeStruct((nb * seq, D_MODEL), F32),
            jax.ShapeDtypeStruct(c_all.shape, F32),
            jax.ShapeDtypeStruct((nb * seq, D_MODEL), F32),
            jax.ShapeDtypeStruct((nb * seq, V7X_LANES), F32),
        ],
        scratch_shapes=[pltpu.VMEM((rows, D_HEAD), F32)],
        input_output_aliases=aliases,
        compiler_params=pltpu.CompilerParams(
            dimension_semantics=("arbitrary",), vmem_limit_bytes=VMEM_LIMIT_BYTES),
        name="mlstm_sample",
    )(*args)


def _rest_sample_kernel(x_ref, ya_ref, p_ref, buf_ref, h0_ref,
                        wsm_ref, bsm_ref,
                        lng1_ref, lnb1_ref, cw_ref, cb_ref, wax_ref, ba_ref, bx_ref, lam_ref,
                        wpa_ref, wpb_ref, wpc_ref, wo_ref, lng_ref, lnb_ref,
                        o_ref, vn_out, conv_out, h_out, *, seq, bbs, alpha):
    rows = seq * bbs

    def flat(blk):
        lo = (blk - BLK_U) * D_MODEL
        return p_ref[:, :, lo:lo + D_MODEL].reshape(rows, D_MODEL)

    vn = _layer_norm(flat(BLK_VB), lng1_ref[...], lnb1_ref[...])
    vn_out[...] = vn.reshape(seq, bbs, D_MODEL)
    mixed_t = []
    for t in range(seq):
        parts = []
        for g in range(N_GROUPS):
            cols = slice(g * D_GROUP, (g + 1) * D_GROUP)
            acc = jnp.full((bbs, D_GROUP), bsm_ref[g * seq + t], F32)
            for s in range(t + 1):
                acc = acc + wsm_ref[(g * seq + t) * seq + s] * vn[s * bbs:(s + 1) * bbs, cols]
            parts.append(acc)
        mixed_t.append(jnp.concatenate(parts, axis=1))
    mixed = jnp.concatenate(mixed_t, axis=0)
    yb = (flat(BLK_U) * mixed * _silu(flat(BLK_ZB))).astype(BF16)

    xc_raw = flat(BLK_XC)
    xp = [buf_ref[t] for t in range(CONV_W - 1)] + [xc_raw[t * bbs:(t + 1) * bbs] for t in range(seq)]
    for t in range(CONV_W - 1):
        conv_out[t] = xp[seq + t]
    xc_t = []
    for t in range(seq):
        acc = cb_ref[...] + cw_ref[0:1, :] * xp[t]
        for jj in range(1, CONV_W):
            acc = acc + cw_ref[jj:jj + 1, :] * xp[t + jj]
        xc_t.append(acc)
    xc = jnp.concatenate(xc_t, axis=0)
    a, mult, i = _lru_gates(xc, wax_ref, ba_ref[...], bx_ref[...], lam_ref[...])
    bterm = mult * i * xc
    h = h0_ref[...]
    hs = []
    for t in range(seq):
        h = a[t * bbs:(t + 1) * bbs] * h + bterm[t * bbs:(t + 1) * bbs]
        hs.append(h)
    h_out[...] = h
    yc = (jnp.concatenate(hs, axis=0) * _silu(flat(BLK_ZC))).astype(BF16)

    ya = ya_ref[...].reshape(rows, D_MODEL).astype(BF16)
    merged = (jax.nn.sigmoid(flat(BLK_GA)) * _dot(ya, wpa_ref[...])
              + jax.nn.sigmoid(flat(BLK_GB)) * _dot(yb, wpb_ref[...])
              + jax.nn.sigmoid(flat(BLK_GC)) * _dot(yc, wpc_ref[...]))
    out = _dot(merged.astype(BF16), wo_ref[...])
    x = x_ref[...].reshape(rows, D_MODEL)
    o_ref[...] = _layer_norm(alpha * x + out, lng_ref[...], lnb_ref[...]).reshape(seq, bbs, D_MODEL)


def _rest_sample(x_tm, ya_tm, p_tm, buf_tm, h0, lw, layer, alpha):
    seq, nb, _ = x_tm.shape
    bbs = min(BB_REST_SAMPLE, nb)
    kern = functools.partial(_rest_sample_kernel, seq=seq, bbs=bbs, alpha=alpha)

    def tm(n, lead=seq):
        return pl.BlockSpec((lead, bbs, n), lambda i: (0, i, 0))

    smem = pl.BlockSpec(memory_space=pltpu.SMEM)
    names = ["gmlp_ln_g", "gmlp_ln_b", "lru_conv_w", "lru_conv_b", "w_ax", "lru_ba", "lru_bx", "lru_lambda",
             "w_pa", "w_pb", "w_pc", "w_out", "ln_g", "ln_b"]
    return pl.pallas_call(
        kern,
        grid=(nb // bbs,),
        in_specs=[tm(D_MODEL), tm(D_MODEL), tm(p_tm.shape[-1]),
                  tm(D_MODEL, CONV_W - 1), pl.BlockSpec((bbs, D_MODEL), lambda i: (i, 0)),
                  smem, smem] + [_at_layer(lw[n], layer) for n in names],
        out_specs=[tm(D_MODEL), tm(D_MODEL), tm(D_MODEL, CONV_W - 1),
                   pl.BlockSpec((bbs, D_MODEL), lambda i: (i, 0))],
        out_shape=[
            jax.ShapeDtypeStruct((seq, nb, D_MODEL), F32),
            jax.ShapeDtypeStruct((seq, nb, D_MODEL), F32),
            jax.ShapeDtypeStruct((CONV_W - 1, nb, D_MODEL), F32),
            jax.ShapeDtypeStruct((nb, D_MODEL), F32),
        ],
        compiler_params=pltpu.CompilerParams(
            dimension_semantics=("arbitrary",), vmem_limit_bytes=VMEM_LIMIT_BYTES),
        name="rest_sample",
    )(x_tm, ya_tm, p_tm, buf_tm, h0, lw["ws_small"][layer], lw["bs_small"][layer],
      *[lw[n] for n in names])


def _pack_w_kernel(lo_ref, hi_ref, o_ref, *, shift):
    blk = pl.program_id(1)

    @pl.when(blk < BLK_U)
    def _():
        o_ref[...] = lo_ref[...].astype(BF16)

    @pl.when(blk >= BLK_U)
    def _():
        both = jnp.concatenate([lo_ref[...], hi_ref[...]], axis=1)
        o_ref[...] = both[:, shift:shift + D_MODEL].astype(BF16)


def _pack_w(w_in):
    depth = w_in.shape[0]
    lanes_per_blk = D_MODEL // V7X_LANES
    kern = functools.partial(_pack_w_kernel, shift=2 * N_HEADS)
    return pl.pallas_call(
        kern,
        grid=(depth, N_BLOCKS),
        in_specs=[
            pl.BlockSpec((None, D_MODEL, D_MODEL), lambda l, k: (l, 0, k)),
            pl.BlockSpec((None, D_MODEL, V7X_LANES), lambda l, k: (l, 0, (k + 1) * lanes_per_blk)),
        ],
        out_specs=pl.BlockSpec((None, D_MODEL, D_MODEL), lambda l, k: (l, 0, k)),
        out_shape=jax.ShapeDtypeStruct((depth, D_MODEL, N_BLOCKS * D_MODEL), BF16),
        compiler_params=pltpu.CompilerParams(
            dimension_semantics=("arbitrary", "arbitrary"), vmem_limit_bytes=VMEM_LIMIT_BYTES),
        name="pack_w",
    )(w_in, w_in)


def _pack_params(w_in, b_in, mlstm_norm_g, gmlp_ln_g, gmlp_ln_b, gmlp_ws, gmlp_bs, lru_conv_w, lru_conv_b,
                 lru_wa, lru_ba, lru_wx, lru_bx, lru_lambda, w_proj_a, w_proj_b, w_proj_c, w_out, ln_g, ln_b,
                 seq_s):
    depth = w_in.shape[0]
    d = D_MODEL
    o_i = 5 * d
    o_f = o_i + N_HEADS
    o_rest = o_f + N_HEADS

    def row(a):
        return a[:, None, :]

    def pad_lanes(a, n):
        return jnp.pad(a, [(0, 0)] * (a.ndim - 1) + [(0, n - a.shape[-1])])

    def drop_gate_cols(a):
        return jnp.concatenate([a[..., :o_i], a[..., o_rest:]], axis=-1)

    wi, wf = w_in[..., o_i:o_f], w_in[..., o_f:o_rest]
    bi, bf = b_in[..., o_i:o_f], b_in[..., o_f:o_rest]
    pad_rows = [(0, 0), (0, V7X_SUBLANES - N_HEADS), (0, 0)]
    lw = {
        "w_all": _pack_w(w_in),
        "b_all": row(drop_gate_cols(b_in)),
        "w_kt": jnp.swapaxes(w_in[..., BLK_K * d:(BLK_K + 1) * d], 1, 2).astype(BF16),
        "b_kt": b_in[:, BLK_K * d:(BLK_K + 1) * d, None],
        "w_gc": jnp.concatenate([pad_lanes(wi, V7X_LANES), pad_lanes(wf, V7X_LANES)], axis=-1).astype(BF16),
        "b_gc": row(jnp.concatenate([pad_lanes(bi, V7X_LANES), pad_lanes(bf, V7X_LANES)], axis=-1)),
        "w_gr": jnp.concatenate([jnp.pad(jnp.swapaxes(wi, 1, 2), pad_rows),
                                 jnp.pad(jnp.swapaxes(wf, 1, 2), pad_rows)], axis=1).astype(BF16),
        "b_gr": jnp.concatenate([jnp.pad(bi[:, :, None], pad_rows),
                                 jnp.pad(bf[:, :, None], pad_rows)], axis=1),
        "norm_g": row(mlstm_norm_g),
        "gmlp_ln_g": row(gmlp_ln_g), "gmlp_ln_b": row(gmlp_ln_b),
        "gmlp_ws": gmlp_ws,
        "gmlp_bs_t": jnp.swapaxes(gmlp_bs, 1, 2),
        "ws_small": gmlp_ws[:, :, :seq_s, :seq_s].reshape(depth, -1),
        "bs_small": gmlp_bs[:, :, :seq_s].reshape(depth, -1),
        "lru_conv_w": lru_conv_w, "lru_conv_b": row(lru_conv_b),
        "w_ax": jnp.concatenate([lru_wa, lru_wx], axis=-1).astype(BF16),
        "lru_ba": row(lru_ba), "lru_bx": row(lru_bx), "lru_lambda": row(lru_lambda),
        "w_pa": w_proj_a.astype(BF16), "w_pb": w_proj_b.astype(BF16), "w_pc": w_proj_c.astype(BF16),
        "w_out": w_out.astype(BF16),
        "ln_g": row(ln_g), "ln_b": row(ln_b),
    }
    return lw


def _prompt_layer(x, lw, layer, alpha):
    merged, c, n, m = _mlstm_prompt(x, lw, layer)
    x_new, conv, h = _mix_prompt(x, merged, lw, layer, alpha)
    return x_new, c, n[:, :N_HEADS], m[:, 0, :N_HEADS], conv, h[:, 0]


def _sample_layer(x_bm, c_all, c_new_all, n0, m0, conv_buf, h0, lw, layer, alpha):
    nb, seq, d = x_bm.shape
    x_rows = x_bm.reshape(nb * seq, d)
    x_tm = jnp.swapaxes(x_bm, 0, 1)
    x_tm_rows = x_tm.reshape(seq * nb, d)
    n_rest = N_BLOCKS - BLK_U
    proj_a = _proj(x_rows, lw["w_all"], lw["b_all"], layer, 0, BLK_U * d, d)
    gates = _proj(x_rows, lw["w_gc"], lw["b_gc"], layer, 0, 2 * V7X_LANES, 2 * V7X_LANES)
    p_tm = _proj(x_tm_rows, lw["w_all"], lw["b_all"], layer, BLK_U * d, n_rest * d, d)
    p_tm = p_tm.reshape(seq, nb, n_rest * d)

    n0_rows = jnp.repeat(n0.reshape(nb, d), seq, axis=0)
    m0_rows = jnp.repeat(jnp.pad(m0, ((0, 0), (0, V7X_LANES - N_HEADS))), seq, axis=0)
    ya, c_new_all, n_rows, m_rows = _mlstm_sample(
        proj_a, gates, c_all, c_new_all, n0_rows, m0_rows, lw, layer, seq)
    n_new = n_rows.reshape(nb, seq, N_HEADS, D_HEAD)[:, seq - 1]
    m_new = m_rows.reshape(nb, seq, V7X_LANES)[:, seq - 1, :N_HEADS]

    ya_tm = jnp.swapaxes(ya.reshape(nb, seq, d), 0, 1)
    buf_tm = jnp.swapaxes(conv_buf, 0, 1)
    x_new_tm, vn_tm, conv_tm, h_new = _rest_sample(x_tm, ya_tm, p_tm, buf_tm, h0, lw, layer, alpha)
    return (jnp.swapaxes(x_new_tm, 0, 1), c_new_all, n_new, m_new, jnp.swapaxes(conv_tm, 0, 1), h_new,
            jnp.swapaxes(vn_tm, 0, 1))


def kernel(x_prompt, x_sample, state_mlstm_c, state_mlstm_n, state_mlstm_m, state_lru_conv, state_lru_h,
           w_in, b_in, mlstm_norm_g, gmlp_ln_g, gmlp_ln_b, gmlp_ws, gmlp_bs, lru_conv_w, lru_conv_b,
           lru_wa, lru_ba, lru_wx, lru_bx, lru_lambda, w_proj_a, w_proj_b, w_proj_c, w_out, ln_g, ln_b):
    depth = w_in.shape[0]
    alpha = float((2 * depth) ** 0.25)
    seq_s = x_sample.shape[1]
    assert V7X_SUBLANES % seq_s == 0 and x_prompt.shape[1] % TT_MLSTM == 0
    lw = _pack_params(w_in, b_in, mlstm_norm_g, gmlp_ln_g, gmlp_ln_b, gmlp_ws, gmlp_bs, lru_conv_w,
                      lru_conv_b, lru_wa, lru_ba, lru_wx, lru_bx, lru_lambda, w_proj_a, w_proj_b, w_proj_c,
                      w_out, ln_g, ln_b, seq_s)
    xp, xs = x_prompt, x_sample
    outs_p = [[] for _ in range(5)]
    outs_s = [[] for _ in range(5)]
    c_new_all = None
    for layer in range(depth):
        xp, *st = _prompt_layer(xp, lw, layer, alpha)
        for acc, val in zip(outs_p, st):
            acc.append(val)
        xs, c_new_all, *st = _sample_layer(
            xs, state_mlstm_c, c_new_all, state_mlstm_n[layer], state_mlstm_m[layer],
            state_lru_conv[layer], state_lru_h[layer], lw, layer, alpha)
        for acc, val in zip(outs_s, st):
            acc.append(val)
    stacked_p = [jnp.stack(v) for v in outs_p]
    stacked_s = [jnp.stack(v) for v in outs_s]
    return (xp, xs, *stacked_p, c_new_all, *stacked_s)
```

```python
import functools

import jax
import jax.numpy as jnp
from jax import lax
from jax.experimental import pallas as pl
from jax.experimental.pallas import tpu as pltpu

D_MODEL = 1024
N_HEADS = 4
D_HEAD = D_MODEL // N_HEADS
CHUNK = 128
N_GROUPS = 4
D_GROUP = D_MODEL // N_GROUPS
LRU_BLOCKS = 8
LRU_BLK = D_MODEL // LRU_BLOCKS
CONV_W = 4
LRU_C = 8.0
LN_EPS = 1e-5
K_SCALE = D_HEAD ** -0.5
LOG2_E = 1.4426950408889634
V7X_LANES = 128
V7X_SUBLANES = 8
VMEM_LIMIT_BYTES = 56 * 1024 * 1024

TT_MLSTM = 512
TT_MIX = 512
SB_MIX = 512
BB_MLSTM_SAMPLE = 8
BB_REST_SAMPLE = 64

BLK_Q, BLK_K, BLK_V, BLK_O, BLK_ZA = 0, 1, 2, 3, 4
BLK_U, BLK_VB, BLK_ZB = 5, 6, 7
BLK_XC, BLK_ZC = 8, 9
BLK_GA, BLK_GB, BLK_GC = 10, 11, 12
N_BLOCKS = 13

BF16 = jnp.bfloat16
F32 = jnp.float32

_NT = (((1,), (1,)), ((), ()))
_TN = (((0,), (0,)), ((), ()))


def _dot(a, b):
    return jnp.dot(a, b, preferred_element_type=F32)


def _dot_nt(a, b):
    return lax.dot_general(a, b, _NT, preferred_element_type=F32)


def _dot_tn(a, b):
    return lax.dot_general(a, b, _TN, preferred_element_type=F32)


def _layer_norm(x, g, b):
    mu = jnp.mean(x, axis=-1, keepdims=True)
    xc = x - mu
    var = jnp.mean(xc * xc, axis=-1, keepdims=True)
    return xc * lax.rsqrt(var + LN_EPS) * g + b


def _silu(x):
    return x * jax.nn.sigmoid(x)


def _scan_axis(x, axis, op, fill):
    n = x.shape[axis]
    idx = lax.broadcasted_iota(jnp.int32, x.shape, axis)
    s = 1
    while s < n:
        x = op(x, jnp.where(idx >= s, pltpu.roll(x, s, axis), fill))
        s *= 2
    return x


def _at_layer(arr, layer):
    shape = arr.shape[1:]
    nd = len(shape)
    return pl.BlockSpec((None,) + shape, lambda *_: (layer,) + (0,) * nd, pipeline_mode=pl.Buffered(1))


def _w_block(layer, blk):
    return pl.BlockSpec((None, D_MODEL, D_MODEL), lambda *_: (layer, 0, blk), pipeline_mode=pl.Buffered(1))


def _b_block(layer, blk):
    return pl.BlockSpec((None, 1, D_MODEL), lambda *_: (layer, 0, blk), pipeline_mode=pl.Buffered(1))


def _mlstm_prompt_kernel(x_ref, wq_ref, wv_ref, wo_ref, wz_ref, wg_ref, bq_ref, bv_ref, bo_ref, bz_ref, bg_ref,
                         wkt_ref, bkt_ref, wgc_ref, bgc_ref, wgr_ref, bgr_ref, ng_ref, wp_ref,
                         o_ref, c_out, n_out, m_out,
                         c_s, n_s, mc_s, mr_s, q_s, kt_s, v_s, o_s, h_s,
                         acol_s, inter_s, em_s, decay_s, grow_s, wrow_s, *, tt):
    j = pl.program_id(1)
    nch = tt // CHUNK

    @pl.when(j == 0)
    def _():
        c_s[...] = jnp.zeros_like(c_s)
        n_s[...] = jnp.zeros_like(n_s)
        mc_s[...] = jnp.zeros_like(mc_s)
        mr_s[...] = jnp.zeros_like(mr_s)

    xb = x_ref[...].astype(BF16)

    def col(w_ref, b_ref):
        return _dot(xb, w_ref[...]) + b_ref[...]

    def project_q():
        q_s[...] = col(wq_ref, bq_ref).astype(BF16)

    def project_k():
        kt_s[...] = (_dot_nt(wkt_ref[...], xb) + bkt_ref[...]) * K_SCALE

    def project_v():
        v_s[...] = col(wv_ref, bv_ref).astype(BF16)

    def project_o():
        o_s[...] = jax.nn.sigmoid(col(wo_ref, bo_ref))

    projections = [project_q, project_k, project_v, project_o]

    gcol = _dot(xb, wgc_ref[...]) + bgc_ref[...]
    it_c = gcol[:, :V7X_LANES]
    lf_c = jax.nn.log_sigmoid(gcol[:, V7X_LANES:])
    grow = _dot_nt(wgr_ref[...], xb) + bgr_ref[...]
    it_r = grow[:V7X_SUBLANES]
    lf_r = jax.nn.log_sigmoid(grow[V7X_SUBLANES:])

    m0c = mc_s[0:1, :]
    m0r = mr_s[...]
    for c in range(nch):
        sl = slice(c * CHUNK, (c + 1) * CHUNK)
        bcol = _scan_axis(lf_c[sl], 0, jnp.add, 0.0)
        gcl = it_c[sl] - bcol
        mcol = bcol + jnp.maximum(m0c, _scan_axis(gcl, 0, jnp.maximum, -jnp.inf))
        blast = bcol[CHUNK - 1:CHUNK]
        mlast = mcol[CHUNK - 1:CHUNK]
        acol = bcol - mcol
        inter = jnp.exp(bcol + m0c - mcol)
        em = jnp.exp(-mcol)
        decay_s[c:c + 1, :] = jnp.exp(blast + m0c - mlast)
        for h in range(N_HEADS):
            acol_s[h, sl, :] = jnp.broadcast_to(acol[:, h:h + 1], (CHUNK, V7X_LANES))
            inter_s[h, sl, :] = jnp.broadcast_to(inter[:, h:h + 1], (CHUNK, V7X_LANES))
            em_s[h, sl, :] = jnp.broadcast_to(em[:, h:h + 1], (CHUNK, V7X_LANES))
        brow = _scan_axis(lf_r[:, sl], 1, jnp.add, 0.0)
        grw = it_r[:, sl] - brow
        mrow = brow + jnp.maximum(m0r, _scan_axis(grw, 1, jnp.maximum, -jnp.inf))
        mlast_r = mrow[:, CHUNK - 1:CHUNK]
        grow_s[:, sl] = grw
        wrow_s[:, sl] = jnp.exp(brow[:, CHUNK - 1:CHUNK] + grw - mlast_r)
        m0c = mlast
        m0r = jnp.broadcast_to(mlast_r, m0r.shape)
        if c < len(projections):
            projections[c]()
    for project in projections[nch:]:
        project()
    mc_s[...] = jnp.broadcast_to(m0c, mc_s.shape)
    mr_s[...] = m0r

    tri = (lax.broadcasted_iota(jnp.int32, (CHUNK, CHUNK), 0)
           >= lax.broadcasted_iota(jnp.int32, (CHUNK, CHUNK), 1))
    ones_rows = jnp.ones((V7X_SUBLANES, CHUNK), BF16)

    for c in range(nch):
        rows = slice(c * CHUNK, (c + 1) * CHUNK)
        decay_c = decay_s[c:c + 1, :]
        for h in range(N_HEADS):
            hc = slice(h * D_HEAD, (h + 1) * D_HEAD)
            qc = q_s[rows, hc]
            kt = kt_s[hc, rows]
            vc = v_s[rows, hc]
            inter = inter_s[h, rows, :]
            decay = decay_c[:, h:h + 1]

            dmat = jnp.exp(jnp.where(tri, acol_s[h, rows, :] + grow_s[h:h + 1, rows], -jnp.inf))
            s = _dot(qc, kt.astype(BF16)) * dmat
            c0 = c_s[h]
            n0 = n_s[h:h + 1, :]
            num = (jnp.concatenate([inter, inter], axis=1) * _dot(qc, c0.astype(BF16))
                   + _dot(s.astype(BF16), vc))
            qn = qc.astype(F32) * n0
            den = jnp.sum(inter * (qn[:, :V7X_LANES] + qn[:, V7X_LANES:]) + s, axis=1, keepdims=True)
            rden = 1.0 / jnp.maximum(jnp.abs(den), em_s[h, rows, :])
            h_s[rows, hc] = num * jnp.concatenate([rden, rden], axis=1)

            kwt = (kt * wrow_s[h:h + 1, rows]).astype(BF16)
            c_s[h] = decay * c0 + _dot(kwt, vc)
            n_s[h:h + 1, :] = decay * n0 + _dot_nt(ones_rows, kwt)[0:1, :]

    hh = o_s[...] * h_s[...]
    parts = []
    for h in range(N_HEADS):
        hc = slice(h * D_HEAD, (h + 1) * D_HEAD)
        hd = hh[:, hc]
        mu = jnp.mean(hd, axis=-1, keepdims=True)
        hd = hd - mu
        var = jnp.mean(hd * hd, axis=-1, keepdims=True)
        parts.append(hd * lax.rsqrt(var + LN_EPS) * ng_ref[:, hc])
    y = (jnp.concatenate(parts, axis=1) * _silu(col(wz_ref, bz_ref))).astype(BF16)
    o_ref[...] = jax.nn.sigmoid(col(wg_ref, bg_ref)) * _dot(y, wp_ref[...])

    @pl.when(j == pl.num_programs(1) - 1)
    def _():
        c_out[...] = c_s[...]
        n_out[...] = n_s[...]
        m_out[...] = mc_s[...]


def _mlstm_prompt(x, lw, layer):
    bsz, t, _ = x.shape
    tt = TT_MLSTM
    kern = functools.partial(_mlstm_prompt_kernel, tt=tt)
    tile = pl.BlockSpec((None, tt, D_MODEL), lambda b, j: (b, j, 0))
    blocks = [BLK_Q, BLK_V, BLK_O, BLK_ZA, BLK_GA]
    names = ["w_kt", "b_kt", "w_gc", "b_gc", "w_gr", "b_gr", "norm_g", "w_pa"]
    head_lanes = (N_HEADS, tt, V7X_LANES)
    return pl.pallas_call(
        kern,
        grid=(bsz, t // tt),
        in_specs=([tile] + [_w_block(layer, blk) for blk in blocks] + [_b_block(layer, blk) for blk in blocks]
                  + [_at_layer(lw[n], layer) for n in names]),
        out_specs=[
            tile,
            pl.BlockSpec((None, N_HEADS, D_HEAD, D_HEAD), lambda b, j: (b, 0, 0, 0)),
            pl.BlockSpec((None, V7X_SUBLANES, D_HEAD), lambda b, j: (b, 0, 0)),
            pl.BlockSpec((None, V7X_SUBLANES, V7X_LANES), lambda b, j: (b, 0, 0)),
        ],
        out_shape=[
            jax.ShapeDtypeStruct((bsz, t, D_MODEL), F32),
            jax.ShapeDtypeStruct((bsz, N_HEADS, D_HEAD, D_HEAD), F32),
            jax.ShapeDtypeStruct((bsz, V7X_SUBLANES, D_HEAD), F32),
            jax.ShapeDtypeStruct((bsz, V7X_SUBLANES, V7X_LANES), F32),
        ],
        scratch_shapes=[
            pltpu.VMEM((N_HEADS, D_HEAD, D_HEAD), F32),
            pltpu.VMEM((V7X_SUBLANES, D_HEAD), F32),
            pltpu.VMEM((V7X_SUBLANES, V7X_LANES), F32),
            pltpu.VMEM((V7X_SUBLANES, V7X_LANES), F32),
            pltpu.VMEM((tt, D_MODEL), BF16),
            pltpu.VMEM((D_MODEL, tt), F32),
            pltpu.VMEM((tt, D_MODEL), BF16),
            pltpu.VMEM((tt, D_MODEL), F32),
            pltpu.VMEM((tt, D_MODEL), F32),
            pltpu.VMEM(head_lanes, F32),
            pltpu.VMEM(head_lanes, F32),
            pltpu.VMEM(head_lanes, F32),
            pltpu.VMEM((V7X_SUBLANES, V7X_LANES), F32),
            pltpu.VMEM((V7X_SUBLANES, tt), F32),
            pltpu.VMEM((V7X_SUBLANES, tt), F32),
        ],
        compiler_params=pltpu.CompilerParams(
            dimension_semantics=("arbitrary", "arbitrary"), vmem_limit_bytes=VMEM_LIMIT_BYTES),
        name="mlstm_prompt",
    )(x, *([lw["w_all"]] * len(blocks)), *([lw["b_all"]] * len(blocks)), *[lw[n] for n in names])


def _lru_gates(xc, wax_ref, ba, bx, lam):
    xcb = xc.astype(BF16)
    pre = [_dot(xcb[:, n * LRU_BLK:(n + 1) * LRU_BLK], wax_ref[n]) for n in range(LRU_BLOCKS)]
    r = jax.nn.sigmoid(jnp.concatenate([p[:, :LRU_BLK] for p in pre], axis=1) + ba)
    i = jax.nn.sigmoid(jnp.concatenate([p[:, LRU_BLK:] for p in pre], axis=1) + bx)
    a = jnp.exp2(r * (LRU_C * LOG2_E * jax.nn.log_sigmoid(lam)))
    v = 1.0 - a * a
    mult = jnp.where(v > 0.0, v * lax.rsqrt(v), 0.0)
    return a, mult, i


def _mix_kernel(x_ref, ma_ref,
                wu_ref, wvb_ref, wzb_ref, wgb_ref, wxc_ref, wzc_ref, wgc_ref,
                bu_ref, bvb_ref, bzb_ref, bgb_ref, bxc_ref, bzc_ref, bgc_ref,
                lng1_ref, lnb1_ref, ws_ref, bs_ref, wpb_ref,
                cw_ref, cb_ref, wax_ref, ba_ref, bx_ref, lam_ref, wpc_ref,
                wo_ref, lng_ref, lnb_ref,
                o_ref, conv_out, h_out,
                ucar_s, vn_s, h_s, hcar_s, *, tt, sb, alpha):
    j = pl.program_id(1)
    nch = tt // CHUNK
    nsb = tt // sb

    @pl.when(j == 0)
    def _():
        ucar_s[...] = jnp.broadcast_to(cb_ref[...], ucar_s.shape)
        hcar_s[...] = jnp.zeros_like(hcar_s)

    x = x_ref[...]
    xb = x.astype(BF16)
    state = {}

    def col(w_ref, b_ref, cols=slice(None), rows=slice(None)):
        return _dot(xb[rows], w_ref[:, cols]) + b_ref[:, cols]

    ngs = sb // V7X_SUBLANES
    sub = lax.broadcasted_iota(jnp.int32, (ngs, V7X_SUBLANES, D_MODEL), 1)
    conv_tail = [ucar_s[jj] for jj in range(CONV_W - 1)]

    def lru_input(k):
        state["x_c", k] = col(wxc_ref, bxc_ref, rows=slice(k * sb, (k + 1) * sb))

    def lru_block(k, hprev):
        r0 = k * sb
        x3 = state["x_c", k].reshape(ngs, V7X_SUBLANES, D_MODEL)
        u = cb_ref[...] + cw_ref[0:1, :] * x3
        for jj in range(1, CONV_W):
            rot = pltpu.roll(u, 1, 1)
            prev = jnp.concatenate([conv_tail[jj - 1][None], rot[:-1]], axis=0)
            conv_tail[jj - 1] = rot[ngs - 1]
            u = jnp.where(sub == 0, prev, rot) + cw_ref[jj:jj + 1, :] * x3
        xc = u.reshape(sb, D_MODEL)
        a, mult, i = _lru_gates(xc, wax_ref, ba_ref[...], bx_ref[...], lam_ref[...])
        if k == 0:
            row = lax.broadcasted_iota(jnp.int32, (sb, 1), 0)
            mult = jnp.where(jnp.logical_and(row == 0, j == 0), 1.0, mult)
        bterm = mult * i * xc
        a3 = a.reshape(ngs, V7X_SUBLANES, D_MODEL)
        b3 = bterm.reshape(ngs, V7X_SUBLANES, D_MODEL)
        s = 1
        while s < V7X_SUBLANES:
            keep = sub >= s
            a_sh = jnp.where(keep, pltpu.roll(a3, s, 1), 1.0)
            b_sh = jnp.where(keep, pltpu.roll(b3, s, 1), 0.0)
            b3 = a3 * b_sh + b3
            a3 = a3 * a_sh
            s *= 2
        for g in range(ngs):
            hg = a3[g] * hprev + b3[g]
            h_s[r0 + g * V7X_SUBLANES:r0 + (g + 1) * V7X_SUBLANES, :] = hg
            hprev = jnp.broadcast_to(hg[V7X_SUBLANES - 1:V7X_SUBLANES, :], hg.shape)
        return hprev

    tri = (lax.broadcasted_iota(jnp.int32, (CHUNK, CHUNK), 0)
           >= lax.broadcasted_iota(jnp.int32, (CHUNK, CHUNK), 1))

    def gmlp_norm():
        vn_s[...] = _layer_norm(col(wvb_ref, bvb_ref), lng1_ref[...], lnb1_ref[...])

    def gmlp_group(g):
        wm = jnp.where(tri, ws_ref[g], 0.0).astype(BF16)
        bias = bs_ref[:, g:g + 1]
        cols = slice(g * D_GROUP, (g + 1) * D_GROUP)
        mixed = jnp.concatenate(
            [_dot(wm, vn_s[c * CHUNK:(c + 1) * CHUNK, cols].astype(BF16)) + bias for c in range(nch)],
            axis=0)
        y = (col(wu_ref, bu_ref, cols) * mixed * _silu(col(wzb_ref, bzb_ref, cols))).astype(BF16)
        contrib = _dot(y, wpb_ref[cols, :])
        state["acc"] = contrib if g == 0 else state["acc"] + contrib

    def lru_silu_z():
        state["sz_c"] = _silu(col(wzc_ref, bzc_ref))

    def gmlp_gate():
        state["merged"] = ma_ref[...] + jax.nn.sigmoid(col(wgb_ref, bgb_ref)) * state["acc"]

    def lru_gate():
        state["g_c"] = jax.nn.sigmoid(col(wgc_ref, bgc_ref))

    steps = ([gmlp_norm, lru_silu_z] + [functools.partial(gmlp_group, g) for g in range(N_GROUPS)]
             + [gmlp_gate, lru_gate])
    bounds = [round(k * len(steps) / nsb) for k in range(nsb + 1)]
    hprev = hcar_s[...]
    lru_input(0)
    for k in range(nsb):
        if k + 1 < nsb:
            lru_input(k + 1)
        for step in steps[bounds[k]:bounds[k + 1]]:
            step()
        hprev = lru_block(k, hprev)
    hcar_s[...] = hprev
    for jj in range(CONV_W - 1):
        ucar_s[jj] = conv_tail[jj]

    @pl.when(j == pl.num_programs(1) - 1)
    def _():
        conv_out[...] = state["x_c", nsb - 1][sb - (CONV_W - 1):, :]
        h_out[...] = hprev

    y_c = (h_s[...] * state["sz_c"]).astype(BF16)
    merged = state["merged"] + state["g_c"] * _dot(y_c, wpc_ref[...])
    out = _dot(merged.astype(BF16), wo_ref[...])
    o_ref[...] = _layer_norm(alpha * x + out, lng_ref[...], lnb_ref[...])


def _mix_prompt(x, merged, lw, layer, alpha):
    bsz, t, _ = x.shape
    tt = TT_MIX
    kern = functools.partial(_mix_kernel, tt=tt, sb=SB_MIX, alpha=alpha)
    tile = pl.BlockSpec((None, tt, D_MODEL), lambda b, j: (b, j, 0))
    blocks = [BLK_U, BLK_VB, BLK_ZB, BLK_GB, BLK_XC, BLK_ZC, BLK_GC]
    names = ["gmlp_ln_g", "gmlp_ln_b", "gmlp_ws", "gmlp_bs_t", "w_pb",
             "lru_conv_w", "lru_conv_b", "w_ax", "lru_ba", "lru_bx", "lru_lambda", "w_pc",
             "w_out", "ln_g", "ln_b"]
    return pl.pallas_call(
        kern,
        grid=(bsz, t // tt),
        in_specs=([tile, tile] + [_w_block(layer, blk) for blk in blocks]
                  + [_b_block(layer, blk) for blk in blocks] + [_at_layer(lw[n], layer) for n in names]),
        out_specs=[
            tile,
            pl.BlockSpec((None, CONV_W - 1, D_MODEL), lambda b, j: (b, 0, 0)),
            pl.BlockSpec((None, V7X_SUBLANES, D_MODEL), lambda b, j: (b, 0, 0)),
        ],
        out_shape=[
            jax.ShapeDtypeStruct((bsz, t, D_MODEL), F32),
            jax.ShapeDtypeStruct((bsz, CONV_W - 1, D_MODEL), F32),
            jax.ShapeDtypeStruct((bsz, V7X_SUBLANES, D_MODEL), F32),
        ],
        scratch_shapes=[
            pltpu.VMEM((CONV_W - 1, V7X_SUBLANES, D_MODEL), F32),
            pltpu.VMEM((tt, D_MODEL), F32),
            pltpu.VMEM((tt, D_MODEL), F32),
            pltpu.VMEM((V7X_SUBLANES, D_MODEL), F32),
        ],
        compiler_params=pltpu.CompilerParams(
            dimension_semantics=("arbitrary", "arbitrary"), vmem_limit_bytes=VMEM_LIMIT_BYTES),
        name="mix_prompt",
    )(x, merged, *([lw["w_all"]] * len(blocks)), *([lw["b_all"]] * len(blocks)), *[lw[n] for n in names])


def _proj_kernel(x_ref, w_ref, b_ref, o_ref):
    o_ref[...] = _dot(x_ref[...].astype(BF16), w_ref[...]) + b_ref[...]


def _proj(x, w, b, layer, col0, ncols, bn):
    rows = x.shape[0]
    off = col0 // bn
    return pl.pallas_call(
        _proj_kernel,
        grid=(ncols // bn,),
        in_specs=[
            pl.BlockSpec((rows, D_MODEL), lambda n: (0, 0)),
            pl.BlockSpec((None, D_MODEL, bn), lambda n: (layer, 0, off + n)),
            pl.BlockSpec((None, 1, bn), lambda n: (layer, 0, off + n)),
        ],
        out_specs=pl.BlockSpec((rows, bn), lambda n: (0, n)),
        out_shape=jax.ShapeDtypeStruct((rows, ncols), F32),
        compiler_params=pltpu.CompilerParams(
            dimension_semantics=("arbitrary",), vmem_limit_bytes=VMEM_LIMIT_BYTES),
        name="proj_sample",
    )(x, w, b)


def _mlstm_sample_kernel(p_ref, g_ref, c_ref, n_ref, m_ref, ng_ref, *rest, bb, seq):
    y_ref, c_out, n_out, m_out, qc_s = rest[-5:]
    rows = bb * seq
    t_idx = lax.broadcasted_iota(jnp.int32, (rows, 1), 0) % seq

    def down(x, d, fill=0.0):
        if d == 0:
            return x
        return jnp.where(t_idx >= d, pltpu.roll(x, d, 0), fill)

    def from_last(x):
        out = x
        for d in range(1, seq):
            out = jnp.where(t_idx == seq - 1 - d, pltpu.roll(x, rows - d, 0), out)
        return out

    it_all = g_ref[:, :V7X_LANES]
    lf_all = jax.nn.log_sigmoid(g_ref[:, V7X_LANES:])
    b_all = lf_all
    for d in range(1, seq):
        b_all = b_all + down(lf_all, d)
    g_all = it_all - b_all
    mx_all = g_all
    for d in range(1, seq):
        mx_all = jnp.maximum(mx_all, down(g_all, d, -jnp.inf))
    m0_all = m_ref[...]
    m_all = b_all + jnp.maximum(m0_all, mx_all)
    mlast_all = from_last(m_all)
    blast_all = from_last(b_all)
    inter_all = jnp.exp(b_all + m0_all - m_all)
    em_all = jnp.exp(-m_all)
    w_all = jnp.exp(blast_all + g_all - mlast_all)
    decay_all = jnp.exp(blast_all + m0_all - mlast_all)
    a_all = b_all - m_all
    m_out[...] = mlast_all

    def head_cols(blk, h):
        return slice(blk * D_MODEL + h * D_HEAD, blk * D_MODEL + (h + 1) * D_HEAD)

    for h in range(N_HEADS):
        hc = slice(h * D_HEAD, (h + 1) * D_HEAD)
        q = p_ref[:, head_cols(BLK_Q, h)]
        k = p_ref[:, head_cols(BLK_K, h)] * K_SCALE
        v = p_ref[:, head_cols(BLK_V, h)]
        lane = slice(h, h + 1)
        a_col, g_col = a_all[:, lane], g_all[:, lane]
        inter, em = inter_all[:, lane], em_all[:, lane]
        wcol, decay = w_all[:, lane], decay_all[:, lane]

        kw = k * wcol
        per_slab = V7X_SUBLANES // seq
        grp = lax.broadcasted_iota(jnp.int32, (V7X_SUBLANES, 1), 0) // seq
        for slab in range(rows // V7X_SUBLANES):
            s0 = slab * V7X_SUBLANES
            q8 = q[s0:s0 + V7X_SUBLANES, :].astype(BF16)
            kw8 = kw[s0:s0 + V7X_SUBLANES, :]
            v8 = v[s0:s0 + V7X_SUBLANES, :].astype(BF16)
            qc8 = jnp.zeros((V7X_SUBLANES, D_HEAD), F32)
            for e in range(per_slab):
                b = slab * per_slab + e
                c0 = c_ref[b, h]
                qc8 = jnp.where(grp == e, _dot(q8, c0.astype(BF16)), qc8)
                kw_e = jnp.where(grp == e, kw8, 0.0).astype(BF16)
                c_out[b, h] = decay[b * seq:b * seq + 1, :] * c0 + _dot_tn(kw_e, v8)
            qc_s[s0:s0 + V7X_SUBLANES, :] = qc8

        n0 = n_ref[:, hc]
        num = inter * qc_s[...]
        den = inter * jnp.sum(q * n0, axis=1, keepdims=True)
        for d in range(seq):
            s_d = jnp.sum(q * down(k, d), axis=1, keepdims=True) * jnp.exp(a_col + down(g_col, d))
            s_d = jnp.where(t_idx >= d, s_d, 0.0)
            num = num + s_d * down(v, d)
            den = den + s_d
        hh = num / jnp.maximum(jnp.abs(den), em)

        ksum = kw
        for d in range(1, seq):
            ksum = ksum + down(kw, d)
        n_out[:, hc] = decay * n0 + ksum

        hh = jax.nn.sigmoid(p_ref[:, head_cols(BLK_O, h)]) * hh
        mu = jnp.mean(hh, axis=-1, keepdims=True)
        hcn = hh - mu
        var = jnp.mean(hcn * hcn, axis=-1, keepdims=True)
        hn = hcn * lax.rsqrt(var + LN_EPS) * ng_ref[:, hc]
        y_ref[:, hc] = hn * _silu(p_ref[:, head_cols(BLK_ZA, h)])


def _mlstm_sample(proj_a, gates, c_all, c_new_all, n0_rows, m0_rows, lw, layer, seq):
    nb = c_all.shape[1]
    bb = BB_MLSTM_SAMPLE
    rows = bb * seq
    kern = functools.partial(_mlstm_sample_kernel, bb=bb, seq=seq)
    c_spec = pl.BlockSpec((None, bb, N_HEADS, D_HEAD, D_HEAD), lambda i: (layer, i, 0, 0, 0))
    in_specs = [
        pl.BlockSpec((rows, 5 * D_MODEL), lambda i: (i, 0)),
        pl.BlockSpec((rows, 2 * V7X_LANES), lambda i: (i, 0)),
        c_spec,
        pl.BlockSpec((rows, D_MODEL), lambda i: (i, 0)),
        pl.BlockSpec((rows, V7X_LANES), lambda i: (i, 0)),
        _at_layer(lw["norm_g"], layer),
    ]
    args = [proj_a, gates, c_all, n0_rows, m0_rows, lw["norm_g"]]
    aliases = {}
    if c_new_all is not None:
        in_specs.append(pl.BlockSpec(memory_space=pl.ANY))
        args.append(c_new_all)
        aliases = {len(args) - 1: 1}
    return pl.pallas_call(
        kern,
        grid=(nb // bb,),
        in_specs=in_specs,
        out_specs=[
            pl.BlockSpec((rows, D_MODEL), lambda i: (i, 0)),
            c_spec,
            pl.BlockSpec((rows, D_MODEL), lambda i: (i, 0)),
            pl.BlockSpec((rows, V7X_LANES), lambda i: (i, 0)),
        ],
        out_shape=[
            jax.ShapeDtypeStruct((nb * seq, D_MODEL), F32),
            jax.ShapeDtypeStruct(c_all.shape, F32),
            jax.ShapeDtypeStruct((nb * seq, D_MODEL), F32),
            jax.ShapeDtypeStruct((nb * seq, V7X_LANES), F32),
        ],
        scratch_shapes=[pltpu.VMEM((rows, D_HEAD), F32)],
        input_output_aliases=aliases,
        compiler_params=pltpu.CompilerParams(
            dimension_semantics=("arbitrary",), vmem_limit_bytes=VMEM_LIMIT_BYTES),
        name="mlstm_sample",
    )(*args)


def _rest_sample_kernel(x_ref, ya_ref, p_ref, buf_ref, h0_ref,
                        wsm_ref, bsm_ref,
                        lng1_ref, lnb1_ref, cw_ref, cb_ref, wax_ref, ba_ref, bx_ref, lam_ref,
                        wpa_ref, wpb_ref, wpc_ref, wo_ref, lng_ref, lnb_ref,
                        o_ref, vn_out, conv_out, h_out, *, seq, bbs, alpha):
    rows = seq * bbs

    def flat(blk):
        lo = (blk - BLK_U) * D_MODEL
        return p_ref[:, :, lo:lo + D_MODEL].reshape(rows, D_MODEL)

    vn = _layer_norm(flat(BLK_VB), lng1_ref[...], lnb1_ref[...])
    vn_out[...] = vn.reshape(seq, bbs, D_MODEL)
    mixed_t = []
    for t in range(seq):
        parts = []
        for g in range(N_GROUPS):
            cols = slice(g * D_GROUP, (g + 1) * D_GROUP)
            acc = jnp.full((bbs, D_GROUP), bsm_ref[g * seq + t], F32)
            for s in range(t + 1):
                acc = acc + wsm_ref[(g * seq + t) * seq + s] * vn[s * bbs:(s + 1) * bbs, cols]
            parts.append(acc)
        mixed_t.append(jnp.concatenate(parts, axis=1))
    mixed = jnp.concatenate(mixed_t, axis=0)
    yb = (flat(BLK_U) * mixed * _silu(flat(BLK_ZB))).astype(BF16)

    xc_raw = flat(BLK_XC)
    xp = [buf_ref[t] for t in range(CONV_W - 1)] + [xc_raw[t * bbs:(t + 1) * bbs] for t in range(seq)]
    for t in range(CONV_W - 1):
        conv_out[t] = xp[seq + t]
    xc_t = []
    for t in range(seq):
        acc = cb_ref[...] + cw_ref[0:1, :] * xp[t]
        for jj in range(1, CONV_W):
            acc = acc + cw_ref[jj:jj + 1, :] * xp[t + jj]
        xc_t.append(acc)
    xc = jnp.concatenate(xc_t, axis=0)
    a, mult, i = _lru_gates(xc, wax_ref, ba_ref[...], bx_ref[...], lam_ref[...])
    bterm = mult * i * xc
    h = h0_ref[...]
    hs = []
    for t in range(seq):
        h = a[t * bbs:(t + 1) * bbs] * h + bterm[t * bbs:(t + 1) * bbs]
        hs.append(h)
    h_out[...] = h
    yc = (jnp.concatenate(hs, axis=0) * _silu(flat(BLK_ZC))).astype(BF16)

    ya = ya_ref[...].reshape(rows, D_MODEL).astype(BF16)
    merged = (jax.nn.sigmoid(flat(BLK_GA)) * _dot(ya, wpa_ref[...])
              + jax.nn.sigmoid(flat(BLK_GB)) * _dot(yb, wpb_ref[...])
              + jax.nn.sigmoid(flat(BLK_GC)) * _dot(yc, wpc_ref[...]))
    out = _dot(merged.astype(BF16), wo_ref[...])
    x = x_ref[...].reshape(rows, D_MODEL)
    o_ref[...] = _layer_norm(alpha * x + out, lng_ref[...], lnb_ref[...]).reshape(seq, bbs, D_MODEL)


def _rest_sample(x_tm, ya_tm, p_tm, buf_tm, h0, lw, layer, alpha):
    seq, nb, _ = x_tm.shape
    bbs = min(BB_REST_SAMPLE, nb)
    kern = functools.partial(_rest_sample_kernel, seq=seq, bbs=bbs, alpha=alpha)

    def tm(n, lead=seq):
        return pl.BlockSpec((lead, bbs, n), lambda i: (0, i, 0))

    smem = pl.BlockSpec(memory_space=pltpu.SMEM)
    names = ["gmlp_ln_g", "gmlp_ln_b", "lru_conv_w", "lru_conv_b", "w_ax", "lru_ba", "lru_bx", "lru_lambda",
             "w_pa", "w_pb", "w_pc", "w_out", "ln_g", "ln_b"]
    return pl.pallas_call(
        kern,
        grid=(nb // bbs,),
        in_specs=[tm(D_MODEL), tm(D_MODEL), tm(p_tm.shape[-1]),
                  tm(D_MODEL, CONV_W - 1), pl.BlockSpec((bbs, D_MODEL), lambda i: (i, 0)),
                  smem, smem] + [_at_layer(lw[n], layer) for n in names],
        out_specs=[tm(D_MODEL), tm(D_MODEL), tm(D_MODEL, CONV_W - 1),
                   pl.BlockSpec((bbs, D_MODEL), lambda i: (i, 0))],
        out_shape=[
            jax.ShapeDtypeStruct((seq, nb, D_MODEL), F32),
            jax.ShapeDtypeStruct((seq, nb, D_MODEL), F32),
            jax.ShapeDtypeStruct((CONV_W - 1, nb, D_MODEL), F32),
            jax.ShapeDtypeStruct((nb, D_MODEL), F32),
        ],
        compiler_params=pltpu.CompilerParams(
            dimension_semantics=("arbitrary",), vmem_limit_bytes=VMEM_LIMIT_BYTES),
        name="rest_sample",
    )(x_tm, ya_tm, p_tm, buf_tm, h0, lw["ws_small"][layer], lw["bs_small"][layer],
      *[lw[n] for n in names])


def _pack_w_kernel(lo_ref, hi_ref, o_ref, gate_ref, kt_ref, *, shift):
    blk = pl.program_id(1)

    @pl.when(blk < BLK_U)
    def _():
        o_ref[...] = lo_ref[...].T.astype(BF16)

    @pl.when(blk == BLK_U - 1)
    def _():
        gate_ref[...] = hi_ref[...]

    @pl.when(blk == BLK_K)
    def _():
        kt_ref[...] = lo_ref[...].astype(BF16)

    @pl.when(blk >= BLK_U)
    def _():
        rows = jnp.concatenate([lo_ref[shift:, :], hi_ref[...]], axis=0)
        o_ref[...] = rows.T.astype(BF16)


def _pack_w(w_in):
    depth = w_in.shape[0]
    shift = 2 * N_HEADS
    assert shift == V7X_SUBLANES
    w_t = jnp.swapaxes(w_in, 1, 2)
    kern = functools.partial(_pack_w_kernel, shift=shift)
    return pl.pallas_call(
        kern,
        grid=(depth, N_BLOCKS),
        in_specs=[
            pl.BlockSpec((None, D_MODEL, D_MODEL), lambda l, k: (l, k, 0)),
            pl.BlockSpec((None, shift, D_MODEL), lambda l, k: (l, (k + 1) * (D_MODEL // shift), 0)),
        ],
        out_specs=[pl.BlockSpec((None, D_MODEL, D_MODEL), lambda l, k: (l, 0, k)),
                   pl.BlockSpec((None, shift, D_MODEL), lambda l, k: (l, 0, 0)),
                   pl.BlockSpec((None, D_MODEL, D_MODEL), lambda l, k: (l, 0, 0))],
        out_shape=[jax.ShapeDtypeStruct((depth, D_MODEL, N_BLOCKS * D_MODEL), BF16),
                   jax.ShapeDtypeStruct((depth, shift, D_MODEL), F32),
                   jax.ShapeDtypeStruct((depth, D_MODEL, D_MODEL), BF16)],
        compiler_params=pltpu.CompilerParams(
            dimension_semantics=("arbitrary", "arbitrary"), vmem_limit_bytes=VMEM_LIMIT_BYTES),
        name="pack_w",
    )(w_t, w_t)


def _pack_params(w_in, b_in, mlstm_norm_g, gmlp_ln_g, gmlp_ln_b, gmlp_ws, gmlp_bs, lru_conv_w, lru_conv_b,
                 lru_wa, lru_ba, lru_wx, lru_bx, lru_lambda, w_proj_a, w_proj_b, w_proj_c, w_out, ln_g, ln_b,
                 seq_s):
    depth = w_in.shape[0]
    d = D_MODEL
    o_i = 5 * d
    o_f = o_i + N_HEADS
    o_rest = o_f + N_HEADS

    def row(a):
        return a[:, None, :]

    def pad_lanes(a, n):
        return jnp.pad(a, [(0, 0)] * (a.ndim - 1) + [(0, n - a.shape[-1])])

    def drop_gate_cols(a):
        return jnp.concatenate([a[..., :o_i], a[..., o_rest:]], axis=-1)

    w_all, w_gate_t, w_kt = _pack_w(w_in)
    w_gate = jnp.swapaxes(w_gate_t, 1, 2)
    wi, wf = w_gate[..., :N_HEADS], w_gate[..., N_HEADS:2 * N_HEADS]
    bi, bf = b_in[..., o_i:o_f], b_in[..., o_f:o_rest]
    pad_rows = [(0, 0), (0, V7X_SUBLANES - N_HEADS), (0, 0)]
    lw = {
        "w_all": w_all,
        "b_all": row(drop_gate_cols(b_in)),
        "w_kt": w_kt,
        "b_kt": b_in[:, BLK_K * d:(BLK_K + 1) * d, None],
        "w_gc": jnp.concatenate([pad_lanes(wi, V7X_LANES), pad_lanes(wf, V7X_LANES)], axis=-1).astype(BF16),
        "b_gc": row(jnp.concatenate([pad_lanes(bi, V7X_LANES), pad_lanes(bf, V7X_LANES)], axis=-1)),
        "w_gr": jnp.concatenate([jnp.pad(jnp.swapaxes(wi, 1, 2), pad_rows),
                                 jnp.pad(jnp.swapaxes(wf, 1, 2), pad_rows)], axis=1).astype(BF16),
        "b_gr": jnp.concatenate([jnp.pad(bi[:, :, None], pad_rows),
                                 jnp.pad(bf[:, :, None], pad_rows)], axis=1),
        "norm_g": row(mlstm_norm_g),
        "gmlp_ln_g": row(gmlp_ln_g), "gmlp_ln_b": row(gmlp_ln_b),
        "gmlp_ws": gmlp_ws,
        "gmlp_bs_t": jnp.swapaxes(gmlp_bs, 1, 2),
        "ws_small": gmlp_ws[:, :, :seq_s, :seq_s].reshape(depth, -1),
        "bs_small": gmlp_bs[:, :, :seq_s].reshape(depth, -1),
        "lru_conv_w": lru_conv_w, "lru_conv_b": row(lru_conv_b),
        "w_ax": jnp.concatenate([lru_wa, lru_wx], axis=-1).astype(BF16),
        "lru_ba": row(lru_ba), "lru_bx": row(lru_bx), "lru_lambda": row(lru_lambda),
        "w_pa": w_proj_a.astype(BF16), "w_pb": w_proj_b.astype(BF16), "w_pc": w_proj_c.astype(BF16),
        "w_out": w_out.astype(BF16),
        "ln_g": row(ln_g), "ln_b": row(ln_b),
    }
    return lw


def _prompt_layer(x, lw, layer, alpha):
    merged, c, n, m = _mlstm_prompt(x, lw, layer)
    x_new, conv, h = _mix_prompt(x, merged, lw, layer, alpha)
    return x_new, c, n[:, :N_HEADS], m[:, 0, :N_HEADS], conv, h[:, 0]


def _sample_layer(x_bm, c_all, c_new_all, n0, m0, conv_buf, h0, lw, layer, alpha):
    nb, seq, d = x_bm.shape
    x_rows = x_bm.reshape(nb * seq, d)
    x_tm = jnp.swapaxes(x_bm, 0, 1)
    x_tm_rows = x_tm.reshape(seq * nb, d)
    n_rest = N_BLOCKS - BLK_U
    proj_a = _proj(x_rows, lw["w_all"], lw["b_all"], layer, 0, BLK_U * d, d)
    gates = _proj(x_rows, lw["w_gc"], lw["b_gc"], layer, 0, 2 * V7X_LANES, 2 * V7X_LANES)
    p_tm = _proj(x_tm_rows, lw["w_all"], lw["b_all"], layer, BLK_U * d, n_rest * d, d)
    p_tm = p_tm.reshape(seq, nb, n_rest * d)

    n0_rows = jnp.repeat(n0.reshape(nb, d), seq, axis=0)
    m0_rows = jnp.repeat(jnp.pad(m0, ((0, 0), (0, V7X_LANES - N_HEADS))), seq, axis=0)
    ya, c_new_all, n_rows, m_rows = _mlstm_sample(
        proj_a, gates, c_all, c_new_all, n0_rows, m0_rows, lw, layer, seq)
    n_new = n_rows.reshape(nb, seq, N_HEADS, D_HEAD)[:, seq - 1]
    m_new = m_rows.reshape(nb, seq, V7X_LANES)[:, seq - 1, :N_HEADS]

    ya_tm = jnp.swapaxes(ya.reshape(nb, seq, d), 0, 1)
    buf_tm = jnp.swapaxes(conv_buf, 0, 1)
    x_new_tm, vn_tm, conv_tm, h_new = _rest_sample(x_tm, ya_tm, p_tm, buf_tm, h0, lw, layer, alpha)
    return (jnp.swapaxes(x_new_tm, 0, 1), c_new_all, n_new, m_new, jnp.swapaxes(conv_tm, 0, 1), h_new,
            jnp.swapaxes(vn_tm, 0, 1))


def kernel(x_prompt, x_sample, state_mlstm_c, state_mlstm_n, state_mlstm_m, state_lru_conv, state_lru_h,
           w_in, b_in, mlstm_norm_g, gmlp_ln_g, gmlp_ln_b, gmlp_ws, gmlp_bs, lru_conv_w, lru_conv_b,
           lru_wa, lru_ba, lru_wx, lru_bx, lru_lambda, w_proj_a, w_proj_b, w_proj_c, w_out, ln_g, ln_b):
    depth = w_in.shape[0]
    alpha = float((2 * depth) ** 0.25)
    seq_s = x_sample.shape[1]
    assert V7X_SUBLANES % seq_s == 0 and x_prompt.shape[1] % TT_MLSTM == 0
    lw = _pack_params(w_in, b_in, mlstm_norm_g, gmlp_ln_g, gmlp_ln_b, gmlp_ws, gmlp_bs, lru_conv_w,
                      lru_conv_b, lru_wa, lru_ba, lru_wx, lru_bx, lru_lambda, w_proj_a, w_proj_b, w_proj_c,
                      w_out, ln_g, ln_b, seq_s)
    xp, xs = x_prompt, x_sample
    outs_p = [[] for _ in range(5)]
    outs_s = [[] for _ in range(5)]
    c_new_all = None
    for layer in range(depth):
        xp, *st = _prompt_layer(xp, lw, layer, alpha)
        for acc, val in zip(outs_p, st):
            acc.append(val)
        xs, c_new_all, *st = _sample_layer(
            xs, state_mlstm_c, c_new_all, state_mlstm_n[layer], state_mlstm_m[layer],
            state_lru_conv[layer], state_lru_h[layer], lw, layer, alpha)
        for acc, val in zip(outs_s, st):
            acc.append(val)
    stacked_p = [jnp.stack(v) for v in outs_p]
    stacked_s = [jnp.stack(v) for v in outs_s]
    return (xp, xs, *stacked_p, c_new_all, *stacked_s)
```

```python
import functools

import jax
import jax.numpy as jnp
from jax import lax
from jax.experimental import pallas as pl
from jax.experimental.pallas import tpu as pltpu

D_MODEL = 1024
N_HEADS = 4
D_HEAD = D_MODEL // N_HEADS
CHUNK = 128
N_GROUPS = 4
D_GROUP = D_MODEL // N_GROUPS
LRU_BLOCKS = 8
LRU_BLK = D_MODEL // LRU_BLOCKS
CONV_W = 4
LRU_C = 8.0
LN_EPS = 1e-5
K_SCALE = D_HEAD ** -0.5
LOG2_E = 1.4426950408889634
V7X_LANES = 128
V7X_SUBLANES = 8
VMEM_LIMIT_BYTES = 56 * 1024 * 1024

TT_MLSTM = 512
TT_MIX = 512
SB_MIX = 512
BB_MLSTM_SAMPLE = 8
BB_REST_SAMPLE = 64

BLK_Q, BLK_K, BLK_V, BLK_O, BLK_ZA = 0, 1, 2, 3, 4
BLK_U, BLK_VB, BLK_ZB = 5, 6, 7
BLK_XC, BLK_ZC = 8, 9
BLK_GA, BLK_GB, BLK_GC = 10, 11, 12
N_BLOCKS = 13

BF16 = jnp.bfloat16
U32 = jnp.uint32
F32 = jnp.float32

_NT = (((1,), (1,)), ((), ()))
_TN = (((0,), (0,)), ((), ()))


def _dot(a, b):
    return jnp.dot(a, b, preferred_element_type=F32)


def _dot_nt(a, b):
    return lax.dot_general(a, b, _NT, preferred_element_type=F32)


def _dot_tn(a, b):
    return lax.dot_general(a, b, _TN, preferred_element_type=F32)


def _unpack(ref, rows=slice(None), cols=slice(None)):
    return pltpu.bitcast(ref[rows, cols], BF16)


def _layer_norm(x, g, b):
    mu = jnp.mean(x, axis=-1, keepdims=True)
    xc = x - mu
    var = jnp.mean(xc * xc, axis=-1, keepdims=True)
    return xc * lax.rsqrt(var + LN_EPS) * g + b


def _silu(x):
    return x * jax.nn.sigmoid(x)


def _scan_axis(x, axis, op, fill):
    n = x.shape[axis]
    idx = lax.broadcasted_iota(jnp.int32, x.shape, axis)
    s = 1
    while s < n:
        x = op(x, jnp.where(idx >= s, pltpu.roll(x, s, axis), fill))
        s *= 2
    return x


def _at_layer(arr, layer):
    shape = arr.shape[1:]
    nd = len(shape)
    return pl.BlockSpec((None,) + shape, lambda *_: (layer,) + (0,) * nd, pipeline_mode=pl.Buffered(1))


def _w_block(layer, blk):
    return pl.BlockSpec((None, D_MODEL // 2, D_MODEL), lambda *_: (layer, 0, blk), pipeline_mode=pl.Buffered(1))


def _b_block(layer, blk):
    return pl.BlockSpec((None, 1, D_MODEL), lambda *_: (layer, 0, blk), pipeline_mode=pl.Buffered(1))


def _mlstm_prompt_kernel(x_ref, wq_ref, wv_ref, wo_ref, wz_ref, wg_ref, bq_ref, bv_ref, bo_ref, bz_ref, bg_ref,
                         wkt_ref, bkt_ref, wgc_ref, bgc_ref, wgr_ref, bgr_ref, ng_ref, wp_ref,
                         o_ref, c_out, n_out, m_out,
                         c_s, n_s, mc_s, mr_s, q_s, kt_s, v_s, o_s, h_s,
                         acol_s, inter_s, em_s, decay_s, grow_s, wrow_s, *, tt):
    j = pl.program_id(1)
    nch = tt // CHUNK

    @pl.when(j == 0)
    def _():
        c_s[...] = jnp.zeros_like(c_s)
        n_s[...] = jnp.zeros_like(n_s)
        mc_s[...] = jnp.zeros_like(mc_s)
        mr_s[...] = jnp.zeros_like(mr_s)

    xb = x_ref[...].astype(BF16)

    def col(w_ref, b_ref):
        return _dot(xb, _unpack(w_ref)) + b_ref[...]

    def project_q():
        q_s[...] = col(wq_ref, bq_ref).astype(BF16)

    def project_k():
        kt_s[...] = (_dot_nt(wkt_ref[...], xb) + bkt_ref[...]) * K_SCALE

    def project_v():
        v_s[...] = col(wv_ref, bv_ref).astype(BF16)

    def project_o():
        o_s[...] = jax.nn.sigmoid(col(wo_ref, bo_ref))

    projections = [project_q, project_k, project_v, project_o]

    gcol = _dot(xb, _unpack(wgc_ref)) + bgc_ref[...]
    it_c = gcol[:, :V7X_LANES]
    lf_c = jax.nn.log_sigmoid(gcol[:, V7X_LANES:])
    grow = _dot_nt(wgr_ref[...], xb) + bgr_ref[...]
    it_r = grow[:V7X_SUBLANES]
    lf_r = jax.nn.log_sigmoid(grow[V7X_SUBLANES:])

    m0c = mc_s[0:1, :]
    m0r = mr_s[...]
    for c in range(nch):
        sl = slice(c * CHUNK, (c + 1) * CHUNK)
        bcol = _scan_axis(lf_c[sl], 0, jnp.add, 0.0)
        gcl = it_c[sl] - bcol
        mcol = bcol + jnp.maximum(m0c, _scan_axis(gcl, 0, jnp.maximum, -jnp.inf))
        blast = bcol[CHUNK - 1:CHUNK]
        mlast = mcol[CHUNK - 1:CHUNK]
        acol = bcol - mcol
        inter = jnp.exp(bcol + m0c - mcol)
        em = jnp.exp(-mcol)
        decay_s[c:c + 1, :] = jnp.exp(blast + m0c - mlast)
        for h in range(N_HEADS):
            acol_s[h, sl, :] = jnp.broadcast_to(acol[:, h:h + 1], (CHUNK, V7X_LANES))
            inter_s[h, sl, :] = jnp.broadcast_to(inter[:, h:h + 1], (CHUNK, V7X_LANES))
            em_s[h, sl, :] = jnp.broadcast_to(em[:, h:h + 1], (CHUNK, V7X_LANES))
        brow = _scan_axis(lf_r[:, sl], 1, jnp.add, 0.0)
        grw = it_r[:, sl] - brow
        mrow = brow + jnp.maximum(m0r, _scan_axis(grw, 1, jnp.maximum, -jnp.inf))
        mlast_r = mrow[:, CHUNK - 1:CHUNK]
        grow_s[:, sl] = grw
        wrow_s[:, sl] = jnp.exp(brow[:, CHUNK - 1:CHUNK] + grw - mlast_r)
        m0c = mlast
        m0r = jnp.broadcast_to(mlast_r, m0r.shape)
        if c < len(projections):
            projections[c]()
    for project in projections[nch:]:
        project()
    mc_s[...] = jnp.broadcast_to(m0c, mc_s.shape)
    mr_s[...] = m0r

    tri = (lax.broadcasted_iota(jnp.int32, (CHUNK, CHUNK), 0)
           >= lax.broadcasted_iota(jnp.int32, (CHUNK, CHUNK), 1))
    ones_rows = jnp.ones((V7X_SUBLANES, CHUNK), BF16)

    for c in range(nch):
        rows = slice(c * CHUNK, (c + 1) * CHUNK)
        decay_c = decay_s[c:c + 1, :]
        for h in range(N_HEADS):
            hc = slice(h * D_HEAD, (h + 1) * D_HEAD)
            qc = q_s[rows, hc]
            kt = kt_s[hc, rows]
            vc = v_s[rows, hc]
            inter = inter_s[h, rows, :]
            decay = decay_c[:, h:h + 1]

            dmat = jnp.exp(jnp.where(tri, acol_s[h, rows, :] + grow_s[h:h + 1, rows], -jnp.inf))
            s = _dot(qc, kt.astype(BF16)) * dmat
            c0 = c_s[h]
            n0 = n_s[h:h + 1, :]
            num = (jnp.concatenate([inter, inter], axis=1) * _dot(qc, c0.astype(BF16))
                   + _dot(s.astype(BF16), vc))
            qn = qc.astype(F32) * n0
            den = jnp.sum(inter * (qn[:, :V7X_LANES] + qn[:, V7X_LANES:]) + s, axis=1, keepdims=True)
            rden = 1.0 / jnp.maximum(jnp.abs(den), em_s[h, rows, :])
            h_s[rows, hc] = num * jnp.concatenate([rden, rden], axis=1)

            kwt = (kt * wrow_s[h:h + 1, rows]).astype(BF16)
            c_s[h] = decay * c0 + _dot(kwt, vc)
            n_s[h:h + 1, :] = decay * n0 + _dot_nt(ones_rows, kwt)[0:1, :]

    hh = o_s[...] * h_s[...]
    parts = []
    for h in range(N_HEADS):
        hc = slice(h * D_HEAD, (h + 1) * D_HEAD)
        hd = hh[:, hc]
        mu = jnp.mean(hd, axis=-1, keepdims=True)
        hd = hd - mu
        var = jnp.mean(hd * hd, axis=-1, keepdims=True)
        parts.append(hd * lax.rsqrt(var + LN_EPS) * ng_ref[:, hc])
    y = (jnp.concatenate(parts, axis=1) * _silu(col(wz_ref, bz_ref))).astype(BF16)
    o_ref[...] = jax.nn.sigmoid(col(wg_ref, bg_ref)) * _dot(y, _unpack(wp_ref))

    @pl.when(j == pl.num_programs(1) - 1)
    def _():
        c_out[...] = c_s[...]
        n_out[...] = n_s[...]
        m_out[...] = mc_s[...]


def _mlstm_prompt(x, lw, layer):
    bsz, t, _ = x.shape
    tt = TT_MLSTM
    kern = functools.partial(_mlstm_prompt_kernel, tt=tt)
    tile = pl.BlockSpec((None, tt, D_MODEL), lambda b, j: (b, j, 0))
    blocks = [BLK_Q, BLK_V, BLK_O, BLK_ZA, BLK_GA]
    names = ["w_kt", "b_kt", "w_gc", "b_gc", "w_gr", "b_gr", "norm_g", "w_pa"]
    head_lanes = (N_HEADS, tt, V7X_LANES)
    return pl.pallas_call(
        kern,
        grid=(bsz, t // tt),
        in_specs=([tile] + [_w_block(layer, blk) for blk in blocks] + [_b_block(layer, blk) for blk in blocks]
                  + [_at_layer(lw[n], layer) for n in names]),
        out_specs=[
            tile,
            pl.BlockSpec((None, N_HEADS, D_HEAD, D_HEAD), lambda b, j: (b, 0, 0, 0)),
            pl.BlockSpec((None, V7X_SUBLANES, D_HEAD), lambda b, j: (b, 0, 0)),
            pl.BlockSpec((None, V7X_SUBLANES, V7X_LANES), lambda b, j: (b, 0, 0)),
        ],
        out_shape=[
            jax.ShapeDtypeStruct((bsz, t, D_MODEL), F32),
            jax.ShapeDtypeStruct((bsz, N_HEADS, D_HEAD, D_HEAD), F32),
            jax.ShapeDtypeStruct((bsz, V7X_SUBLANES, D_HEAD), F32),
            jax.ShapeDtypeStruct((bsz, V7X_SUBLANES, V7X_LANES), F32),
        ],
        scratch_shapes=[
            pltpu.VMEM((N_HEADS, D_HEAD, D_HEAD), F32),
            pltpu.VMEM((V7X_SUBLANES, D_HEAD), F32),
            pltpu.VMEM((V7X_SUBLANES, V7X_LANES), F32),
            pltpu.VMEM((V7X_SUBLANES, V7X_LANES), F32),
            pltpu.VMEM((tt, D_MODEL), BF16),
            pltpu.VMEM((D_MODEL, tt), F32),
            pltpu.VMEM((tt, D_MODEL), BF16),
            pltpu.VMEM((tt, D_MODEL), F32),
            pltpu.VMEM((tt, D_MODEL), F32),
            pltpu.VMEM(head_lanes, F32),
            pltpu.VMEM(head_lanes, F32),
            pltpu.VMEM(head_lanes, F32),
            pltpu.VMEM((V7X_SUBLANES, V7X_LANES), F32),
            pltpu.VMEM((V7X_SUBLANES, tt), F32),
            pltpu.VMEM((V7X_SUBLANES, tt), F32),
        ],
        compiler_params=pltpu.CompilerParams(
            dimension_semantics=("arbitrary", "arbitrary"), vmem_limit_bytes=VMEM_LIMIT_BYTES),
        name="mlstm_prompt",
    )(x, *([lw["w_all"]] * len(blocks)), *([lw["b_all"]] * len(blocks)), *[lw[n] for n in names])


def _lru_gates(xc, wax_ref, ba, bx, lam):
    xcb = xc.astype(BF16)
    half = LRU_BLK // 2
    pre = [_dot(xcb[:, n * LRU_BLK:(n + 1) * LRU_BLK], _unpack(wax_ref, slice(n * half, (n + 1) * half)))
           for n in range(LRU_BLOCKS)]
    r = jax.nn.sigmoid(jnp.concatenate([p[:, :LRU_BLK] for p in pre], axis=1) + ba)
    i = jax.nn.sigmoid(jnp.concatenate([p[:, LRU_BLK:] for p in pre], axis=1) + bx)
    a = jnp.exp2(r * (LRU_C * LOG2_E * jax.nn.log_sigmoid(lam)))
    v = 1.0 - a * a
    mult = jnp.where(v > 0.0, v * lax.rsqrt(v), 0.0)
    return a, mult, i


def _mix_kernel(x_ref, ma_ref,
                wu_ref, wvb_ref, wzb_ref, wgb_ref, wxc_ref, wzc_ref, wgc_ref,
                bu_ref, bvb_ref, bzb_ref, bgb_ref, bxc_ref, bzc_ref, bgc_ref,
                lng1_ref, lnb1_ref, ws_ref, bs_ref, wpb_ref,
                cw_ref, cb_ref, wax_ref, ba_ref, bx_ref, lam_ref, wpc_ref,
                wo_ref, lng_ref, lnb_ref,
                o_ref, conv_out, h_out,
                ucar_s, vn_s, h_s, hcar_s, *, tt, sb, alpha):
    j = pl.program_id(1)
    nch = tt // CHUNK
    nsb = tt // sb

    @pl.when(j == 0)
    def _():
        ucar_s[...] = jnp.broadcast_to(cb_ref[...], ucar_s.shape)
        hcar_s[...] = jnp.zeros_like(hcar_s)

    x = x_ref[...]
    xb = x.astype(BF16)
    state = {}

    def col(w_ref, b_ref, cols=slice(None), rows=slice(None)):
        return _dot(xb[rows], _unpack(w_ref, cols=cols)) + b_ref[:, cols]

    ngs = sb // V7X_SUBLANES
    sub = lax.broadcasted_iota(jnp.int32, (ngs, V7X_SUBLANES, D_MODEL), 1)
    conv_tail = [ucar_s[jj] for jj in range(CONV_W - 1)]

    def lru_input(k):
        state["x_c", k] = col(wxc_ref, bxc_ref, rows=slice(k * sb, (k + 1) * sb))

    def lru_block(k, hprev):
        r0 = k * sb
        x3 = state["x_c", k].reshape(ngs, V7X_SUBLANES, D_MODEL)
        u = cb_ref[...] + cw_ref[0:1, :] * x3
        for jj in range(1, CONV_W):
            rot = pltpu.roll(u, 1, 1)
            prev = jnp.concatenate([conv_tail[jj - 1][None], rot[:-1]], axis=0)
            conv_tail[jj - 1] = rot[ngs - 1]
            u = jnp.where(sub == 0, prev, rot) + cw_ref[jj:jj + 1, :] * x3
        xc = u.reshape(sb, D_MODEL)
        a, mult, i = _lru_gates(xc, wax_ref, ba_ref[...], bx_ref[...], lam_ref[...])
        if k == 0:
            row = lax.broadcasted_iota(jnp.int32, (V7X_SUBLANES, 1), 0)
            top = jnp.where(jnp.logical_and(row == 0, j == 0), 1.0, mult[:V7X_SUBLANES])
            mult = jnp.concatenate([top, mult[V7X_SUBLANES:]], axis=0)
        bterm = mult * i * xc
        a3 = a.reshape(ngs, V7X_SUBLANES, D_MODEL)
        b3 = bterm.reshape(ngs, V7X_SUBLANES, D_MODEL)
        s = 1
        while s < V7X_SUBLANES:
            keep = sub >= s
            a_sh = jnp.where(keep, pltpu.roll(a3, s, 1), 1.0)
            b_sh = jnp.where(keep, pltpu.roll(b3, s, 1), 0.0)
            b3 = a3 * b_sh + b3
            a3 = a3 * a_sh
            s *= 2
        for g in range(ngs):
            hg = a3[g] * hprev + b3[g]
            h_s[r0 + g * V7X_SUBLANES:r0 + (g + 1) * V7X_SUBLANES, :] = hg
            hprev = jnp.broadcast_to(hg[V7X_SUBLANES - 1:V7X_SUBLANES, :], hg.shape)
        return hprev

    tri = (lax.broadcasted_iota(jnp.int32, (CHUNK, CHUNK), 0)
           >= lax.broadcasted_iota(jnp.int32, (CHUNK, CHUNK), 1))

    def gmlp_norm():
        vn_s[...] = _layer_norm(col(wvb_ref, bvb_ref), lng1_ref[...], lnb1_ref[...])

    def gmlp_group(g):
        wm = jnp.where(tri, ws_ref[g], 0.0).astype(BF16)
        bias = bs_ref[:, g:g + 1]
        cols = slice(g * D_GROUP, (g + 1) * D_GROUP)
        mixed = jnp.concatenate(
            [_dot(wm, vn_s[c * CHUNK:(c + 1) * CHUNK, cols].astype(BF16)) + bias for c in range(nch)],
            axis=0)
        y = (col(wu_ref, bu_ref, cols) * mixed * _silu(col(wzb_ref, bzb_ref, cols))).astype(BF16)
        contrib = _dot(y, _unpack(wpb_ref, slice(g * D_GROUP // 2, (g + 1) * D_GROUP // 2)))
        state["acc"] = contrib if g == 0 else state["acc"] + contrib

    def lru_silu_z():
        state["sz_c"] = _silu(col(wzc_ref, bzc_ref))

    def gmlp_gate():
        state["merged"] = ma_ref[...] + jax.nn.sigmoid(col(wgb_ref, bgb_ref)) * state["acc"]

    def lru_gate():
        state["g_c"] = jax.nn.sigmoid(col(wgc_ref, bgc_ref))

    steps = ([gmlp_norm, lru_silu_z] + [functools.partial(gmlp_group, g) for g in range(N_GROUPS)]
             + [gmlp_gate, lru_gate])
    bounds = [round(k * len(steps) / nsb) for k in range(nsb + 1)]
    hprev = hcar_s[...]
    lru_input(0)
    for k in range(nsb):
        if k + 1 < nsb:
            lru_input(k + 1)
        for step in steps[bounds[k]:bounds[k + 1]]:
            step()
        hprev = lru_block(k, hprev)
    hcar_s[...] = hprev
    for jj in range(CONV_W - 1):
        ucar_s[jj] = conv_tail[jj]

    @pl.when(j == pl.num_programs(1) - 1)
    def _():
        conv_out[...] = state["x_c", nsb - 1][sb - (CONV_W - 1):, :]
        h_out[...] = hprev

    y_c = (h_s[...] * state["sz_c"]).astype(BF16)
    merged = state["merged"] + state["g_c"] * _dot(y_c, _unpack(wpc_ref))
    out = _dot(merged.astype(BF16), _unpack(wo_ref))
    o_ref[...] = _layer_norm(alpha * x + out, lng_ref[...], lnb_ref[...])


def _mix_prompt(x, merged, lw, layer, alpha):
    bsz, t, _ = x.shape
    tt = TT_MIX
    kern = functools.partial(_mix_kernel, tt=tt, sb=SB_MIX, alpha=alpha)
    tile = pl.BlockSpec((None, tt, D_MODEL), lambda b, j: (b, j, 0))
    blocks = [BLK_U, BLK_VB, BLK_ZB, BLK_GB, BLK_XC, BLK_ZC, BLK_GC]
    names = ["gmlp_ln_g", "gmlp_ln_b", "gmlp_ws", "gmlp_bs_t", "w_pb",
             "lru_conv_w", "lru_conv_b", "w_ax", "lru_ba", "lru_bx", "lru_lambda", "w_pc",
             "w_out", "ln_g", "ln_b"]
    return pl.pallas_call(
        kern,
        grid=(bsz, t // tt),
        in_specs=([tile, tile] + [_w_block(layer, blk) for blk in blocks]
                  + [_b_block(layer, blk) for blk in blocks] + [_at_layer(lw[n], layer) for n in names]),
        out_specs=[
            tile,
            pl.BlockSpec((None, CONV_W - 1, D_MODEL), lambda b, j: (b, 0, 0)),
            pl.BlockSpec((None, V7X_SUBLANES, D_MODEL), lambda b, j: (b, 0, 0)),
        ],
        out_shape=[
            jax.ShapeDtypeStruct((bsz, t, D_MODEL), F32),
            jax.ShapeDtypeStruct((bsz, CONV_W - 1, D_MODEL), F32),
            jax.ShapeDtypeStruct((bsz, V7X_SUBLANES, D_MODEL), F32),
        ],
        scratch_shapes=[
            pltpu.VMEM((CONV_W - 1, V7X_SUBLANES, D_MODEL), F32),
            pltpu.VMEM((tt, D_MODEL), F32),
            pltpu.VMEM((tt, D_MODEL), F32),
            pltpu.VMEM((V7X_SUBLANES, D_MODEL), F32),
        ],
        compiler_params=pltpu.CompilerParams(
            dimension_semantics=("arbitrary", "arbitrary"), vmem_limit_bytes=VMEM_LIMIT_BYTES),
        name="mix_prompt",
    )(x, merged, *([lw["w_all"]] * len(blocks)), *([lw["b_all"]] * len(blocks)), *[lw[n] for n in names])


def _proj_kernel(x_ref, w_ref, b_ref, o_ref):
    o_ref[...] = _dot(x_ref[...].astype(BF16), _unpack(w_ref)) + b_ref[...]


def _proj(x, w, b, layer, col0, ncols, bn):
    rows = x.shape[0]
    off = col0 // bn
    return pl.pallas_call(
        _proj_kernel,
        grid=(ncols // bn,),
        in_specs=[
            pl.BlockSpec((rows, D_MODEL), lambda n: (0, 0)),
            pl.BlockSpec((None, D_MODEL // 2, bn), lambda n: (layer, 0, off + n)),
            pl.BlockSpec((None, 1, bn), lambda n: (layer, 0, off + n)),
        ],
        out_specs=pl.BlockSpec((rows, bn), lambda n: (0, n)),
        out_shape=jax.ShapeDtypeStruct((rows, ncols), F32),
        compiler_params=pltpu.CompilerParams(
            dimension_semantics=("arbitrary",), vmem_limit_bytes=VMEM_LIMIT_BYTES),
        name="proj_sample",
    )(x, w, b)


def _mlstm_sample_kernel(p_ref, g_ref, c_ref, n_ref, m_ref, ng_ref, *rest, bb, seq):
    y_ref, c_out, n_out, m_out, qc_s = rest[-5:]
    rows = bb * seq
    t_idx = lax.broadcasted_iota(jnp.int32, (rows, 1), 0) % seq

    def down(x, d, fill=0.0):
        if d == 0:
            return x
        return jnp.where(t_idx >= d, pltpu.roll(x, d, 0), fill)

    def from_last(x):
        out = x
        for d in range(1, seq):
            out = jnp.where(t_idx == seq - 1 - d, pltpu.roll(x, rows - d, 0), out)
        return out

    it_all = g_ref[:, :V7X_LANES]
    lf_all = jax.nn.log_sigmoid(g_ref[:, V7X_LANES:])
    b_all = lf_all
    for d in range(1, seq):
        b_all = b_all + down(lf_all, d)
    g_all = it_all - b_all
    mx_all = g_all
    for d in range(1, seq):
        mx_all = jnp.maximum(mx_all, down(g_all, d, -jnp.inf))
    m0_all = m_ref[...]
    m_all = b_all + jnp.maximum(m0_all, mx_all)
    mlast_all = from_last(m_all)
    blast_all = from_last(b_all)
    inter_all = jnp.exp(b_all + m0_all - m_all)
    em_all = jnp.exp(-m_all)
    w_all = jnp.exp(blast_all + g_all - mlast_all)
    decay_all = jnp.exp(blast_all + m0_all - mlast_all)
    a_all = b_all - m_all
    m_out[...] = mlast_all

    def head_cols(blk, h):
        return slice(blk * D_MODEL + h * D_HEAD, blk * D_MODEL + (h + 1) * D_HEAD)

    for h in range(N_HEADS):
        hc = slice(h * D_HEAD, (h + 1) * D_HEAD)
        q = p_ref[:, head_cols(BLK_Q, h)]
        k = p_ref[:, head_cols(BLK_K, h)] * K_SCALE
        v = p_ref[:, head_cols(BLK_V, h)]
        lane = slice(h, h + 1)
        a_col, g_col = a_all[:, lane], g_all[:, lane]
        inter, em = inter_all[:, lane], em_all[:, lane]
        wcol, decay = w_all[:, lane], decay_all[:, lane]

        kw = k * wcol
        per_slab = V7X_SUBLANES // seq
        grp = lax.broadcasted_iota(jnp.int32, (V7X_SUBLANES, 1), 0) // seq
        for slab in range(rows // V7X_SUBLANES):
            s0 = slab * V7X_SUBLANES
            q8 = q[s0:s0 + V7X_SUBLANES, :].astype(BF16)
            kw8 = kw[s0:s0 + V7X_SUBLANES, :]
            v8 = v[s0:s0 + V7X_SUBLANES, :].astype(BF16)
            qc8 = jnp.zeros((V7X_SUBLANES, D_HEAD), F32)
            for e in range(per_slab):
                b = slab * per_slab + e
                c0 = c_ref[b, h]
                qc8 = jnp.where(grp == e, _dot(q8, c0.astype(BF16)), qc8)
                kw_e = jnp.where(grp == e, kw8, 0.0).astype(BF16)
                c_out[b, h] = decay[b * seq:b * seq + 1, :] * c0 + _dot_tn(kw_e, v8)
            qc_s[s0:s0 + V7X_SUBLANES, :] = qc8

        n0 = n_ref[:, hc]
        num = inter * qc_s[...]
        den = inter * jnp.sum(q * n0, axis=1, keepdims=True)
        for d in range(seq):
            s_d = jnp.sum(q * down(k, d), axis=1, keepdims=True) * jnp.exp(a_col + down(g_col, d))
            s_d = jnp.where(t_idx >= d, s_d, 0.0)
            num = num + s_d * down(v, d)
            den = den + s_d
        hh = num / jnp.maximum(jnp.abs(den), em)

        ksum = kw
        for d in range(1, seq):
            ksum = ksum + down(kw, d)
        n_out[:, hc] = decay * n0 + ksum

        hh = jax.nn.sigmoid(p_ref[:, head_cols(BLK_O, h)]) * hh
        mu = jnp.mean(hh, axis=-1, keepdims=True)
        hcn = hh - mu
        var = jnp.mean(hcn * hcn, axis=-1, keepdims=True)
        hn = hcn * lax.rsqrt(var + LN_EPS) * ng_ref[:, hc]
        y_ref[:, hc] = hn * _silu(p_ref[:, head_cols(BLK_ZA, h)])


def _mlstm_sample(proj_a, gates, c_all, c_new_all, n0_rows, m0_rows, lw, layer, seq):
    nb = c_all.shape[1]
    bb = BB_MLSTM_SAMPLE
    rows = bb * seq
    kern = functools.partial(_mlstm_sample_kernel, bb=bb, seq=seq)
    c_spec = pl.BlockSpec((None, bb, N_HEADS, D_HEAD, D_HEAD), lambda i: (layer, i, 0, 0, 0))
    in_specs = [
        pl.BlockSpec((rows, 5 * D_MODEL), lambda i: (i, 0)),
        pl.BlockSpec((rows, 2 * V7X_LANES), lambda i: (i, 0)),
        c_spec,
        pl.BlockSpec((rows, D_MODEL), lambda i: (i, 0)),
        pl.BlockSpec((rows, V7X_LANES), lambda i: (i, 0)),
        _at_layer(lw["norm_g"], layer),
    ]
    args = [proj_a, gates, c_all, n0_rows, m0_rows, lw["norm_g"]]
    aliases = {}
    if c_new_all is not None:
        in_specs.append(pl.BlockSpec(memory_space=pl.ANY))
        args.append(c_new_all)
        aliases = {len(args) - 1: 1}
    return pl.pallas_call(
        kern,
        grid=(nb // bb,),
        in_specs=in_specs,
        out_specs=[
            pl.BlockSpec((rows, D_MODEL), lambda i: (i, 0)),
            c_spec,
            pl.BlockSpec((rows, D_MODEL), lambda i: (i, 0)),
            pl.BlockSpec((rows, V7X_LANES), lambda i: (i, 0)),
        ],
        out_shape=[
            jax.ShapeDtypeStruct((nb * seq, D_MODEL), F32),
            jax.ShapeDtypeStruct(c_all.shape, F32),
            jax.ShapeDtypeStruct((nb * seq, D_MODEL), F32),
            jax.ShapeDtypeStruct((nb * seq, V7X_LANES), F32),
        ],
        scratch_shapes=[pltpu.VMEM((rows, D_HEAD), F32)],
        input_output_aliases=aliases,
        compiler_params=pltpu.CompilerParams(
            dimension_semantics=("arbitrary",), vmem_limit_bytes=VMEM_LIMIT_BYTES),
        name="mlstm_sample",
    )(*args)


def _rest_sample_kernel(x_ref, ya_ref, p_ref, buf_ref, h0_ref,
                        wsm_ref, bsm_ref,
                        lng1_ref, lnb1_ref, cw_ref, cb_ref, wax_ref, ba_ref, bx_ref, lam_ref,
                        wpa_ref, wpb_ref, wpc_ref, wo_ref, lng_ref, lnb_ref,
                        o_ref, vn_out, conv_out, h_out, *, seq, bbs, alpha):
    rows = seq * bbs

    def flat(blk):
        lo = (blk - BLK_U) * D_MODEL
        return p_ref[:, :, lo:lo + D_MODEL].reshape(rows, D_MODEL)

    vn = _layer_norm(flat(BLK_VB), lng1_ref[...], lnb1_ref[...])
    vn_out[...] = vn.reshape(seq, bbs, D_MODEL)
    mixed_t = []
    for t in range(seq):
        parts = []
        for g in range(N_GROUPS):
            cols = slice(g * D_GROUP, (g + 1) * D_GROUP)
            acc = jnp.full((bbs, D_GROUP), bsm_ref[g * seq + t], F32)
            for s in range(t + 1):
                acc = acc + wsm_ref[(g * seq + t) * seq + s] * vn[s * bbs:(s + 1) * bbs, cols]
            parts.append(acc)
        mixed_t.append(jnp.concatenate(parts, axis=1))
    mixed = jnp.concatenate(mixed_t, axis=0)
    yb = (flat(BLK_U) * mixed * _silu(flat(BLK_ZB))).astype(BF16)

    xc_raw = flat(BLK_XC)
    xp = [buf_ref[t] for t in range(CONV_W - 1)] + [xc_raw[t * bbs:(t + 1) * bbs] for t in range(seq)]
    for t in range(CONV_W - 1):
        conv_out[t] = xp[seq + t]
    xc_t = []
    for t in range(seq):
        acc = cb_ref[...] + cw_ref[0:1, :] * xp[t]
        for jj in range(1, CONV_W):
            acc = acc + cw_ref[jj:jj + 1, :] * xp[t + jj]
        xc_t.append(acc)
    xc = jnp.concatenate(xc_t, axis=0)
    a, mult, i = _lru_gates(xc, wax_ref, ba_ref[...], bx_ref[...], lam_ref[...])
    bterm = mult * i * xc
    h = h0_ref[...]
    hs = []
    for t in range(seq):
        h = a[t * bbs:(t + 1) * bbs] * h + bterm[t * bbs:(t + 1) * bbs]
        hs.append(h)
    h_out[...] = h
    yc = (jnp.concatenate(hs, axis=0) * _silu(flat(BLK_ZC))).astype(BF16)

    ya = ya_ref[...].reshape(rows, D_MODEL).astype(BF16)
    merged = (jax.nn.sigmoid(flat(BLK_GA)) * _dot(ya, _unpack(wpa_ref))
              + jax.nn.sigmoid(flat(BLK_GB)) * _dot(yb, _unpack(wpb_ref))
              + jax.nn.sigmoid(flat(BLK_GC)) * _dot(yc, _unpack(wpc_ref)))
    out = _dot(merged.astype(BF16), _unpack(wo_ref))
    x = x_ref[...].reshape(rows, D_MODEL)
    o_ref[...] = _layer_norm(alpha * x + out, lng_ref[...], lnb_ref[...]).reshape(seq, bbs, D_MODEL)


def _rest_sample(x_tm, ya_tm, p_tm, buf_tm, h0, lw, layer, alpha):
    seq, nb, _ = x_tm.shape
    bbs = min(BB_REST_SAMPLE, nb)
    kern = functools.partial(_rest_sample_kernel, seq=seq, bbs=bbs, alpha=alpha)

    def tm(n, lead=seq):
        return pl.BlockSpec((lead, bbs, n), lambda i: (0, i, 0))

    smem = pl.BlockSpec(memory_space=pltpu.SMEM)
    names = ["gmlp_ln_g", "gmlp_ln_b", "lru_conv_w", "lru_conv_b", "w_ax", "lru_ba", "lru_bx", "lru_lambda",
             "w_pa", "w_pb", "w_pc", "w_out", "ln_g", "ln_b"]
    return pl.pallas_call(
        kern,
        grid=(nb // bbs,),
        in_specs=[tm(D_MODEL), tm(D_MODEL), tm(p_tm.shape[-1]),
                  tm(D_MODEL, CONV_W - 1), pl.BlockSpec((bbs, D_MODEL), lambda i: (i, 0)),
                  smem, smem] + [_at_layer(lw[n], layer) for n in names],
        out_specs=[tm(D_MODEL), tm(D_MODEL), tm(D_MODEL, CONV_W - 1),
                   pl.BlockSpec((bbs, D_MODEL), lambda i: (i, 0))],
        out_shape=[
            jax.ShapeDtypeStruct((seq, nb, D_MODEL), F32),
            jax.ShapeDtypeStruct((seq, nb, D_MODEL), F32),
            jax.ShapeDtypeStruct((CONV_W - 1, nb, D_MODEL), F32),
            jax.ShapeDtypeStruct((nb, D_MODEL), F32),
        ],
        compiler_params=pltpu.CompilerParams(
            dimension_semantics=("arbitrary",), vmem_limit_bytes=VMEM_LIMIT_BYTES),
        name="rest_sample",
    )(x_tm, ya_tm, p_tm, buf_tm, h0, lw["ws_small"][layer], lw["bs_small"][layer],
      *[lw[n] for n in names])


def _pack_w_kernel(lo_ref, hi_ref, o_ref, gate_ref, kt_ref, *, shift):
    blk = pl.program_id(1)

    @pl.when(blk < BLK_U)
    def _():
        o_ref[...] = pltpu.bitcast(lo_ref[...].T.astype(BF16), U32)

    @pl.when(blk == BLK_U - 1)
    def _():
        gate_ref[...] = hi_ref[...]

    @pl.when(blk == BLK_K)
    def _():
        kt_ref[...] = lo_ref[...].astype(BF16)

    @pl.when(blk >= BLK_U)
    def _():
        rows = jnp.concatenate([lo_ref[shift:, :], hi_ref[...]], axis=0)
        o_ref[...] = pltpu.bitcast(rows.T.astype(BF16), U32)


def _pack_w(w_in):
    depth = w_in.shape[0]
    shift = 2 * N_HEADS
    assert shift == V7X_SUBLANES
    w_t = jnp.swapaxes(w_in, 1, 2)
    kern = functools.partial(_pack_w_kernel, shift=shift)
    return pl.pallas_call(
        kern,
        grid=(depth, N_BLOCKS),
        in_specs=[
            pl.BlockSpec((None, D_MODEL, D_MODEL), lambda l, k: (l, k, 0)),
            pl.BlockSpec((None, shift, D_MODEL), lambda l, k: (l, (k + 1) * (D_MODEL // shift), 0)),
        ],
        out_specs=[pl.BlockSpec((None, D_MODEL // 2, D_MODEL), lambda l, k: (l, 0, k)),
                   pl.BlockSpec((None, shift, D_MODEL), lambda l, k: (l, 0, 0)),
                   pl.BlockSpec((None, D_MODEL, D_MODEL), lambda l, k: (l, 0, 0))],
        out_shape=[jax.ShapeDtypeStruct((depth, D_MODEL // 2, N_BLOCKS * D_MODEL), U32),
                   jax.ShapeDtypeStruct((depth, shift, D_MODEL), F32),
                   jax.ShapeDtypeStruct((depth, D_MODEL, D_MODEL), BF16)],
        compiler_params=pltpu.CompilerParams(
            dimension_semantics=("arbitrary", "arbitrary"), vmem_limit_bytes=VMEM_LIMIT_BYTES),
        name="pack_w",
    )(w_t, w_t)


def _pack_rows_kernel(x_ref, o_ref):
    o_ref[...] = pltpu.bitcast(x_ref[...].astype(BF16), U32)


def _pack_rows(arr):
    depth, r, c = arr.shape
    return pl.pallas_call(
        _pack_rows_kernel,
        grid=(depth,),
        in_specs=[pl.BlockSpec((None, r, c), lambda l: (l, 0, 0))],
        out_specs=pl.BlockSpec((None, r // 2, c), lambda l: (l, 0, 0)),
        out_shape=jax.ShapeDtypeStruct((depth, r // 2, c), U32),
        compiler_params=pltpu.CompilerParams(
            dimension_semantics=("arbitrary",), vmem_limit_bytes=VMEM_LIMIT_BYTES),
        name="pack_rows",
    )(arr)


def _pack_params(w_in, b_in, mlstm_norm_g, gmlp_ln_g, gmlp_ln_b, gmlp_ws, gmlp_bs, lru_conv_w, lru_conv_b,
                 lru_wa, lru_ba, lru_wx, lru_bx, lru_lambda, w_proj_a, w_proj_b, w_proj_c, w_out, ln_g, ln_b,
                 seq_s):
    depth = w_in.shape[0]
    d = D_MODEL
    o_i = 5 * d
    o_f = o_i + N_HEADS
    o_rest = o_f + N_HEADS

    def row(a):
        return a[:, None, :]

    def pad_lanes(a, n):
        return jnp.pad(a, [(0, 0)] * (a.ndim - 1) + [(0, n - a.shape[-1])])

    def drop_gate_cols(a):
        return jnp.concatenate([a[..., :o_i], a[..., o_rest:]], axis=-1)

    w_all, w_gate_t, w_kt = _pack_w(w_in)
    w_gate = jnp.swapaxes(w_gate_t, 1, 2)
    wi, wf = w_gate[..., :N_HEADS], w_gate[..., N_HEADS:2 * N_HEADS]
    bi, bf = b_in[..., o_i:o_f], b_in[..., o_f:o_rest]
    pad_rows = [(0, 0), (0, V7X_SUBLANES - N_HEADS), (0, 0)]
    lw = {
        "w_all": w_all,
        "b_all": row(drop_gate_cols(b_in)),
        "w_kt": w_kt,
        "b_kt": b_in[:, BLK_K * d:(BLK_K + 1) * d, None],
        "w_gc": _pack_rows(jnp.concatenate([pad_lanes(wi, V7X_LANES), pad_lanes(wf, V7X_LANES)], axis=-1)),
        "b_gc": row(jnp.concatenate([pad_lanes(bi, V7X_LANES), pad_lanes(bf, V7X_LANES)], axis=-1)),
        "w_gr": jnp.concatenate([jnp.pad(jnp.swapaxes(wi, 1, 2), pad_rows),
                                 jnp.pad(jnp.swapaxes(wf, 1, 2), pad_rows)], axis=1).astype(BF16),
        "b_gr": jnp.concatenate([jnp.pad(bi[:, :, None], pad_rows),
                                 jnp.pad(bf[:, :, None], pad_rows)], axis=1),
        "norm_g": row(mlstm_norm_g),
        "gmlp_ln_g": row(gmlp_ln_g), "gmlp_ln_b": row(gmlp_ln_b),
        "gmlp_ws": gmlp_ws,
        "gmlp_bs_t": jnp.swapaxes(gmlp_bs, 1, 2),
        "ws_small": gmlp_ws[:, :, :seq_s, :seq_s].reshape(depth, -1),
        "bs_small": gmlp_bs[:, :, :seq_s].reshape(depth, -1),
        "lru_conv_w": lru_conv_w, "lru_conv_b": row(lru_conv_b),
        "w_ax": _pack_rows(jnp.concatenate([lru_wa, lru_wx], axis=-1).reshape(depth, d, 2 * LRU_BLK)),
        "lru_ba": row(lru_ba), "lru_bx": row(lru_bx), "lru_lambda": row(lru_lambda),
        "w_pa": _pack_rows(w_proj_a), "w_pb": _pack_rows(w_proj_b), "w_pc": _pack_rows(w_proj_c),
        "w_out": _pack_rows(w_out),
        "ln_g": row(ln_g), "ln_b": row(ln_b),
    }
    return lw


def _prompt_layer(x, lw, layer, alpha):
    merged, c, n, m = _mlstm_prompt(x, lw, layer)
    x_new, conv, h = _mix_prompt(x, merged, lw, layer, alpha)
    return x_new, c, n[:, :N_HEADS], m[:, 0, :N_HEADS], conv, h[:, 0]


def _sample_layer(x_bm, c_all, c_new_all, n0, m0, conv_buf, h0, lw, layer, alpha):
    nb, seq, d = x_bm.shape
    x_rows = x_bm.reshape(nb * seq, d)
    x_tm = jnp.swapaxes(x_bm, 0, 1)
    x_tm_rows = x_tm.reshape(seq * nb, d)
    n_rest = N_BLOCKS - BLK_U
    proj_a = _proj(x_rows, lw["w_all"], lw["b_all"], layer, 0, BLK_U * d, d)
    gates = _proj(x_rows, lw["w_gc"], lw["b_gc"], layer, 0, 2 * V7X_LANES, 2 * V7X_LANES)
    p_tm = _proj(x_tm_rows, lw["w_all"], lw["b_all"], layer, BLK_U * d, n_rest * d, d)
    p_tm = p_tm.reshape(seq, nb, n_rest * d)

    n0_rows = jnp.repeat(n0.reshape(nb, d), seq, axis=0)
    m0_rows = jnp.repeat(jnp.pad(m0, ((0, 0), (0, V7X_LANES - N_HEADS))), seq, axis=0)
    ya, c_new_all, n_rows, m_rows = _mlstm_sample(
        proj_a, gates, c_all, c_new_all, n0_rows, m0_rows, lw, layer, seq)
    n_new = n_rows.reshape(nb, seq, N_HEADS, D_HEAD)[:, seq - 1]
    m_new = m_rows.reshape(nb, seq, V7X_LANES)[:, seq - 1, :N_HEADS]

    ya_tm = jnp.swapaxes(ya.reshape(nb, seq, d), 0, 1)
    buf_tm = jnp.swapaxes(conv_buf, 0, 1)
    x_new_tm, vn_tm, conv_tm, h_new = _rest_sample(x_tm, ya_tm, p_tm, buf_tm, h0, lw, layer, alpha)
    return (jnp.swapaxes(x_new_tm, 0, 1), c_new_all, n_new, m_new, jnp.swapaxes(conv_tm, 0, 1), h_new,
            jnp.swapaxes(vn_tm, 0, 1))


def kernel(x_prompt, x_sample, state_mlstm_c, state_mlstm_n, state_mlstm_m, state_lru_conv, state_lru_h,
           w_in, b_in, mlstm_norm_g, gmlp_ln_g, gmlp_ln_b, gmlp_ws, gmlp_bs, lru_conv_w, lru_conv_b,
           lru_wa, lru_ba, lru_wx, lru_bx, lru_lambda, w_proj_a, w_proj_b, w_proj_c, w_out, ln_g, ln_b):
    depth = w_in.shape[0]
    alpha = float((2 * depth) ** 0.25)
    seq_s = x_sample.shape[1]
    assert V7X_SUBLANES % seq_s == 0 and x_prompt.shape[1] % TT_MLSTM == 0
    lw = _pack_params(w_in, b_in, mlstm_norm_g, gmlp_ln_g, gmlp_ln_b, gmlp_ws, gmlp_bs, lru_conv_w,
                      lru_conv_b, lru_wa, lru_ba, lru_wx, lru_bx, lru_lambda, w_proj_a, w_proj_b, w_proj_c,
                      w_out, ln_g, ln_b, seq_s)
    xp, xs = x_prompt, x_sample
    outs_p = [[] for _ in range(5)]
    outs_s = [[] for _ in range(5)]
    c_new_all = None
    for layer in range(depth):
        xp, *st = _prompt_layer(xp, lw, layer, alpha)
        for acc, val in zip(outs_p, st):
            acc.append(val)
        xs, c_new_all, *st = _sample_layer(
            xs, state_mlstm_c, c_new_all, state_mlstm_n[layer], state_mlstm_m[layer],
            state_lru_conv[layer], state_lru_h[layer], lw, layer, alpha)
        for acc, val in zip(outs_s, st):
            acc.append(val)
    stacked_p = [jnp.stack(v) for v in outs_p]
    stacked_s = [jnp.stack(v) for v in outs_s]
    return (xp, xs, *stacked_p, c_new_all, *stacked_s)
```

```python
import functools

import jax
import jax.numpy as jnp
from jax import lax
from jax.experimental import pallas as pl
from jax.experimental.pallas import tpu as pltpu

D_MODEL = 1024
N_HEADS = 4
D_HEAD = D_MODEL // N_HEADS
CHUNK = 128
N_GROUPS = 4
D_GROUP = D_MODEL // N_GROUPS
LRU_BLOCKS = 8
LRU_BLK = D_MODEL // LRU_BLOCKS
CONV_W = 4
LRU_C = 8.0
LN_EPS = 1e-5
K_SCALE = D_HEAD ** -0.5
LOG2_E = 1.4426950408889634
V7X_LANES = 128
V7X_SUBLANES = 8
VMEM_LIMIT_BYTES = 56 * 1024 * 1024

TT_MLSTM = 512
TT_MIX = 512
SB_MIX = 512
BB_MLSTM_SAMPLE = 8
BB_REST_SAMPLE = 64

BLK_Q, BLK_K, BLK_V, BLK_O, BLK_ZA = 0, 1, 2, 3, 4
BLK_U, BLK_VB, BLK_ZB = 5, 6, 7
BLK_XC, BLK_ZC = 8, 9
BLK_GA, BLK_GB, BLK_GC = 10, 11, 12
N_BLOCKS = 13

BF16 = jnp.bfloat16
U32 = jnp.uint32
F32 = jnp.float32

_NT = (((1,), (1,)), ((), ()))
_TN = (((0,), (0,)), ((), ()))


def _dot(a, b):
    return jnp.dot(a, b, preferred_element_type=F32)


def _dot_nt(a, b):
    return lax.dot_general(a, b, _NT, preferred_element_type=F32)


def _dot_tn(a, b):
    return lax.dot_general(a, b, _TN, preferred_element_type=F32)


def _unpack(ref, rows=slice(None), cols=slice(None)):
    return pltpu.bitcast(ref[rows, cols], BF16)


def _layer_norm(x, g, b):
    mu = jnp.mean(x, axis=-1, keepdims=True)
    xc = x - mu
    var = jnp.mean(xc * xc, axis=-1, keepdims=True)
    return xc * lax.rsqrt(var + LN_EPS) * g + b


def _silu(x):
    return x * jax.nn.sigmoid(x)


def _scan_axis(x, axis, op, fill):
    n = x.shape[axis]
    idx = lax.broadcasted_iota(jnp.int32, x.shape, axis)
    s = 1
    while s < n:
        x = op(x, jnp.where(idx >= s, pltpu.roll(x, s, axis), fill))
        s *= 2
    return x


def _at_layer(arr, layer):
    shape = arr.shape[1:]
    nd = len(shape)
    return pl.BlockSpec((None,) + shape, lambda *_: (layer,) + (0,) * nd, pipeline_mode=pl.Buffered(1))


def _w_block(layer, blk):
    return pl.BlockSpec((None, D_MODEL // 2, D_MODEL), lambda *_: (layer, 0, blk), pipeline_mode=pl.Buffered(1))


def _b_block(layer, blk):
    return pl.BlockSpec((None, 1, D_MODEL), lambda *_: (layer, 0, blk), pipeline_mode=pl.Buffered(1))


def _mlstm_prompt_kernel(x_ref, wq_ref, wv_ref, wo_ref, wz_ref, wg_ref, bq_ref, bv_ref, bo_ref, bz_ref, bg_ref,
                         wkt_ref, bkt_ref, wgc_ref, bgc_ref, wgr_ref, bgr_ref, ng_ref, wp_ref,
                         o_ref, c_out, n_out, m_out,
                         c_s, n_s, mc_s, mr_s, q_s, kt_s, v_s, o_s, h_s,
                         acol_s, inter_s, em_s, decay_s, grow_s, wrow_s, *, tt):
    j = pl.program_id(1)
    nch = tt // CHUNK

    @pl.when(j == 0)
    def _():
        c_s[...] = jnp.zeros_like(c_s)
        n_s[...] = jnp.zeros_like(n_s)
        mc_s[...] = jnp.zeros_like(mc_s)
        mr_s[...] = jnp.zeros_like(mr_s)

    xb = x_ref[...].astype(BF16)

    def col(w_ref, b_ref):
        return _dot(xb, _unpack(w_ref)) + b_ref[...]

    def project_q():
        q_s[...] = col(wq_ref, bq_ref).astype(BF16)

    def project_k():
        kt_s[...] = (_dot_nt(wkt_ref[...], xb) + bkt_ref[...]) * K_SCALE

    def project_v():
        v_s[...] = col(wv_ref, bv_ref).astype(BF16)

    def project_o():
        o_s[...] = jax.nn.sigmoid(col(wo_ref, bo_ref))

    projections = [project_q, project_k, project_v, project_o]

    gcol = _dot(xb, _unpack(wgc_ref)) + bgc_ref[...]
    it_c = gcol[:, :V7X_LANES]
    lf_c = jax.nn.log_sigmoid(gcol[:, V7X_LANES:])
    grow = _dot_nt(wgr_ref[...], xb) + bgr_ref[...]
    it_r = grow[:V7X_SUBLANES]
    lf_r = jax.nn.log_sigmoid(grow[V7X_SUBLANES:])

    m0c = mc_s[0:1, :]
    m0r = mr_s[...]
    for c in range(nch):
        if c < len(projections):
            projections[c]()
        sl = slice(c * CHUNK, (c + 1) * CHUNK)
        bcol = _scan_axis(lf_c[sl], 0, jnp.add, 0.0)
        gcl = it_c[sl] - bcol
        mcol = bcol + jnp.maximum(m0c, _scan_axis(gcl, 0, jnp.maximum, -jnp.inf))
        blast = bcol[CHUNK - 1:CHUNK]
        mlast = mcol[CHUNK - 1:CHUNK]
        acol = bcol - mcol
        inter = jnp.exp(bcol + m0c - mcol)
        em = jnp.exp(-mcol)
        decay_s[c:c + 1, :] = jnp.exp(blast + m0c - mlast)
        for h in range(N_HEADS):
            acol_s[h, sl, :] = jnp.broadcast_to(acol[:, h:h + 1], (CHUNK, V7X_LANES))
            inter_s[h, sl, :] = jnp.broadcast_to(inter[:, h:h + 1], (CHUNK, V7X_LANES))
            em_s[h, sl, :] = jnp.broadcast_to(em[:, h:h + 1], (CHUNK, V7X_LANES))
        brow = _scan_axis(lf_r[:, sl], 1, jnp.add, 0.0)
        grw = it_r[:, sl] - brow
        mrow = brow + jnp.maximum(m0r, _scan_axis(grw, 1, jnp.maximum, -jnp.inf))
        mlast_r = mrow[:, CHUNK - 1:CHUNK]
        grow_s[:, sl] = grw
        wrow_s[:, sl] = jnp.exp(brow[:, CHUNK - 1:CHUNK] + grw - mlast_r)
        m0c = mlast
        m0r = jnp.broadcast_to(mlast_r, m0r.shape)
    for project in projections[nch:]:
        project()
    mc_s[...] = jnp.broadcast_to(m0c, mc_s.shape)
    mr_s[...] = m0r

    tri = (lax.broadcasted_iota(jnp.int32, (CHUNK, CHUNK), 0)
           >= lax.broadcasted_iota(jnp.int32, (CHUNK, CHUNK), 1))
    ones_rows = jnp.ones((V7X_SUBLANES, CHUNK), BF16)

    for c in range(nch):
        rows = slice(c * CHUNK, (c + 1) * CHUNK)
        decay_c = decay_s[c:c + 1, :]
        for h in range(N_HEADS):
            hc = slice(h * D_HEAD, (h + 1) * D_HEAD)
            qc = q_s[rows, hc]
            kt = kt_s[hc, rows]
            vc = v_s[rows, hc]
            inter = inter_s[h, rows, :]
            decay = decay_c[:, h:h + 1]

            dmat = jnp.exp(jnp.where(tri, acol_s[h, rows, :] + grow_s[h:h + 1, rows], -jnp.inf))
            s = _dot(qc, kt.astype(BF16)) * dmat
            c0 = c_s[h]
            n0 = n_s[h:h + 1, :]
            num = (jnp.concatenate([inter, inter], axis=1) * _dot(qc, c0.astype(BF16))
                   + _dot(s.astype(BF16), vc))
            qn = qc.astype(F32) * n0
            den = jnp.sum(inter * (qn[:, :V7X_LANES] + qn[:, V7X_LANES:]) + s, axis=1, keepdims=True)
            rden = 1.0 / jnp.maximum(jnp.abs(den), em_s[h, rows, :])
            h_s[rows, hc] = num * jnp.concatenate([rden, rden], axis=1)

            kwt = (kt * wrow_s[h:h + 1, rows]).astype(BF16)
            c_s[h] = decay * c0 + _dot(kwt, vc)
            n_s[h:h + 1, :] = decay * n0 + _dot_nt(ones_rows, kwt)[0:1, :]

    hh = o_s[...] * h_s[...]
    parts = []
    for h in range(N_HEADS):
        hc = slice(h * D_HEAD, (h + 1) * D_HEAD)
        hd = hh[:, hc]
        mu = jnp.mean(hd, axis=-1, keepdims=True)
        hd = hd - mu
        var = jnp.mean(hd * hd, axis=-1, keepdims=True)
        parts.append(hd * lax.rsqrt(var + LN_EPS) * ng_ref[:, hc])
    y = (jnp.concatenate(parts, axis=1) * _silu(col(wz_ref, bz_ref))).astype(BF16)
    o_ref[...] = jax.nn.sigmoid(col(wg_ref, bg_ref)) * _dot(y, _unpack(wp_ref))

    @pl.when(j == pl.num_programs(1) - 1)
    def _():
        c_out[...] = c_s[...]
        n_out[...] = n_s[...]
        m_out[...] = mc_s[...]


def _mlstm_prompt(x, lw, layer):
    bsz, t, _ = x.shape
    tt = TT_MLSTM
    kern = functools.partial(_mlstm_prompt_kernel, tt=tt)
    tile = pl.BlockSpec((None, tt, D_MODEL), lambda b, j: (b, j, 0))
    blocks = [BLK_Q, BLK_V, BLK_O, BLK_ZA, BLK_GA]
    names = ["w_kt", "b_kt", "w_gc", "b_gc", "w_gr", "b_gr", "norm_g", "w_pa"]
    head_lanes = (N_HEADS, tt, V7X_LANES)
    return pl.pallas_call(
        kern,
        grid=(bsz, t // tt),
        in_specs=([tile] + [_w_block(layer, blk) for blk in blocks] + [_b_block(layer, blk) for blk in blocks]
                  + [_at_layer(lw[n], layer) for n in names]),
        out_specs=[
            tile,
            pl.BlockSpec((None, N_HEADS, D_HEAD, D_HEAD), lambda b, j: (b, 0, 0, 0)),
            pl.BlockSpec((None, V7X_SUBLANES, D_HEAD), lambda b, j: (b, 0, 0)),
            pl.BlockSpec((None, V7X_SUBLANES, V7X_LANES), lambda b, j: (b, 0, 0)),
        ],
        out_shape=[
            jax.ShapeDtypeStruct((bsz, t, D_MODEL), F32),
            jax.ShapeDtypeStruct((bsz, N_HEADS, D_HEAD, D_HEAD), F32),
            jax.ShapeDtypeStruct((bsz, V7X_SUBLANES, D_HEAD), F32),
            jax.ShapeDtypeStruct((bsz, V7X_SUBLANES, V7X_LANES), F32),
        ],
        scratch_shapes=[
            pltpu.VMEM((N_HEADS, D_HEAD, D_HEAD), F32),
            pltpu.VMEM((V7X_SUBLANES, D_HEAD), F32),
            pltpu.VMEM((V7X_SUBLANES, V7X_LANES), F32),
            pltpu.VMEM((V7X_SUBLANES, V7X_LANES), F32),
            pltpu.VMEM((tt, D_MODEL), BF16),
            pltpu.VMEM((D_MODEL, tt), F32),
            pltpu.VMEM((tt, D_MODEL), BF16),
            pltpu.VMEM((tt, D_MODEL), F32),
            pltpu.VMEM((tt, D_MODEL), F32),
            pltpu.VMEM(head_lanes, F32),
            pltpu.VMEM(head_lanes, F32),
            pltpu.VMEM(head_lanes, F32),
            pltpu.VMEM((V7X_SUBLANES, V7X_LANES), F32),
            pltpu.VMEM((V7X_SUBLANES, tt), F32),
            pltpu.VMEM((V7X_SUBLANES, tt), F32),
        ],
        compiler_params=pltpu.CompilerParams(
            dimension_semantics=("arbitrary", "arbitrary"), vmem_limit_bytes=VMEM_LIMIT_BYTES),
        name="mlstm_prompt",
    )(x, *([lw["w_all"]] * len(blocks)), *([lw["b_all"]] * len(blocks)), *[lw[n] for n in names])


def _lru_gates(xc, wax_ref, ba, bx, lam):
    xcb = xc.astype(BF16)
    half = LRU_BLK // 2
    pre = [_dot(xcb[:, n * LRU_BLK:(n + 1) * LRU_BLK], _unpack(wax_ref, slice(n * half, (n + 1) * half)))
           for n in range(LRU_BLOCKS)]
    r = jax.nn.sigmoid(jnp.concatenate([p[:, :LRU_BLK] for p in pre], axis=1) + ba)
    i = jax.nn.sigmoid(jnp.concatenate([p[:, LRU_BLK:] for p in pre], axis=1) + bx)
    a = jnp.exp2(r * (LRU_C * LOG2_E * jax.nn.log_sigmoid(lam)))
    v = 1.0 - a * a
    mult = jnp.where(v > 0.0, v * lax.rsqrt(v), 0.0)
    return a, mult, i


def _mix_kernel(x_ref, ma_ref,
                wu_ref, wvb_ref, wzb_ref, wgb_ref, wxc_ref, wzc_ref, wgc_ref,
                bu_ref, bvb_ref, bzb_ref, bgb_ref, bxc_ref, bzc_ref, bgc_ref,
                lng1_ref, lnb1_ref, ws_ref, bs_ref, wpb_ref,
                cw_ref, cb_ref, wax_ref, ba_ref, bx_ref, lam_ref, wpc_ref,
                wo_ref, lng_ref, lnb_ref,
                o_ref, conv_out, h_out,
                ucar_s, vn_s, h_s, hcar_s, *, tt, sb, alpha):
    j = pl.program_id(1)
    nch = tt // CHUNK
    nsb = tt // sb

    @pl.when(j == 0)
    def _():
        ucar_s[...] = jnp.broadcast_to(cb_ref[...], ucar_s.shape)
        hcar_s[...] = jnp.zeros_like(hcar_s)

    x = x_ref[...]
    xb = x.astype(BF16)
    state = {}

    def col(w_ref, b_ref, cols=slice(None), rows=slice(None)):
        return _dot(xb[rows], _unpack(w_ref, cols=cols)) + b_ref[:, cols]

    ngs = sb // V7X_SUBLANES
    sub = lax.broadcasted_iota(jnp.int32, (ngs, V7X_SUBLANES, D_MODEL), 1)
    conv_tail = [ucar_s[jj] for jj in range(CONV_W - 1)]

    def lru_input(k):
        state["x_c", k] = col(wxc_ref, bxc_ref, rows=slice(k * sb, (k + 1) * sb))

    def lru_block(k, hprev):
        r0 = k * sb
        x3 = state["x_c", k].reshape(ngs, V7X_SUBLANES, D_MODEL)
        u = cb_ref[...] + cw_ref[0:1, :] * x3
        for jj in range(1, CONV_W):
            rot = pltpu.roll(u, 1, 1)
            prev = jnp.concatenate([conv_tail[jj - 1][None], rot[:-1]], axis=0)
            conv_tail[jj - 1] = rot[ngs - 1]
            u = jnp.where(sub == 0, prev, rot) + cw_ref[jj:jj + 1, :] * x3
        xc = u.reshape(sb, D_MODEL)
        a, mult, i = _lru_gates(xc, wax_ref, ba_ref[...], bx_ref[...], lam_ref[...])
        if k == 0:
            row = lax.broadcasted_iota(jnp.int32, (V7X_SUBLANES, 1), 0)
            top = jnp.where(jnp.logical_and(row == 0, j == 0), 1.0, mult[:V7X_SUBLANES])
            mult = jnp.concatenate([top, mult[V7X_SUBLANES:]], axis=0)
        bterm = mult * i * xc
        a3 = a.reshape(ngs, V7X_SUBLANES, D_MODEL)
        b3 = bterm.reshape(ngs, V7X_SUBLANES, D_MODEL)
        s = 1
        while s < V7X_SUBLANES:
            keep = sub >= s
            a_sh = jnp.where(keep, pltpu.roll(a3, s, 1), 1.0)
            b_sh = jnp.where(keep, pltpu.roll(b3, s, 1), 0.0)
            b3 = a3 * b_sh + b3
            a3 = a3 * a_sh
            s *= 2
        for g in range(ngs):
            hg = a3[g] * hprev + b3[g]
            h_s[r0 + g * V7X_SUBLANES:r0 + (g + 1) * V7X_SUBLANES, :] = hg
            hprev = jnp.broadcast_to(hg[V7X_SUBLANES - 1:V7X_SUBLANES, :], hg.shape)
        return hprev

    tri = (lax.broadcasted_iota(jnp.int32, (CHUNK, CHUNK), 0)
           >= lax.broadcasted_iota(jnp.int32, (CHUNK, CHUNK), 1))

    def gmlp_norm():
        vn_s[...] = _layer_norm(col(wvb_ref, bvb_ref), lng1_ref[...], lnb1_ref[...])

    def gmlp_group(g):
        wm = jnp.where(tri, ws_ref[g], 0.0).astype(BF16)
        bias = bs_ref[:, g:g + 1]
        cols = slice(g * D_GROUP, (g + 1) * D_GROUP)
        mixed = jnp.concatenate(
            [_dot(wm, vn_s[c * CHUNK:(c + 1) * CHUNK, cols].astype(BF16)) + bias for c in range(nch)],
            axis=0)
        y = (col(wu_ref, bu_ref, cols) * mixed * _silu(col(wzb_ref, bzb_ref, cols))).astype(BF16)
        contrib = _dot(y, _unpack(wpb_ref, slice(g * D_GROUP // 2, (g + 1) * D_GROUP // 2)))
        state["acc"] = contrib if g == 0 else state["acc"] + contrib

    def lru_silu_z():
        state["sz_c"] = _silu(col(wzc_ref, bzc_ref))

    def gmlp_gate():
        state["merged"] = ma_ref[...] + jax.nn.sigmoid(col(wgb_ref, bgb_ref)) * state["acc"]

    def lru_gate():
        state["g_c"] = jax.nn.sigmoid(col(wgc_ref, bgc_ref))

    steps = ([gmlp_norm, lru_silu_z] + [functools.partial(gmlp_group, g) for g in range(N_GROUPS)]
             + [gmlp_gate, lru_gate])
    bounds = [round(k * len(steps) / nsb) for k in range(nsb + 1)]
    hprev = hcar_s[...]
    lru_input(0)
    for k in range(nsb):
        if k + 1 < nsb:
            lru_input(k + 1)
        for step in steps[bounds[k]:bounds[k + 1]]:
            step()
        hprev = lru_block(k, hprev)
    hcar_s[...] = hprev
    for jj in range(CONV_W - 1):
        ucar_s[jj] = conv_tail[jj]

    @pl.when(j == pl.num_programs(1) - 1)
    def _():
        conv_out[...] = state["x_c", nsb - 1][sb - (CONV_W - 1):, :]
        h_out[...] = hprev

    y_c = (h_s[...] * state["sz_c"]).astype(BF16)
    merged = state["merged"] + state["g_c"] * _dot(y_c, _unpack(wpc_ref))
    out = _dot(merged.astype(BF16), _unpack(wo_ref))
    o_ref[...] = _layer_norm(alpha * x + out, lng_ref[...], lnb_ref[...])


def _mix_prompt(x, merged, lw, layer, alpha):
    bsz, t, _ = x.shape
    tt = TT_MIX
    kern = functools.partial(_mix_kernel, tt=tt, sb=SB_MIX, alpha=alpha)
    tile = pl.BlockSpec((None, tt, D_MODEL), lambda b, j: (b, j, 0))
    blocks = [BLK_U, BLK_VB, BLK_ZB, BLK_GB, BLK_XC, BLK_ZC, BLK_GC]
    names = ["gmlp_ln_g", "gmlp_ln_b", "gmlp_ws", "gmlp_bs_t", "w_pb",
             "lru_conv_w", "lru_conv_b", "w_ax", "lru_ba", "lru_bx", "lru_lambda", "w_pc",
             "w_out", "ln_g", "ln_b"]
    return pl.pallas_call(
        kern,
        grid=(bsz, t // tt),
        in_specs=([tile, tile] + [_w_block(layer, blk) for blk in blocks]
                  + [_b_block(layer, blk) for blk in blocks] + [_at_layer(lw[n], layer) for n in names]),
        out_specs=[
            tile,
            pl.BlockSpec((None, CONV_W - 1, D_MODEL), lambda b, j: (b, 0, 0)),
            pl.BlockSpec((None, V7X_SUBLANES, D_MODEL), lambda b, j: (b, 0, 0)),
        ],
        out_shape=[
            jax.ShapeDtypeStruct((bsz, t, D_MODEL), F32),
            jax.ShapeDtypeStruct((bsz, CONV_W - 1, D_MODEL), F32),
            jax.ShapeDtypeStruct((bsz, V7X_SUBLANES, D_MODEL), F32),
        ],
        scratch_shapes=[
            pltpu.VMEM((CONV_W - 1, V7X_SUBLANES, D_MODEL), F32),
            pltpu.VMEM((tt, D_MODEL), F32),
            pltpu.VMEM((tt, D_MODEL), F32),
            pltpu.VMEM((V7X_SUBLANES, D_MODEL), F32),
        ],
        compiler_params=pltpu.CompilerParams(
            dimension_semantics=("arbitrary", "arbitrary"), vmem_limit_bytes=VMEM_LIMIT_BYTES),
        name="mix_prompt",
    )(x, merged, *([lw["w_all"]] * len(blocks)), *([lw["b_all"]] * len(blocks)), *[lw[n] for n in names])


def _proj_kernel(x_ref, w_ref, b_ref, o_ref):
    o_ref[...] = _dot(x_ref[...].astype(BF16), _unpack(w_ref)) + b_ref[...]


def _proj(x, w, b, layer, col0, ncols, bn):
    rows = x.shape[0]
    off = col0 // bn
    return pl.pallas_call(
        _proj_kernel,
        grid=(ncols // bn,),
        in_specs=[
            pl.BlockSpec((rows, D_MODEL), lambda n: (0, 0)),
            pl.BlockSpec((None, D_MODEL // 2, bn), lambda n: (layer, 0, off + n)),
            pl.BlockSpec((None, 1, bn), lambda n: (layer, 0, off + n)),
        ],
        out_specs=pl.BlockSpec((rows, bn), lambda n: (0, n)),
        out_shape=jax.ShapeDtypeStruct((rows, ncols), F32),
        compiler_params=pltpu.CompilerParams(
            dimension_semantics=("arbitrary",), vmem_limit_bytes=VMEM_LIMIT_BYTES),
        name="proj_sample",
    )(x, w, b)


def _mlstm_sample_kernel(p_ref, g_ref, c_ref, n_ref, m_ref, ng_ref, *rest, bb, seq):
    y_ref, c_out, n_out, m_out, qc_s = rest[-5:]
    rows = bb * seq
    t_idx = lax.broadcasted_iota(jnp.int32, (rows, 1), 0) % seq

    def down(x, d, fill=0.0):
        if d == 0:
            return x
        return jnp.where(t_idx >= d, pltpu.roll(x, d, 0), fill)

    def from_last(x):
        out = x
        for d in range(1, seq):
            out = jnp.where(t_idx == seq - 1 - d, pltpu.roll(x, rows - d, 0), out)
        return out

    it_all = g_ref[:, :V7X_LANES]
    lf_all = jax.nn.log_sigmoid(g_ref[:, V7X_LANES:])
    b_all = lf_all
    for d in range(1, seq):
        b_all = b_all + down(lf_all, d)
    g_all = it_all - b_all
    mx_all = g_all
    for d in range(1, seq):
        mx_all = jnp.maximum(mx_all, down(g_all, d, -jnp.inf))
    m0_all = m_ref[...]
    m_all = b_all + jnp.maximum(m0_all, mx_all)
    mlast_all = from_last(m_all)
    blast_all = from_last(b_all)
    inter_all = jnp.exp(b_all + m0_all - m_all)
    em_all = jnp.exp(-m_all)
    w_all = jnp.exp(blast_all + g_all - mlast_all)
    decay_all = jnp.exp(blast_all + m0_all - mlast_all)
    a_all = b_all - m_all
    m_out[...] = mlast_all

    def head_cols(blk, h):
        return slice(blk * D_MODEL + h * D_HEAD, blk * D_MODEL + (h + 1) * D_HEAD)

    for h in range(N_HEADS):
        hc = slice(h * D_HEAD, (h + 1) * D_HEAD)
        q = p_ref[:, head_cols(BLK_Q, h)]
        k = p_ref[:, head_cols(BLK_K, h)] * K_SCALE
        v = p_ref[:, head_cols(BLK_V, h)]
        lane = slice(h, h + 1)
        a_col, g_col = a_all[:, lane], g_all[:, lane]
        inter, em = inter_all[:, lane], em_all[:, lane]
        wcol, decay = w_all[:, lane], decay_all[:, lane]

        kw = k * wcol
        per_slab = V7X_SUBLANES // seq
        grp = lax.broadcasted_iota(jnp.int32, (V7X_SUBLANES, 1), 0) // seq
        for slab in range(rows // V7X_SUBLANES):
            s0 = slab * V7X_SUBLANES
            q8 = q[s0:s0 + V7X_SUBLANES, :].astype(BF16)
            kw8 = kw[s0:s0 + V7X_SUBLANES, :]
            v8 = v[s0:s0 + V7X_SUBLANES, :].astype(BF16)
            qc8 = jnp.zeros((V7X_SUBLANES, D_HEAD), F32)
            for e in range(per_slab):
                b = slab * per_slab + e
                c0 = c_ref[b, h]
                qc8 = jnp.where(grp == e, _dot(q8, c0.astype(BF16)), qc8)
                kw_e = jnp.where(grp == e, kw8, 0.0).astype(BF16)
                c_out[b, h] = decay[b * seq:b * seq + 1, :] * c0 + _dot_tn(kw_e, v8)
            qc_s[s0:s0 + V7X_SUBLANES, :] = qc8

        n0 = n_ref[:, hc]
        num = inter * qc_s[...]
        den = inter * jnp.sum(q * n0, axis=1, keepdims=True)
        for d in range(seq):
            s_d = jnp.sum(q * down(k, d), axis=1, keepdims=True) * jnp.exp(a_col + down(g_col, d))
            s_d = jnp.where(t_idx >= d, s_d, 0.0)
            num = num + s_d * down(v, d)
            den = den + s_d
        hh = num / jnp.maximum(jnp.abs(den), em)

        ksum = kw
        for d in range(1, seq):
            ksum = ksum + down(kw, d)
        n_out[:, hc] = decay * n0 + ksum

        hh = jax.nn.sigmoid(p_ref[:, head_cols(BLK_O, h)]) * hh
        mu = jnp.mean(hh, axis=-1, keepdims=True)
        hcn = hh - mu
        var = jnp.mean(hcn * hcn, axis=-1, keepdims=True)
        hn = hcn * lax.rsqrt(var + LN_EPS) * ng_ref[:, hc]
        y_ref[:, hc] = hn * _silu(p_ref[:, head_cols(BLK_ZA, h)])


def _mlstm_sample(proj_a, gates, c_all, c_new_all, n0_rows, m0_rows, lw, layer, seq):
    nb = c_all.shape[1]
    bb = BB_MLSTM_SAMPLE
    rows = bb * seq
    kern = functools.partial(_mlstm_sample_kernel, bb=bb, seq=seq)
    c_spec = pl.BlockSpec((None, bb, N_HEADS, D_HEAD, D_HEAD), lambda i: (layer, i, 0, 0, 0))
    in_specs = [
        pl.BlockSpec((rows, 5 * D_MODEL), lambda i: (i, 0)),
        pl.BlockSpec((rows, 2 * V7X_LANES), lambda i: (i, 0)),
        c_spec,
        pl.BlockSpec((rows, D_MODEL), lambda i: (i, 0)),
        pl.BlockSpec((rows, V7X_LANES), lambda i: (i, 0)),
        _at_layer(lw["norm_g"], layer),
    ]
    args = [proj_a, gates, c_all, n0_rows, m0_rows, lw["norm_g"]]
    aliases = {}
    if c_new_all is not None:
        in_specs.append(pl.BlockSpec(memory_space=pl.ANY))
        args.append(c_new_all)
        aliases = {len(args) - 1: 1}
    return pl.pallas_call(
        kern,
        grid=(nb // bb,),
        in_specs=in_specs,
        out_specs=[
            pl.BlockSpec((rows, D_MODEL), lambda i: (i, 0)),
            c_spec,
            pl.BlockSpec((rows, D_MODEL), lambda i: (i, 0)),
            pl.BlockSpec((rows, V7X_LANES), lambda i: (i, 0)),
        ],
        out_shape=[
            jax.ShapeDtypeStruct((nb * seq, D_MODEL), F32),
            jax.ShapeDtypeStruct(c_all.shape, F32),
            jax.ShapeDtypeStruct((nb * seq, D_MODEL), F32),
            jax.ShapeDtypeStruct((nb * seq, V7X_LANES), F32),
        ],
        scratch_shapes=[pltpu.VMEM((rows, D_HEAD), F32)],
        input_output_aliases=aliases,
        compiler_params=pltpu.CompilerParams(
            dimension_semantics=("arbitrary",), vmem_limit_bytes=VMEM_LIMIT_BYTES),
        name="mlstm_sample",
    )(*args)


def _rest_sample_kernel(x_ref, ya_ref, buf_ref, h0_ref, wsm_ref, bsm_ref, *rest, seq, bbs, alpha):
    n_blk = N_BLOCKS - BLK_U
    w_refs, b_refs = rest[:n_blk], rest[n_blk:2 * n_blk]
    (lng1_ref, lnb1_ref, cw_ref, cb_ref, wax_ref, ba_ref, bx_ref, lam_ref,
     wpa_ref, wpb_ref, wpc_ref, wo_ref, lng_ref, lnb_ref,
     o_ref, vn_out, conv_out, h_out) = rest[2 * n_blk:]
    rows = seq * bbs
    x = x_ref[...].reshape(rows, D_MODEL)
    xb = x.astype(BF16)

    def flat(blk):
        return _dot(xb, _unpack(w_refs[blk - BLK_U])) + b_refs[blk - BLK_U][...]

    vn = _layer_norm(flat(BLK_VB), lng1_ref[...], lnb1_ref[...])
    vn_out[...] = vn.reshape(seq, bbs, D_MODEL)
    mixed_t = []
    for t in range(seq):
        parts = []
        for g in range(N_GROUPS):
            cols = slice(g * D_GROUP, (g + 1) * D_GROUP)
            acc = jnp.full((bbs, D_GROUP), bsm_ref[g * seq + t], F32)
            for s in range(t + 1):
                acc = acc + wsm_ref[(g * seq + t) * seq + s] * vn[s * bbs:(s + 1) * bbs, cols]
            parts.append(acc)
        mixed_t.append(jnp.concatenate(parts, axis=1))
    mixed = jnp.concatenate(mixed_t, axis=0)
    yb = (flat(BLK_U) * mixed * _silu(flat(BLK_ZB))).astype(BF16)

    xc_raw = flat(BLK_XC)
    xp = [buf_ref[t] for t in range(CONV_W - 1)] + [xc_raw[t * bbs:(t + 1) * bbs] for t in range(seq)]
    for t in range(CONV_W - 1):
        conv_out[t] = xp[seq + t]
    xc_t = []
    for t in range(seq):
        acc = cb_ref[...] + cw_ref[0:1, :] * xp[t]
        for jj in range(1, CONV_W):
            acc = acc + cw_ref[jj:jj + 1, :] * xp[t + jj]
        xc_t.append(acc)
    xc = jnp.concatenate(xc_t, axis=0)
    a, mult, i = _lru_gates(xc, wax_ref, ba_ref[...], bx_ref[...], lam_ref[...])
    bterm = mult * i * xc
    h = h0_ref[...]
    hs = []
    for t in range(seq):
        h = a[t * bbs:(t + 1) * bbs] * h + bterm[t * bbs:(t + 1) * bbs]
        hs.append(h)
    h_out[...] = h
    yc = (jnp.concatenate(hs, axis=0) * _silu(flat(BLK_ZC))).astype(BF16)

    ya = ya_ref[...].reshape(rows, D_MODEL).astype(BF16)
    merged = (jax.nn.sigmoid(flat(BLK_GA)) * _dot(ya, _unpack(wpa_ref))
              + jax.nn.sigmoid(flat(BLK_GB)) * _dot(yb, _unpack(wpb_ref))
              + jax.nn.sigmoid(flat(BLK_GC)) * _dot(yc, _unpack(wpc_ref)))
    out = _dot(merged.astype(BF16), _unpack(wo_ref))
    o_ref[...] = _layer_norm(alpha * x + out, lng_ref[...], lnb_ref[...]).reshape(seq, bbs, D_MODEL)


def _rest_sample(x_tm, ya_tm, buf_tm, h0, lw, layer, alpha):
    seq, nb, _ = x_tm.shape
    bbs = min(BB_REST_SAMPLE, nb)
    kern = functools.partial(_rest_sample_kernel, seq=seq, bbs=bbs, alpha=alpha)

    def tm(n, lead=seq):
        return pl.BlockSpec((lead, bbs, n), lambda i: (0, i, 0))

    smem = pl.BlockSpec(memory_space=pltpu.SMEM)
    blocks = list(range(BLK_U, N_BLOCKS))
    names = ["gmlp_ln_g", "gmlp_ln_b", "lru_conv_w", "lru_conv_b", "w_ax", "lru_ba", "lru_bx", "lru_lambda",
             "w_pa", "w_pb", "w_pc", "w_out", "ln_g", "ln_b"]
    return pl.pallas_call(
        kern,
        grid=(nb // bbs,),
        in_specs=([tm(D_MODEL), tm(D_MODEL),
                   tm(D_MODEL, CONV_W - 1), pl.BlockSpec((bbs, D_MODEL), lambda i: (i, 0)),
                   smem, smem] + [_w_block(layer, blk) for blk in blocks]
                  + [_b_block(layer, blk) for blk in blocks] + [_at_layer(lw[n], layer) for n in names]),
        out_specs=[tm(D_MODEL), tm(D_MODEL), tm(D_MODEL, CONV_W - 1),
                   pl.BlockSpec((bbs, D_MODEL), lambda i: (i, 0))],
        out_shape=[
            jax.ShapeDtypeStruct((seq, nb, D_MODEL), F32),
            jax.ShapeDtypeStruct((seq, nb, D_MODEL), F32),
            jax.ShapeDtypeStruct((CONV_W - 1, nb, D_MODEL), F32),
            jax.ShapeDtypeStruct((nb, D_MODEL), F32),
        ],
        compiler_params=pltpu.CompilerParams(
            dimension_semantics=("arbitrary",), vmem_limit_bytes=VMEM_LIMIT_BYTES),
        name="rest_sample",
    )(x_tm, ya_tm, buf_tm, h0, lw["ws_small"][layer], lw["bs_small"][layer],
      *([lw["w_all"]] * len(blocks)), *([lw["b_all"]] * len(blocks)), *[lw[n] for n in names])


def _pack_w_kernel(lo_ref, hi_ref, o_ref, gate_ref, kt_ref, *, shift):
    blk = pl.program_id(1)

    @pl.when(blk < BLK_U)
    def _():
        o_ref[...] = pltpu.bitcast(lo_ref[...].T.astype(BF16), U32)

    @pl.when(blk == BLK_U - 1)
    def _():
        gate_ref[...] = hi_ref[...]

    @pl.when(blk == BLK_K)
    def _():
        kt_ref[...] = lo_ref[...].astype(BF16)

    @pl.when(blk >= BLK_U)
    def _():
        rows = jnp.concatenate([lo_ref[shift:, :], hi_ref[...]], axis=0)
        o_ref[...] = pltpu.bitcast(rows.T.astype(BF16), U32)


def _pack_w(w_in):
    depth = w_in.shape[0]
    shift = 2 * N_HEADS
    assert shift == V7X_SUBLANES
    w_t = jnp.swapaxes(w_in, 1, 2)
    kern = functools.partial(_pack_w_kernel, shift=shift)
    return pl.pallas_call(
        kern,
        grid=(depth, N_BLOCKS),
        in_specs=[
            pl.BlockSpec((None, D_MODEL, D_MODEL), lambda l, k: (l, k, 0)),
            pl.BlockSpec((None, shift, D_MODEL), lambda l, k: (l, (k + 1) * (D_MODEL // shift), 0)),
        ],
        out_specs=[pl.BlockSpec((None, D_MODEL // 2, D_MODEL), lambda l, k: (l, 0, k)),
                   pl.BlockSpec((None, shift, D_MODEL), lambda l, k: (l, 0, 0)),
                   pl.BlockSpec((None, D_MODEL, D_MODEL), lambda l, k: (l, 0, 0))],
        out_shape=[jax.ShapeDtypeStruct((depth, D_MODEL // 2, N_BLOCKS * D_MODEL), U32),
                   jax.ShapeDtypeStruct((depth, shift, D_MODEL), F32),
                   jax.ShapeDtypeStruct((depth, D_MODEL, D_MODEL), BF16)],
        compiler_params=pltpu.CompilerParams(
            dimension_semantics=("arbitrary", "arbitrary"), vmem_limit_bytes=VMEM_LIMIT_BYTES),
        name="pack_w",
    )(w_t, w_t)


def _pack_rows_kernel(x_ref, o_ref):
    o_ref[...] = pltpu.bitcast(x_ref[...].astype(BF16), U32)


def _pack_rows(arr):
    depth, r, c = arr.shape
    return pl.pallas_call(
        _pack_rows_kernel,
        grid=(depth,),
        in_specs=[pl.BlockSpec((None, r, c), lambda l: (l, 0, 0))],
        out_specs=pl.BlockSpec((None, r // 2, c), lambda l: (l, 0, 0)),
        out_shape=jax.ShapeDtypeStruct((depth, r // 2, c), U32),
        compiler_params=pltpu.CompilerParams(
            dimension_semantics=("arbitrary",), vmem_limit_bytes=VMEM_LIMIT_BYTES),
        name="pack_rows",
    )(arr)


def _pack_params(w_in, b_in, mlstm_norm_g, gmlp_ln_g, gmlp_ln_b, gmlp_ws, gmlp_bs, lru_conv_w, lru_conv_b,
                 lru_wa, lru_ba, lru_wx, lru_bx, lru_lambda, w_proj_a, w_proj_b, w_proj_c, w_out, ln_g, ln_b,
                 seq_s):
    depth = w_in.shape[0]
    d = D_MODEL
    o_i = 5 * d
    o_f = o_i + N_HEADS
    o_rest = o_f + N_HEADS

    def row(a):
        return a[:, None, :]

    def pad_lanes(a, n):
        return jnp.pad(a, [(0, 0)] * (a.ndim - 1) + [(0, n - a.shape[-1])])

    def drop_gate_cols(a):
        return jnp.concatenate([a[..., :o_i], a[..., o_rest:]], axis=-1)

    w_all, w_gate_t, w_kt = _pack_w(w_in)
    w_gate = jnp.swapaxes(w_gate_t, 1, 2)
    wi, wf = w_gate[..., :N_HEADS], w_gate[..., N_HEADS:2 * N_HEADS]
    bi, bf = b_in[..., o_i:o_f], b_in[..., o_f:o_rest]
    pad_rows = [(0, 0), (0, V7X_SUBLANES - N_HEADS), (0, 0)]
    lw = {
        "w_all": w_all,
        "b_all": row(drop_gate_cols(b_in)),
        "w_kt": w_kt,
        "b_kt": b_in[:, BLK_K * d:(BLK_K + 1) * d, None],
        "w_gc": _pack_rows(jnp.concatenate([pad_lanes(wi, V7X_LANES), pad_lanes(wf, V7X_LANES)], axis=-1)),
        "b_gc": row(jnp.concatenate([pad_lanes(bi, V7X_LANES), pad_lanes(bf, V7X_LANES)], axis=-1)),
        "w_gr": jnp.concatenate([jnp.pad(jnp.swapaxes(wi, 1, 2), pad_rows),
                                 jnp.pad(jnp.swapaxes(wf, 1, 2), pad_rows)], axis=1).astype(BF16),
        "b_gr": jnp.concatenate([jnp.pad(bi[:, :, None], pad_rows),
                                 jnp.pad(bf[:, :, None], pad_rows)], axis=1),
        "norm_g": row(mlstm_norm_g),
        "gmlp_ln_g": row(gmlp_ln_g), "gmlp_ln_b": row(gmlp_ln_b),
        "gmlp_ws": gmlp_ws,
        "gmlp_bs_t": jnp.swapaxes(gmlp_bs, 1, 2),
        "ws_small": gmlp_ws[:, :, :seq_s, :seq_s].reshape(depth, -1),
        "bs_small": gmlp_bs[:, :, :seq_s].reshape(depth, -1),
        "lru_conv_w": lru_conv_w, "lru_conv_b": row(lru_conv_b),
        "w_ax": _pack_rows(jnp.concatenate([lru_wa, lru_wx], axis=-1).reshape(depth, d, 2 * LRU_BLK)),
        "lru_ba": row(lru_ba), "lru_bx": row(lru_bx), "lru_lambda": row(lru_lambda),
        "w_pa": _pack_rows(w_proj_a), "w_pb": _pack_rows(w_proj_b), "w_pc": _pack_rows(w_proj_c),
        "w_out": _pack_rows(w_out),
        "ln_g": row(ln_g), "ln_b": row(ln_b),
    }
    return lw


def _prompt_layer(x, lw, layer, alpha):
    merged, c, n, m = _mlstm_prompt(x, lw, layer)
    x_new, conv, h = _mix_prompt(x, merged, lw, layer, alpha)
    return x_new, c, n[:, :N_HEADS], m[:, 0, :N_HEADS], conv, h[:, 0]


def _sample_layer(x_bm, c_all, c_new_all, n0, m0, conv_buf, h0, lw, layer, alpha):
    nb, seq, d = x_bm.shape
    x_rows = x_bm.reshape(nb * seq, d)
    x_tm = jnp.swapaxes(x_bm, 0, 1)
    proj_a = _proj(x_rows, lw["w_all"], lw["b_all"], layer, 0, BLK_U * d, d)
    gates = _proj(x_rows, lw["w_gc"], lw["b_gc"], layer, 0, 2 * V7X_LANES, 2 * V7X_LANES)

    n0_rows = jnp.repeat(n0.reshape(nb, d), seq, axis=0)
    m0_rows = jnp.repeat(jnp.pad(m0, ((0, 0), (0, V7X_LANES - N_HEADS))), seq, axis=0)
    ya, c_new_all, n_rows, m_rows = _mlstm_sample(
        proj_a, gates, c_all, c_new_all, n0_rows, m0_rows, lw, layer, seq)
    n_new = n_rows.reshape(nb, seq, N_HEADS, D_HEAD)[:, seq - 1]
    m_new = m_rows.reshape(nb, seq, V7X_LANES)[:, seq - 1, :N_HEADS]

    ya_tm = jnp.swapaxes(ya.reshape(nb, seq, d), 0, 1)
    buf_tm = jnp.swapaxes(conv_buf, 0, 1)
    x_new_tm, vn_tm, conv_tm, h_new = _rest_sample(x_tm, ya_tm, buf_tm, h0, lw, layer, alpha)
    return (jnp.swapaxes(x_new_tm, 0, 1), c_new_all, n_new, m_new, jnp.swapaxes(conv_tm, 0, 1), h_new,
            jnp.swapaxes(vn_tm, 0, 1))


def kernel(x_prompt, x_sample, state_mlstm_c, state_mlstm_n, state_mlstm_m, state_lru_conv, state_lru_h,
           w_in, b_in, mlstm_norm_g, gmlp_ln_g, gmlp_ln_b, gmlp_ws, gmlp_bs, lru_conv_w, lru_conv_b,
           lru_wa, lru_ba, lru_wx, lru_bx, lru_lambda, w_proj_a, w_proj_b, w_proj_c, w_out, ln_g, ln_b):
    depth = w_in.shape[0]
    alpha = float((2 * depth) ** 0.25)
    seq_s = x_sample.shape[1]
    assert V7X_SUBLANES % seq_s == 0 and x_prompt.shape[1] % TT_MLSTM == 0
    lw = _pack_params(w_in, b_in, mlstm_norm_g, gmlp_ln_g, gmlp_ln_b, gmlp_ws, gmlp_bs, lru_conv_w,
                      lru_conv_b, lru_wa, lru_ba, lru_wx, lru_bx, lru_lambda, w_proj_a, w_proj_b, w_proj_c,
                      w_out, ln_g, ln_b, seq_s)
    xp, xs = x_prompt, x_sample
    outs_p = [[] for _ in range(5)]
    outs_s = [[] for _ in range(5)]
    c_new_all = None
    for layer in range(depth):
        xp, *st = _prompt_layer(xp, lw, layer, alpha)
        for acc, val in zip(outs_p, st):
            acc.append(val)
        xs, c_new_all, *st = _sample_layer(
            xs, state_mlstm_c, c_new_all, state_mlstm_n[layer], state_mlstm_m[layer],
            state_lru_conv[layer], state_lru_h[layer], lw, layer, alpha)
        for acc, val in zip(outs_s, st):
            acc.append(val)
    stacked_p = [jnp.stack(v) for v in outs_p]
    stacked_s = [jnp.stack(v) for v in outs_s]
    return (xp, xs, *stacked_p, c_new_all, *stacked_s)
```

```python
import functools

import jax
import jax.numpy as jnp
from jax import lax
from jax.experimental import pallas as pl
from jax.experimental.pallas import tpu as pltpu

D_MODEL = 1024
N_HEADS = 4
D_HEAD = D_MODEL // N_HEADS
CHUNK = 128
N_GROUPS = 4
D_GROUP = D_MODEL // N_GROUPS
LRU_BLOCKS = 8
LRU_BLK = D_MODEL // LRU_BLOCKS
CONV_W = 4
LRU_C = 8.0
LN_EPS = 1e-5
K_SCALE = D_HEAD ** -0.5
LOG2_E = 1.4426950408889634
V7X_LANES = 128
V7X_SUBLANES = 8
VMEM_LIMIT_BYTES = 56 * 1024 * 1024

TT_MLSTM = 512
TT_MIX = 512
SB_MIX = 512
BB_MLSTM_SAMPLE = 8
BB_REST_SAMPLE = 64

BLK_Q, BLK_K, BLK_V, BLK_O, BLK_ZA = 0, 1, 2, 3, 4
BLK_U, BLK_VB, BLK_ZB = 5, 6, 7
BLK_XC, BLK_ZC = 8, 9
BLK_GA, BLK_GB, BLK_GC = 10, 11, 12
N_BLOCKS = 13

BF16 = jnp.bfloat16
U32 = jnp.uint32
F32 = jnp.float32

_NT = (((1,), (1,)), ((), ()))
_TN = (((0,), (0,)), ((), ()))


def _dot(a, b):
    return jnp.dot(a, b, preferred_element_type=F32)


def _dot_nt(a, b):
    return lax.dot_general(a, b, _NT, preferred_element_type=F32)


def _dot_tn(a, b):
    return lax.dot_general(a, b, _TN, preferred_element_type=F32)


def _unpack(ref, rows=slice(None), cols=slice(None)):
    return pltpu.bitcast(ref[rows, cols], BF16)


def _layer_norm(x, g, b):
    mu = jnp.mean(x, axis=-1, keepdims=True)
    xc = x - mu
    var = jnp.mean(xc * xc, axis=-1, keepdims=True)
    return xc * lax.rsqrt(var + LN_EPS) * g + b


def _silu(x):
    return x * jax.nn.sigmoid(x)


def _scan_axis(x, axis, op, fill):
    n = x.shape[axis]
    idx = lax.broadcasted_iota(jnp.int32, x.shape, axis)
    s = 1
    while s < n:
        x = op(x, jnp.where(idx >= s, pltpu.roll(x, s, axis), fill))
        s *= 2
    return x


def _at_layer(arr, layer):
    shape = arr.shape[1:]
    nd = len(shape)
    return pl.BlockSpec((None,) + shape, lambda *_: (layer,) + (0,) * nd, pipeline_mode=pl.Buffered(1))


def _w_block(layer, blk):
    return pl.BlockSpec((None, D_MODEL // 2, D_MODEL), lambda *_: (layer, 0, blk), pipeline_mode=pl.Buffered(1))


def _b_block(layer, blk):
    return pl.BlockSpec((None, 1, D_MODEL), lambda *_: (layer, 0, blk), pipeline_mode=pl.Buffered(1))


def _mlstm_prompt_kernel(x_ref, wq_ref, wv_ref, wo_ref, wz_ref, wg_ref, bq_ref, bv_ref, bo_ref, bz_ref, bg_ref,
                         wkt_ref, bkt_ref, wgc_ref, bgc_ref, wgr_ref, bgr_ref, ng_ref, wp_ref,
                         o_ref, c_out, n_out, m_out,
                         c_s, n_s, mc_s, mr_s, q_s, kt_s, v_s, o_s, h_s,
                         acol_s, inter_s, em_s, decay_s, grow_s, wrow_s, *, tt):
    j = pl.program_id(1)
    nch = tt // CHUNK

    @pl.when(j == 0)
    def _():
        c_s[...] = jnp.zeros_like(c_s)
        n_s[...] = jnp.zeros_like(n_s)
        mc_s[...] = jnp.zeros_like(mc_s)
        mr_s[...] = jnp.zeros_like(mr_s)

    xb = x_ref[...].astype(BF16)

    def col(w_ref, b_ref):
        return _dot(xb, _unpack(w_ref)) + b_ref[...]

    def project_q():
        q_s[...] = col(wq_ref, bq_ref).astype(BF16)

    def project_k():
        kt_s[...] = (_dot_nt(wkt_ref[...], xb) + bkt_ref[...]) * K_SCALE

    def project_v():
        v_s[...] = col(wv_ref, bv_ref).astype(BF16)

    def project_o():
        o_s[...] = jax.nn.sigmoid(col(wo_ref, bo_ref))

    projections = [project_q, project_k, project_v, project_o]

    gcol = _dot(xb, _unpack(wgc_ref)) + bgc_ref[...]
    it_c = gcol[:, :V7X_LANES]
    lf_c = jax.nn.log_sigmoid(gcol[:, V7X_LANES:])
    grow = _dot_nt(wgr_ref[...], xb) + bgr_ref[...]
    it_r = grow[:V7X_SUBLANES]
    lf_r = jax.nn.log_sigmoid(grow[V7X_SUBLANES:])

    m0c = mc_s[0:1, :]
    m0r = mr_s[...]
    for c in range(nch):
        if c < len(projections):
            projections[c]()
        sl = slice(c * CHUNK, (c + 1) * CHUNK)
        bcol = _scan_axis(lf_c[sl], 0, jnp.add, 0.0)
        gcl = it_c[sl] - bcol
        mcol = bcol + jnp.maximum(m0c, _scan_axis(gcl, 0, jnp.maximum, -jnp.inf))
        blast = bcol[CHUNK - 1:CHUNK]
        mlast = mcol[CHUNK - 1:CHUNK]
        acol = bcol - mcol
        inter = jnp.exp(bcol + m0c - mcol)
        em = jnp.exp(-mcol)
        decay_s[c:c + 1, :] = jnp.exp(blast + m0c - mlast)
        for h in range(N_HEADS):
            acol_s[h, sl, :] = jnp.broadcast_to(acol[:, h:h + 1], (CHUNK, V7X_LANES))
            inter_s[h, sl, :] = jnp.broadcast_to(inter[:, h:h + 1], (CHUNK, V7X_LANES))
            em_s[h, sl, :] = jnp.broadcast_to(em[:, h:h + 1], (CHUNK, V7X_LANES))
        brow = _scan_axis(lf_r[:, sl], 1, jnp.add, 0.0)
        grw = it_r[:, sl] - brow
        mrow = brow + jnp.maximum(m0r, _scan_axis(grw, 1, jnp.maximum, -jnp.inf))
        mlast_r = mrow[:, CHUNK - 1:CHUNK]
        grow_s[:, sl] = grw
        wrow_s[:, sl] = jnp.exp(brow[:, CHUNK - 1:CHUNK] + grw - mlast_r)
        m0c = mlast
        m0r = jnp.broadcast_to(mlast_r, m0r.shape)
    for project in projections[nch:]:
        project()
    mc_s[...] = jnp.broadcast_to(m0c, mc_s.shape)
    mr_s[...] = m0r

    tri = (lax.broadcasted_iota(jnp.int32, (CHUNK, CHUNK), 0)
           >= lax.broadcasted_iota(jnp.int32, (CHUNK, CHUNK), 1))
    ones_rows = jnp.ones((V7X_SUBLANES, CHUNK), BF16)

    for c in range(nch):
        rows = slice(c * CHUNK, (c + 1) * CHUNK)
        decay_c = decay_s[c:c + 1, :]
        for h in range(N_HEADS):
            hc = slice(h * D_HEAD, (h + 1) * D_HEAD)
            qc = q_s[rows, hc]
            kt = kt_s[hc, rows]
            vc = v_s[rows, hc]
            inter = inter_s[h, rows, :]
            decay = decay_c[:, h:h + 1]

            dmat = jnp.exp(jnp.where(tri, acol_s[h, rows, :] + grow_s[h:h + 1, rows], -jnp.inf))
            s = _dot(qc, kt.astype(BF16)) * dmat
            c0 = c_s[h]
            n0 = n_s[h:h + 1, :]
            num = (jnp.concatenate([inter, inter], axis=1) * _dot(qc, c0.astype(BF16))
                   + _dot(s.astype(BF16), vc))
            qn = qc.astype(F32) * n0
            den = jnp.sum(inter * (qn[:, :V7X_LANES] + qn[:, V7X_LANES:]) + s, axis=1, keepdims=True)
            rden = 1.0 / jnp.maximum(jnp.abs(den), em_s[h, rows, :])
            h_s[rows, hc] = num * jnp.concatenate([rden, rden], axis=1)

            kwt = (kt * wrow_s[h:h + 1, rows]).astype(BF16)
            c_s[h] = decay * c0 + _dot(kwt, vc)
            n_s[h:h + 1, :] = decay * n0 + _dot_nt(ones_rows, kwt)[0:1, :]

    hh = o_s[...] * h_s[...]
    parts = []
    for h in range(N_HEADS):
        hc = slice(h * D_HEAD, (h + 1) * D_HEAD)
        hd = hh[:, hc]
        mu = jnp.mean(hd, axis=-1, keepdims=True)
        hd = hd - mu
        var = jnp.mean(hd * hd, axis=-1, keepdims=True)
        parts.append(hd * lax.rsqrt(var + LN_EPS) * ng_ref[:, hc])
    y = (jnp.concatenate(parts, axis=1) * _silu(col(wz_ref, bz_ref))).astype(BF16)
    o_ref[...] = jax.nn.sigmoid(col(wg_ref, bg_ref)) * _dot(y, _unpack(wp_ref))

    @pl.when(j == pl.num_programs(1) - 1)
    def _():
        c_out[...] = c_s[...]
        n_out[...] = n_s[...]
        m_out[...] = mc_s[...]


def _mlstm_prompt(x, lw, layer):
    bsz, t, _ = x.shape
    tt = TT_MLSTM
    kern = functools.partial(_mlstm_prompt_kernel, tt=tt)
    tile = pl.BlockSpec((None, tt, D_MODEL), lambda b, j: (b, j, 0))
    blocks = [BLK_Q, BLK_V, BLK_O, BLK_ZA, BLK_GA]
    names = ["w_kt", "b_kt", "w_gc", "b_gc", "w_gr", "b_gr", "norm_g", "w_pa"]
    head_lanes = (N_HEADS, tt, V7X_LANES)
    return pl.pallas_call(
        kern,
        grid=(bsz, t // tt),
        in_specs=([tile] + [_w_block(layer, blk) for blk in blocks] + [_b_block(layer, blk) for blk in blocks]
                  + [_at_layer(lw[n], layer) for n in names]),
        out_specs=[
            tile,
            pl.BlockSpec((None, N_HEADS, D_HEAD, D_HEAD), lambda b, j: (b, 0, 0, 0)),
            pl.BlockSpec((None, V7X_SUBLANES, D_HEAD), lambda b, j: (b, 0, 0)),
            pl.BlockSpec((None, V7X_SUBLANES, V7X_LANES), lambda b, j: (b, 0, 0)),
        ],
        out_shape=[
            jax.ShapeDtypeStruct((bsz, t, D_MODEL), F32),
            jax.ShapeDtypeStruct((bsz, N_HEADS, D_HEAD, D_HEAD), F32),
            jax.ShapeDtypeStruct((bsz, V7X_SUBLANES, D_HEAD), F32),
            jax.ShapeDtypeStruct((bsz, V7X_SUBLANES, V7X_LANES), F32),
        ],
        scratch_shapes=[
            pltpu.VMEM((N_HEADS, D_HEAD, D_HEAD), F32),
            pltpu.VMEM((V7X_SUBLANES, D_HEAD), F32),
            pltpu.VMEM((V7X_SUBLANES, V7X_LANES), F32),
            pltpu.VMEM((V7X_SUBLANES, V7X_LANES), F32),
            pltpu.VMEM((tt, D_MODEL), BF16),
            pltpu.VMEM((D_MODEL, tt), F32),
            pltpu.VMEM((tt, D_MODEL), BF16),
            pltpu.VMEM((tt, D_MODEL), F32),
            pltpu.VMEM((tt, D_MODEL), F32),
            pltpu.VMEM(head_lanes, F32),
            pltpu.VMEM(head_lanes, F32),
            pltpu.VMEM(head_lanes, F32),
            pltpu.VMEM((V7X_SUBLANES, V7X_LANES), F32),
            pltpu.VMEM((V7X_SUBLANES, tt), F32),
            pltpu.VMEM((V7X_SUBLANES, tt), F32),
        ],
        compiler_params=pltpu.CompilerParams(
            dimension_semantics=("arbitrary", "arbitrary"), vmem_limit_bytes=VMEM_LIMIT_BYTES),
        name="mlstm_prompt",
    )(x, *([lw["w_all"]] * len(blocks)), *([lw["b_all"]] * len(blocks)), *[lw[n] for n in names])


def _lru_gates(xc, wax_ref, ba, bx, lam):
    xcb = xc.astype(BF16)
    half = LRU_BLK // 2
    pre = [_dot(xcb[:, n * LRU_BLK:(n + 1) * LRU_BLK], _unpack(wax_ref, slice(n * half, (n + 1) * half)))
           for n in range(LRU_BLOCKS)]
    r = jax.nn.sigmoid(jnp.concatenate([p[:, :LRU_BLK] for p in pre], axis=1) + ba)
    i = jax.nn.sigmoid(jnp.concatenate([p[:, LRU_BLK:] for p in pre], axis=1) + bx)
    a = jnp.exp2(r * (LRU_C * LOG2_E * jax.nn.log_sigmoid(lam)))
    v = 1.0 - a * a
    mult = jnp.where(v > 0.0, v * lax.rsqrt(v), 0.0)
    return a, mult, i


def _mix_kernel(x_ref, ma_ref,
                wu_ref, wvb_ref, wzb_ref, wgb_ref, wxc_ref, wzc_ref, wgc_ref,
                bu_ref, bvb_ref, bzb_ref, bgb_ref, bxc_ref, bzc_ref, bgc_ref,
                lng1_ref, lnb1_ref, ws_ref, bs_ref, wpb_ref,
                cw_ref, cb_ref, wax_ref, ba_ref, bx_ref, lam_ref, wpc_ref,
                wo_ref, lng_ref, lnb_ref,
                o_ref, conv_out, h_out,
                ucar_s, vn_s, h_s, hcar_s, *, tt, sb, alpha):
    j = pl.program_id(1)
    nch = tt // CHUNK
    nsb = tt // sb

    @pl.when(j == 0)
    def _():
        ucar_s[...] = jnp.broadcast_to(cb_ref[...], ucar_s.shape)
        hcar_s[...] = jnp.zeros_like(hcar_s)

    x = x_ref[...]
    xb = x.astype(BF16)
    state = {}

    def col(w_ref, b_ref, cols=slice(None), rows=slice(None)):
        return _dot(xb[rows], _unpack(w_ref, cols=cols)) + b_ref[:, cols]

    ngs = sb // V7X_SUBLANES
    sub = lax.broadcasted_iota(jnp.int32, (ngs, V7X_SUBLANES, D_MODEL), 1)
    conv_tail = [ucar_s[jj] for jj in range(CONV_W - 1)]

    def lru_input(k):
        state["x_c", k] = col(wxc_ref, bxc_ref, rows=slice(k * sb, (k + 1) * sb))

    def lru_block(k, hprev):
        r0 = k * sb
        x3 = state["x_c", k].reshape(ngs, V7X_SUBLANES, D_MODEL)
        u = cb_ref[...] + cw_ref[0:1, :] * x3
        for jj in range(1, CONV_W):
            rot = pltpu.roll(u, 1, 1)
            prev = jnp.concatenate([conv_tail[jj - 1][None], rot[:-1]], axis=0)
            conv_tail[jj - 1] = rot[ngs - 1]
            u = jnp.where(sub == 0, prev, rot) + cw_ref[jj:jj + 1, :] * x3
        xc = u.reshape(sb, D_MODEL)
        a, mult, i = _lru_gates(xc, wax_ref, ba_ref[...], bx_ref[...], lam_ref[...])
        if k == 0:
            row = lax.broadcasted_iota(jnp.int32, (V7X_SUBLANES, 1), 0)
            top = jnp.where(jnp.logical_and(row == 0, j == 0), 1.0, mult[:V7X_SUBLANES])
            mult = jnp.concatenate([top, mult[V7X_SUBLANES:]], axis=0)
        bterm = mult * i * xc
        a3 = a.reshape(ngs, V7X_SUBLANES, D_MODEL)
        b3 = bterm.reshape(ngs, V7X_SUBLANES, D_MODEL)
        s = 1
        while s < V7X_SUBLANES:
            keep = sub >= s
            a_sh = jnp.where(keep, pltpu.roll(a3, s, 1), 1.0)
            b_sh = jnp.where(keep, pltpu.roll(b3, s, 1), 0.0)
            b3 = a3 * b_sh + b3
            a3 = a3 * a_sh
            s *= 2
        for g in range(ngs):
            hg = a3[g] * hprev + b3[g]
            h_s[r0 + g * V7X_SUBLANES:r0 + (g + 1) * V7X_SUBLANES, :] = hg
            hprev = jnp.broadcast_to(hg[V7X_SUBLANES - 1:V7X_SUBLANES, :], hg.shape)
        return hprev

    tri = (lax.broadcasted_iota(jnp.int32, (CHUNK, CHUNK), 0)
           >= lax.broadcasted_iota(jnp.int32, (CHUNK, CHUNK), 1))

    def gmlp_norm():
        vn_s[...] = _layer_norm(col(wvb_ref, bvb_ref), lng1_ref[...], lnb1_ref[...])

    def gmlp_group(g):
        wm = jnp.where(tri, ws_ref[g], 0.0).astype(BF16)
        bias = bs_ref[:, g:g + 1]
        cols = slice(g * D_GROUP, (g + 1) * D_GROUP)
        mixed = jnp.concatenate(
            [_dot(wm, vn_s[c * CHUNK:(c + 1) * CHUNK, cols].astype(BF16)) + bias for c in range(nch)],
            axis=0)
        state["y_b", g] = (col(wu_ref, bu_ref, cols) * mixed * _silu(col(wzb_ref, bzb_ref, cols))).astype(BF16)

    def lru_silu_z():
        state["sz_c"] = _silu(col(wzc_ref, bzc_ref))

    def gmlp_gate():
        y_b = jnp.concatenate([state["y_b", g] for g in range(N_GROUPS)], axis=1)
        state["merged"] = (ma_ref[...]
                           + jax.nn.sigmoid(col(wgb_ref, bgb_ref)) * _dot(y_b, _unpack(wpb_ref)))

    def lru_gate():
        state["g_c"] = jax.nn.sigmoid(col(wgc_ref, bgc_ref))

    steps = ([gmlp_norm, lru_silu_z] + [functools.partial(gmlp_group, g) for g in range(N_GROUPS)]
             + [gmlp_gate, lru_gate])
    bounds = [round(k * len(steps) / nsb) for k in range(nsb + 1)]
    hprev = hcar_s[...]
    lru_input(0)
    for k in range(nsb):
        if k + 1 < nsb:
            lru_input(k + 1)
        for step in steps[bounds[k]:bounds[k + 1]]:
            step()
        hprev = lru_block(k, hprev)
    hcar_s[...] = hprev
    for jj in range(CONV_W - 1):
        ucar_s[jj] = conv_tail[jj]

    @pl.when(j == pl.num_programs(1) - 1)
    def _():
        conv_out[...] = state["x_c", nsb - 1][sb - (CONV_W - 1):, :]
        h_out[...] = hprev

    y_c = (h_s[...] * state["sz_c"]).astype(BF16)
    merged = state["merged"] + state["g_c"] * _dot(y_c, _unpack(wpc_ref))
    out = _dot(merged.astype(BF16), _unpack(wo_ref))
    o_ref[...] = _layer_norm(alpha * x + out, lng_ref[...], lnb_ref[...])


def _mix_prompt(x, merged, lw, layer, alpha):
    bsz, t, _ = x.shape
    tt = TT_MIX
    kern = functools.partial(_mix_kernel, tt=tt, sb=SB_MIX, alpha=alpha)
    tile = pl.BlockSpec((None, tt, D_MODEL), lambda b, j: (b, j, 0))
    blocks = [BLK_U, BLK_VB, BLK_ZB, BLK_GB, BLK_XC, BLK_ZC, BLK_GC]
    names = ["gmlp_ln_g", "gmlp_ln_b", "gmlp_ws", "gmlp_bs_t", "w_pb",
             "lru_conv_w", "lru_conv_b", "w_ax", "lru_ba", "lru_bx", "lru_lambda", "w_pc",
             "w_out", "ln_g", "ln_b"]
    return pl.pallas_call(
        kern,
        grid=(bsz, t // tt),
        in_specs=([tile, tile] + [_w_block(layer, blk) for blk in blocks]
                  + [_b_block(layer, blk) for blk in blocks] + [_at_layer(lw[n], layer) for n in names]),
        out_specs=[
            tile,
            pl.BlockSpec((None, CONV_W - 1, D_MODEL), lambda b, j: (b, 0, 0)),
            pl.BlockSpec((None, V7X_SUBLANES, D_MODEL), lambda b, j: (b, 0, 0)),
        ],
        out_shape=[
            jax.ShapeDtypeStruct((bsz, t, D_MODEL), F32),
            jax.ShapeDtypeStruct((bsz, CONV_W - 1, D_MODEL), F32),
            jax.ShapeDtypeStruct((bsz, V7X_SUBLANES, D_MODEL), F32),
        ],
        scratch_shapes=[
            pltpu.VMEM((CONV_W - 1, V7X_SUBLANES, D_MODEL), F32),
            pltpu.VMEM((tt, D_MODEL), F32),
            pltpu.VMEM((tt, D_MODEL), F32),
            pltpu.VMEM((V7X_SUBLANES, D_MODEL), F32),
        ],
        compiler_params=pltpu.CompilerParams(
            dimension_semantics=("arbitrary", "arbitrary"), vmem_limit_bytes=VMEM_LIMIT_BYTES),
        name="mix_prompt",
    )(x, merged, *([lw["w_all"]] * len(blocks)), *([lw["b_all"]] * len(blocks)), *[lw[n] for n in names])


def _proj_kernel(x_ref, w_ref, b_ref, o_ref):
    o_ref[...] = _dot(x_ref[...].astype(BF16), _unpack(w_ref)) + b_ref[...]


def _proj(x, w, b, layer, col0, ncols, bn):
    rows = x.shape[0]
    off = col0 // bn
    return pl.pallas_call(
        _proj_kernel,
        grid=(ncols // bn,),
        in_specs=[
            pl.BlockSpec((rows, D_MODEL), lambda n: (0, 0)),
            pl.BlockSpec((None, D_MODEL // 2, bn), lambda n: (layer, 0, off + n)),
            pl.BlockSpec((None, 1, bn), lambda n: (layer, 0, off + n)),
        ],
        out_specs=pl.BlockSpec((rows, bn), lambda n: (0, n)),
        out_shape=jax.ShapeDtypeStruct((rows, ncols), F32),
        compiler_params=pltpu.CompilerParams(
            dimension_semantics=("arbitrary",), vmem_limit_bytes=VMEM_LIMIT_BYTES),
        name="proj_sample",
    )(x, w, b)


def _mlstm_sample_kernel(p_ref, g_ref, c_ref, n_ref, m_ref, ng_ref, *rest, bb, seq):
    y_ref, c_out, n_out, m_out, qc_s = rest[-5:]
    rows = bb * seq
    t_idx = lax.broadcasted_iota(jnp.int32, (rows, 1), 0) % seq

    def down(x, d, fill=0.0):
        if d == 0:
            return x
        return jnp.where(t_idx >= d, pltpu.roll(x, d, 0), fill)

    def from_last(x):
        out = x
        for d in range(1, seq):
            out = jnp.where(t_idx == seq - 1 - d, pltpu.roll(x, rows - d, 0), out)
        return out

    it_all = g_ref[:, :V7X_LANES]
    lf_all = jax.nn.log_sigmoid(g_ref[:, V7X_LANES:])
    b_all = lf_all
    for d in range(1, seq):
        b_all = b_all + down(lf_all, d)
    g_all = it_all - b_all
    mx_all = g_all
    for d in range(1, seq):
        mx_all = jnp.maximum(mx_all, down(g_all, d, -jnp.inf))
    m0_all = m_ref[...]
    m_all = b_all + jnp.maximum(m0_all, mx_all)
    mlast_all = from_last(m_all)
    blast_all = from_last(b_all)
    inter_all = jnp.exp(b_all + m0_all - m_all)
    em_all = jnp.exp(-m_all)
    w_all = jnp.exp(blast_all + g_all - mlast_all)
    decay_all = jnp.exp(blast_all + m0_all - mlast_all)
    a_all = b_all - m_all
    m_out[...] = mlast_all

    def head_cols(blk, h):
        return slice(blk * D_MODEL + h * D_HEAD, blk * D_MODEL + (h + 1) * D_HEAD)

    for h in range(N_HEADS):
        hc = slice(h * D_HEAD, (h + 1) * D_HEAD)
        q = p_ref[:, head_cols(BLK_Q, h)]
        k = p_ref[:, head_cols(BLK_K, h)] * K_SCALE
        v = p_ref[:, head_cols(BLK_V, h)]
        lane = slice(h, h + 1)
        a_col, g_col = a_all[:, lane], g_all[:, lane]
        inter, em = inter_all[:, lane], em_all[:, lane]
        wcol, decay = w_all[:, lane], decay_all[:, lane]

        kw = k * wcol
        per_slab = V7X_SUBLANES // seq
        grp = lax.broadcasted_iota(jnp.int32, (V7X_SUBLANES, 1), 0) // seq
        for slab in range(rows // V7X_SUBLANES):
            s0 = slab * V7X_SUBLANES
            q8 = q[s0:s0 + V7X_SUBLANES, :].astype(BF16)
            kw8 = kw[s0:s0 + V7X_SUBLANES, :]
            v8 = v[s0:s0 + V7X_SUBLANES, :].astype(BF16)
            qc8 = jnp.zeros((V7X_SUBLANES, D_HEAD), F32)
            for e in range(per_slab):
                b = slab * per_slab + e
                c0 = c_ref[b, h]
                qc8 = jnp.where(grp == e, _dot(q8, c0.astype(BF16)), qc8)
                kw_e = jnp.where(grp == e, kw8, 0.0).astype(BF16)
                c_out[b, h] = decay[b * seq:b * seq + 1, :] * c0 + _dot_tn(kw_e, v8)
            qc_s[s0:s0 + V7X_SUBLANES, :] = qc8

        n0 = n_ref[:, hc]
        num = inter * qc_s[...]
        den = inter * jnp.sum(q * n0, axis=1, keepdims=True)
        for d in range(seq):
            s_d = jnp.sum(q * down(k, d), axis=1, keepdims=True) * jnp.exp(a_col + down(g_col, d))
            s_d = jnp.where(t_idx >= d, s_d, 0.0)
            num = num + s_d * down(v, d)
            den = den + s_d
        hh = num / jnp.maximum(jnp.abs(den), em)

        ksum = kw
        for d in range(1, seq):
            ksum = ksum + down(kw, d)
        n_out[:, hc] = decay * n0 + ksum

        hh = jax.nn.sigmoid(p_ref[:, head_cols(BLK_O, h)]) * hh
        mu = jnp.mean(hh, axis=-1, keepdims=True)
        hcn = hh - mu
        var = jnp.mean(hcn * hcn, axis=-1, keepdims=True)
        hn = hcn * lax.rsqrt(var + LN_EPS) * ng_ref[:, hc]
        y_ref[:, hc] = hn * _silu(p_ref[:, head_cols(BLK_ZA, h)])


def _mlstm_sample(proj_a, gates, c_all, c_new_all, n0_rows, m0_rows, lw, layer, seq):
    nb = c_all.shape[1]
    bb = BB_MLSTM_SAMPLE
    rows = bb * seq
    kern = functools.partial(_mlstm_sample_kernel, bb=bb, seq=seq)
    c_spec = pl.BlockSpec((None, bb, N_HEADS, D_HEAD, D_HEAD), lambda i: (layer, i, 0, 0, 0))
    in_specs = [
        pl.BlockSpec((rows, 5 * D_MODEL), lambda i: (i, 0)),
        pl.BlockSpec((rows, 2 * V7X_LANES), lambda i: (i, 0)),
        c_spec,
        pl.BlockSpec((rows, D_MODEL), lambda i: (i, 0)),
        pl.BlockSpec((rows, V7X_LANES), lambda i: (i, 0)),
        _at_layer(lw["norm_g"], layer),
    ]
    args = [proj_a, gates, c_all, n0_rows, m0_rows, lw["norm_g"]]
    aliases = {}
    if c_new_all is not None:
        in_specs.append(pl.BlockSpec(memory_space=pl.ANY))
        args.append(c_new_all)
        aliases = {len(args) - 1: 1}
    return pl.pallas_call(
        kern,
        grid=(nb // bb,),
        in_specs=in_specs,
        out_specs=[
            pl.BlockSpec((rows, D_MODEL), lambda i: (i, 0)),
            c_spec,
            pl.BlockSpec((rows, D_MODEL), lambda i: (i, 0)),
            pl.BlockSpec((rows, V7X_LANES), lambda i: (i, 0)),
        ],
        out_shape=[
            jax.ShapeDtypeStruct((nb * seq, D_MODEL), F32),
            jax.ShapeDtypeStruct(c_all.shape, F32),
            jax.ShapeDtypeStruct((nb * seq, D_MODEL), F32),
            jax.ShapeDtypeStruct((nb * seq, V7X_LANES), F32),
        ],
        scratch_shapes=[pltpu.VMEM((rows, D_HEAD), F32)],
        input_output_aliases=aliases,
        compiler_params=pltpu.CompilerParams(
            dimension_semantics=("arbitrary",), vmem_limit_bytes=VMEM_LIMIT_BYTES),
        name="mlstm_sample",
    )(*args)


def _rest_sample_kernel(x_ref, ya_ref, buf_ref, h0_ref, wsm_ref, bsm_ref, *rest, seq, bbs, alpha):
    n_blk = N_BLOCKS - BLK_U
    w_refs, b_refs = rest[:n_blk], rest[n_blk:2 * n_blk]
    (lng1_ref, lnb1_ref, cw_ref, cb_ref, wax_ref, ba_ref, bx_ref, lam_ref,
     wpa_ref, wpb_ref, wpc_ref, wo_ref, lng_ref, lnb_ref,
     o_ref, vn_out, conv_out, h_out) = rest[2 * n_blk:]
    rows = seq * bbs
    x = x_ref[...].reshape(rows, D_MODEL)
    xb = x.astype(BF16)

    def flat(blk):
        return _dot(xb, _unpack(w_refs[blk - BLK_U])) + b_refs[blk - BLK_U][...]

    vn = _layer_norm(flat(BLK_VB), lng1_ref[...], lnb1_ref[...])
    vn_out[...] = vn.reshape(seq, bbs, D_MODEL)
    mixed_t = []
    for t in range(seq):
        parts = []
        for g in range(N_GROUPS):
            cols = slice(g * D_GROUP, (g + 1) * D_GROUP)
            acc = jnp.full((bbs, D_GROUP), bsm_ref[g * seq + t], F32)
            for s in range(t + 1):
                acc = acc + wsm_ref[(g * seq + t) * seq + s] * vn[s * bbs:(s + 1) * bbs, cols]
            parts.append(acc)
        mixed_t.append(jnp.concatenate(parts, axis=1))
    mixed = jnp.concatenate(mixed_t, axis=0)
    yb = (flat(BLK_U) * mixed * _silu(flat(BLK_ZB))).astype(BF16)

    xc_raw = flat(BLK_XC)
    xp = [buf_ref[t] for t in range(CONV_W - 1)] + [xc_raw[t * bbs:(t + 1) * bbs] for t in range(seq)]
    for t in range(CONV_W - 1):
        conv_out[t] = xp[seq + t]
    xc_t = []
    for t in range(seq):
        acc = cb_ref[...] + cw_ref[0:1, :] * xp[t]
        for jj in range(1, CONV_W):
            acc = acc + cw_ref[jj:jj + 1, :] * xp[t + jj]
        xc_t.append(acc)
    xc = jnp.concatenate(xc_t, axis=0)
    a, mult, i = _lru_gates(xc, wax_ref, ba_ref[...], bx_ref[...], lam_ref[...])
    bterm = mult * i * xc
    h = h0_ref[...]
    hs = []
    for t in range(seq):
        h = a[t * bbs:(t + 1) * bbs] * h + bterm[t * bbs:(t + 1) * bbs]
        hs.append(h)
    h_out[...] = h
    yc = (jnp.concatenate(hs, axis=0) * _silu(flat(BLK_ZC))).astype(BF16)

    ya = ya_ref[...].reshape(rows, D_MODEL).astype(BF16)
    merged = (jax.nn.sigmoid(flat(BLK_GA)) * _dot(ya, _unpack(wpa_ref))
              + jax.nn.sigmoid(flat(BLK_GB)) * _dot(yb, _unpack(wpb_ref))
              + jax.nn.sigmoid(flat(BLK_GC)) * _dot(yc, _unpack(wpc_ref)))
    out = _dot(merged.astype(BF16), _unpack(wo_ref))
    o_ref[...] = _layer_norm(alpha * x + out, lng_ref[...], lnb_ref[...]).reshape(seq, bbs, D_MODEL)


def _rest_sample(x_tm, ya_tm, buf_tm, h0, lw, layer, alpha):
    seq, nb, _ = x_tm.shape
    bbs = min(BB_REST_SAMPLE, nb)
    kern = functools.partial(_rest_sample_kernel, seq=seq, bbs=bbs, alpha=alpha)

    def tm(n, lead=seq):
        return pl.BlockSpec((lead, bbs, n), lambda i: (0, i, 0))

    smem = pl.BlockSpec(memory_space=pltpu.SMEM)
    blocks = list(range(BLK_U, N_BLOCKS))
    names = ["gmlp_ln_g", "gmlp_ln_b", "lru_conv_w", "lru_conv_b", "w_ax", "lru_ba", "lru_bx", "lru_lambda",
             "w_pa", "w_pb", "w_pc", "w_out", "ln_g", "ln_b"]
    return pl.pallas_call(
        kern,
        grid=(nb // bbs,),
        in_specs=([tm(D_MODEL), tm(D_MODEL),
                   tm(D_MODEL, CONV_W - 1), pl.BlockSpec((bbs, D_MODEL), lambda i: (i, 0)),
                   smem, smem] + [_w_block(layer, blk) for blk in blocks]
                  + [_b_block(layer, blk) for blk in blocks] + [_at_layer(lw[n], layer) for n in names]),
        out_specs=[tm(D_MODEL), tm(D_MODEL), tm(D_MODEL, CONV_W - 1),
                   pl.BlockSpec((bbs, D_MODEL), lambda i: (i, 0))],
        out_shape=[
            jax.ShapeDtypeStruct((seq, nb, D_MODEL), F32),
            jax.ShapeDtypeStruct((seq, nb, D_MODEL), F32),
            jax.ShapeDtypeStruct((CONV_W - 1, nb, D_MODEL), F32),
            jax.ShapeDtypeStruct((nb, D_MODEL), F32),
        ],
        compiler_params=pltpu.CompilerParams(
            dimension_semantics=("arbitrary",), vmem_limit_bytes=VMEM_LIMIT_BYTES),
        name="rest_sample",
    )(x_tm, ya_tm, buf_tm, h0, lw["ws_small"][layer], lw["bs_small"][layer],
      *([lw["w_all"]] * len(blocks)), *([lw["b_all"]] * len(blocks)), *[lw[n] for n in names])


def _pack_w_kernel(lo_ref, hi_ref, o_ref, gate_ref, kt_ref, *, shift):
    blk = pl.program_id(1)

    @pl.when(blk < BLK_U)
    def _():
        o_ref[...] = pltpu.bitcast(lo_ref[...].T.astype(BF16), U32)

    @pl.when(blk == BLK_U - 1)
    def _():
        gate_ref[...] = hi_ref[...]

    @pl.when(blk == BLK_K)
    def _():
        kt_ref[...] = lo_ref[...].astype(BF16)

    @pl.when(blk >= BLK_U)
    def _():
        rows = jnp.concatenate([lo_ref[shift:, :], hi_ref[...]], axis=0)
        o_ref[...] = pltpu.bitcast(rows.T.astype(BF16), U32)


def _pack_w(w_in):
    depth = w_in.shape[0]
    shift = 2 * N_HEADS
    assert shift == V7X_SUBLANES
    w_t = jnp.swapaxes(w_in, 1, 2)
    kern = functools.partial(_pack_w_kernel, shift=shift)
    return pl.pallas_call(
        kern,
        grid=(depth, N_BLOCKS),
        in_specs=[
            pl.BlockSpec((None, D_MODEL, D_MODEL), lambda l, k: (l, k, 0)),
            pl.BlockSpec((None, shift, D_MODEL), lambda l, k: (l, (k + 1) * (D_MODEL // shift), 0)),
        ],
        out_specs=[pl.BlockSpec((None, D_MODEL // 2, D_MODEL), lambda l, k: (l, 0, k)),
                   pl.BlockSpec((None, shift, D_MODEL), lambda l, k: (l, 0, 0)),
                   pl.BlockSpec((None, D_MODEL, D_MODEL), lambda l, k: (l, 0, 0))],
        out_shape=[jax.ShapeDtypeStruct((depth, D_MODEL // 2, N_BLOCKS * D_MODEL), U32),
                   jax.ShapeDtypeStruct((depth, shift, D_MODEL), F32),
                   jax.ShapeDtypeStruct((depth, D_MODEL, D_MODEL), BF16)],
        compiler_params=pltpu.CompilerParams(
            dimension_semantics=("arbitrary", "arbitrary"), vmem_limit_bytes=VMEM_LIMIT_BYTES),
        name="pack_w",
    )(w_t, w_t)


def _pack_rows_kernel(x_ref, o_ref):
    o_ref[...] = pltpu.bitcast(x_ref[...].astype(BF16), U32)


def _pack_rows(arr):
    depth, r, c = arr.shape
    return pl.pallas_call(
        _pack_rows_kernel,
        grid=(depth,),
        in_specs=[pl.BlockSpec((None, r, c), lambda l: (l, 0, 0))],
        out_specs=pl.BlockSpec((None, r // 2, c), lambda l: (l, 0, 0)),
        out_shape=jax.ShapeDtypeStruct((depth, r // 2, c), U32),
        compiler_params=pltpu.CompilerParams(
            dimension_semantics=("arbitrary",), vmem_limit_bytes=VMEM_LIMIT_BYTES),
        name="pack_rows",
    )(arr)


def _pack_params(w_in, b_in, mlstm_norm_g, gmlp_ln_g, gmlp_ln_b, gmlp_ws, gmlp_bs, lru_conv_w, lru_conv_b,
                 lru_wa, lru_ba, lru_wx, lru_bx, lru_lambda, w_proj_a, w_proj_b, w_proj_c, w_out, ln_g, ln_b,
                 seq_s):
    depth = w_in.shape[0]
    d = D_MODEL
    o_i = 5 * d
    o_f = o_i + N_HEADS
    o_rest = o_f + N_HEADS

    def row(a):
        return a[:, None, :]

    def pad_lanes(a, n):
        return jnp.pad(a, [(0, 0)] * (a.ndim - 1) + [(0, n - a.shape[-1])])

    def drop_gate_cols(a):
        return jnp.concatenate([a[..., :o_i], a[..., o_rest:]], axis=-1)

    w_all, w_gate_t, w_kt = _pack_w(w_in)
    w_gate = jnp.swapaxes(w_gate_t, 1, 2)
    wi, wf = w_gate[..., :N_HEADS], w_gate[..., N_HEADS:2 * N_HEADS]
    bi, bf = b_in[..., o_i:o_f], b_in[..., o_f:o_rest]
    pad_rows = [(0, 0), (0, V7X_SUBLANES - N_HEADS), (0, 0)]
    lw = {
        "w_all": w_all,
        "b_all": row(drop_gate_cols(b_in)),
        "w_kt": w_kt,
        "b_kt": b_in[:, BLK_K * d:(BLK_K + 1) * d, None],
        "w_gc": _pack_rows(jnp.concatenate([pad_lanes(wi, V7X_LANES), pad_lanes(wf, V7X_LANES)], axis=-1)),
        "b_gc": row(jnp.concatenate([pad_lanes(bi, V7X_LANES), pad_lanes(bf, V7X_LANES)], axis=-1)),
        "w_gr": jnp.concatenate([jnp.pad(jnp.swapaxes(wi, 1, 2), pad_rows),
                                 jnp.pad(jnp.swapaxes(wf, 1, 2), pad_rows)], axis=1).astype(BF16),
        "b_gr": jnp.concatenate([jnp.pad(bi[:, :, None], pad_rows),
                                 jnp.pad(bf[:, :, None], pad_rows)], axis=1),
        "norm_g": row(mlstm_norm_g),
        "gmlp_ln_g": row(gmlp_ln_g), "gmlp_ln_b": row(gmlp_ln_b),
        "gmlp_ws": gmlp_ws,
        "gmlp_bs_t": jnp.swapaxes(gmlp_bs, 1, 2),
        "ws_small": gmlp_ws[:, :, :seq_s, :seq_s].reshape(depth, -1),
        "bs_small": gmlp_bs[:, :, :seq_s].reshape(depth, -1),
        "lru_conv_w": lru_conv_w, "lru_conv_b": row(lru_conv_b),
        "w_ax": _pack_rows(jnp.concatenate([lru_wa, lru_wx], axis=-1).reshape(depth, d, 2 * LRU_BLK)),
        "lru_ba": row(lru_ba), "lru_bx": row(lru_bx), "lru_lambda": row(lru_lambda),
        "w_pa": _pack_rows(w_proj_a), "w_pb": _pack_rows(w_proj_b), "w_pc": _pack_rows(w_proj_c),
        "w_out": _pack_rows(w_out),
        "ln_g": row(ln_g), "ln_b": row(ln_b),
    }
    return lw


def _prompt_layer(x, lw, layer, alpha):
    merged, c, n, m = _mlstm_prompt(x, lw, layer)
    x_new, conv, h = _mix_prompt(x, merged, lw, layer, alpha)
    return x_new, c, n[:, :N_HEADS], m[:, 0, :N_HEADS], conv, h[:, 0]


def _sample_layer(x_bm, c_all, c_new_all, n0, m0, conv_buf, h0, lw, layer, alpha):
    nb, seq, d = x_bm.shape
    x_rows = x_bm.reshape(nb * seq, d)
    x_tm = jnp.swapaxes(x_bm, 0, 1)
    proj_a = _proj(x_rows, lw["w_all"], lw["b_all"], layer, 0, BLK_U * d, d)
    gates = _proj(x_rows, lw["w_gc"], lw["b_gc"], layer, 0, 2 * V7X_LANES, 2 * V7X_LANES)

    n0_rows = jnp.repeat(n0.reshape(nb, d), seq, axis=0)
    m0_rows = jnp.repeat(jnp.pad(m0, ((0, 0), (0, V7X_LANES - N_HEADS))), seq, axis=0)
    ya, c_new_all, n_rows, m_rows = _mlstm_sample(
        proj_a, gates, c_all, c_new_all, n0_rows, m0_rows, lw, layer, seq)
    n_new = n_rows.reshape(nb, seq, N_HEADS, D_HEAD)[:, seq - 1]
    m_new = m_rows.reshape(nb, seq, V7X_LANES)[:, seq - 1, :N_HEADS]

    ya_tm = jnp.swapaxes(ya.reshape(nb, seq, d), 0, 1)
    buf_tm = jnp.swapaxes(conv_buf, 0, 1)
    x_new_tm, vn_tm, conv_tm, h_new = _rest_sample(x_tm, ya_tm, buf_tm, h0, lw, layer, alpha)
    return (jnp.swapaxes(x_new_tm, 0, 1), c_new_all, n_new, m_new, jnp.swapaxes(conv_tm, 0, 1), h_new,
            jnp.swapaxes(vn_tm, 0, 1))


def kernel(x_prompt, x_sample, state_mlstm_c, state_mlstm_n, state_mlstm_m, state_lru_conv, state_lru_h,
           w_in, b_in, mlstm_norm_g, gmlp_ln_g, gmlp_ln_b, gmlp_ws, gmlp_bs, lru_conv_w, lru_conv_b,
           lru_wa, lru_ba, lru_wx, lru_bx, lru_lambda, w_proj_a, w_proj_b, w_proj_c, w_out, ln_g, ln_b):
    depth = w_in.shape[0]
    alpha = float((2 * depth) ** 0.25)
    seq_s = x_sample.shape[1]
    assert V7X_SUBLANES % seq_s == 0 and x_prompt.shape[1] % TT_MLSTM == 0
    lw = _pack_params(w_in, b_in, mlstm_norm_g, gmlp_ln_g, gmlp_ln_b, gmlp_ws, gmlp_bs, lru_conv_w,
                      lru_conv_b, lru_wa, lru_ba, lru_wx, lru_bx, lru_lambda, w_proj_a, w_proj_b, w_proj_c,
                      w_out, ln_g, ln_b, seq_s)
    xp, xs = x_prompt, x_sample
    outs_p = [[] for _ in range(5)]
    outs_s = [[] for _ in range(5)]
    c_new_all = None
    for layer in range(depth):
        xp, *st = _prompt_layer(xp, lw, layer, alpha)
        for acc, val in zip(outs_p, st):
            acc.append(val)
        xs, c_new_all, *st = _sample_layer(
            xs, state_mlstm_c, c_new_all, state_mlstm_n[layer], state_mlstm_m[layer],
            state_lru_conv[layer], state_lru_h[layer], lw, layer, alpha)
        for acc, val in zip(outs_s, st):
            acc.append(val)
    stacked_p = [jnp.stack(v) for v in outs_p]
    stacked_s = [jnp.stack(v) for v in outs_s]
    return (xp, xs, *stacked_p, c_new_all, *stacked_s)
```

```python
import functools

import jax
import jax.numpy as jnp
from jax import lax
from jax.experimental import pallas as pl
from jax.experimental.pallas import tpu as pltpu

D_MODEL = 1024
N_HEADS = 4
D_HEAD = D_MODEL // N_HEADS
CHUNK = 128
N_GROUPS = 4
D_GROUP = D_MODEL // N_GROUPS
LRU_BLOCKS = 8
LRU_BLK = D_MODEL // LRU_BLOCKS
CONV_W = 4
LRU_C = 8.0
LN_EPS = 1e-5
K_SCALE = D_HEAD ** -0.5
LOG2_E = 1.4426950408889634
V7X_LANES = 128
V7X_SUBLANES = 8
VMEM_LIMIT_BYTES = 56 * 1024 * 1024

TT_MLSTM = 512
TT_MIX = 512
SB_MIX = 512
LRU_LANE_SPLIT = 2
BB_MLSTM_SAMPLE = 8
BB_REST_SAMPLE = 64

BLK_Q, BLK_K, BLK_V, BLK_O, BLK_ZA = 0, 1, 2, 3, 4
BLK_U, BLK_VB, BLK_ZB = 5, 6, 7
BLK_XC, BLK_ZC = 8, 9
BLK_GA, BLK_GB, BLK_GC = 10, 11, 12
N_BLOCKS = 13

BF16 = jnp.bfloat16
U32 = jnp.uint32
F32 = jnp.float32

_NT = (((1,), (1,)), ((), ()))
_TN = (((0,), (0,)), ((), ()))


def _dot(a, b):
    return jnp.dot(a, b, preferred_element_type=F32)


def _dot_nt(a, b):
    return lax.dot_general(a, b, _NT, preferred_element_type=F32)


def _dot_tn(a, b):
    return lax.dot_general(a, b, _TN, preferred_element_type=F32)


def _unpack(ref, rows=slice(None), cols=slice(None)):
    return pltpu.bitcast(ref[rows, cols], BF16)


def _layer_norm(x, g, b):
    mu = jnp.mean(x, axis=-1, keepdims=True)
    xc = x - mu
    var = jnp.mean(xc * xc, axis=-1, keepdims=True)
    return xc * lax.rsqrt(var + LN_EPS) * g + b


def _silu(x):
    return x * jax.nn.sigmoid(x)


def _scan_axis(x, axis, op, fill):
    n = x.shape[axis]
    idx = lax.broadcasted_iota(jnp.int32, x.shape, axis)
    s = 1
    while s < n:
        x = op(x, jnp.where(idx >= s, pltpu.roll(x, s, axis), fill))
        s *= 2
    return x


def _at_layer(arr, layer):
    shape = arr.shape[1:]
    nd = len(shape)
    return pl.BlockSpec((None,) + shape, lambda *_: (layer,) + (0,) * nd, pipeline_mode=pl.Buffered(1))


def _w_block(layer, blk):
    return pl.BlockSpec((None, D_MODEL // 2, D_MODEL), lambda *_: (layer, 0, blk), pipeline_mode=pl.Buffered(1))


def _b_block(layer, blk):
    return pl.BlockSpec((None, 1, D_MODEL), lambda *_: (layer, 0, blk), pipeline_mode=pl.Buffered(1))


def _mlstm_prompt_kernel(x_ref, wq_ref, wv_ref, wo_ref, wz_ref, wg_ref, bq_ref, bv_ref, bo_ref, bz_ref, bg_ref,
                         wkt_ref, bkt_ref, wgc_ref, bgc_ref, wgr_ref, bgr_ref, ng_ref, wp_ref,
                         o_ref, c_out, n_out, m_out,
                         c_s, n_s, mc_s, mr_s, q_s, kt_s, v_s, o_s, h_s,
                         acol_s, inter_s, em_s, decay_s, grow_s, wrow_s, *, tt):
    j = pl.program_id(1)
    nch = tt // CHUNK

    @pl.when(j == 0)
    def _():
        c_s[...] = jnp.zeros_like(c_s)
        n_s[...] = jnp.zeros_like(n_s)
        mc_s[...] = jnp.zeros_like(mc_s)
        mr_s[...] = jnp.zeros_like(mr_s)

    xb = x_ref[...].astype(BF16)

    def col(w_ref, b_ref):
        return _dot(xb, _unpack(w_ref)) + b_ref[...]

    def project_q():
        q_s[...] = col(wq_ref, bq_ref).astype(BF16)

    def project_k():
        kt_s[...] = (_dot_nt(wkt_ref[...], xb) + bkt_ref[...]) * K_SCALE

    def project_v():
        v_s[...] = col(wv_ref, bv_ref).astype(BF16)

    def project_o():
        o_s[...] = jax.nn.sigmoid(col(wo_ref, bo_ref))

    projections = [project_q, project_k, project_v, project_o]

    gcol = _dot(xb, _unpack(wgc_ref)) + bgc_ref[...]
    it_c = gcol[:, :V7X_LANES]
    lf_c = jax.nn.log_sigmoid(gcol[:, V7X_LANES:])
    grow = _dot_nt(wgr_ref[...], xb) + bgr_ref[...]
    it_r = grow[:V7X_SUBLANES]
    lf_r = jax.nn.log_sigmoid(grow[V7X_SUBLANES:])

    m0c = mc_s[0:1, :]
    m0r = mr_s[...]
    for c in range(nch):
        if c < len(projections):
            projections[c]()
        sl = slice(c * CHUNK, (c + 1) * CHUNK)
        bcol = _scan_axis(lf_c[sl], 0, jnp.add, 0.0)
        gcl = it_c[sl] - bcol
        mcol = bcol + jnp.maximum(m0c, _scan_axis(gcl, 0, jnp.maximum, -jnp.inf))
        blast = bcol[CHUNK - 1:CHUNK]
        mlast = mcol[CHUNK - 1:CHUNK]
        acol = bcol - mcol
        inter = jnp.exp(bcol + m0c - mcol)
        em = jnp.exp(-mcol)
        decay_s[c:c + 1, :] = jnp.exp(blast + m0c - mlast)
        for h in range(N_HEADS):
            acol_s[h, sl, :] = jnp.broadcast_to(acol[:, h:h + 1], (CHUNK, V7X_LANES))
            inter_s[h, sl, :] = jnp.broadcast_to(inter[:, h:h + 1], (CHUNK, V7X_LANES))
            em_s[h, sl, :] = jnp.broadcast_to(em[:, h:h + 1], (CHUNK, V7X_LANES))
        brow = _scan_axis(lf_r[:, sl], 1, jnp.add, 0.0)
        grw = it_r[:, sl] - brow
        mrow = brow + jnp.maximum(m0r, _scan_axis(grw, 1, jnp.maximum, -jnp.inf))
        mlast_r = mrow[:, CHUNK - 1:CHUNK]
        grow_s[:, sl] = grw
        wrow_s[:, sl] = jnp.exp(brow[:, CHUNK - 1:CHUNK] + grw - mlast_r)
        m0c = mlast
        m0r = jnp.broadcast_to(mlast_r, m0r.shape)
    for project in projections[nch:]:
        project()
    mc_s[...] = jnp.broadcast_to(m0c, mc_s.shape)
    mr_s[...] = m0r

    tri = (lax.broadcasted_iota(jnp.int32, (CHUNK, CHUNK), 0)
           >= lax.broadcasted_iota(jnp.int32, (CHUNK, CHUNK), 1))
    ones_rows = jnp.ones((V7X_SUBLANES, CHUNK), BF16)

    for c in range(nch):
        rows = slice(c * CHUNK, (c + 1) * CHUNK)
        decay_c = decay_s[c:c + 1, :]
        for h in range(N_HEADS):
            hc = slice(h * D_HEAD, (h + 1) * D_HEAD)
            qc = q_s[rows, hc]
            kt = kt_s[hc, rows]
            vc = v_s[rows, hc]
            inter = inter_s[h, rows, :]
            decay = decay_c[:, h:h + 1]

            dmat = jnp.exp(jnp.where(tri, acol_s[h, rows, :] + grow_s[h:h + 1, rows], -jnp.inf))
            s = _dot(qc, kt.astype(BF16)) * dmat
            c0 = c_s[h]
            n0 = n_s[h:h + 1, :]
            num = (jnp.concatenate([inter, inter], axis=1) * _dot(qc, c0.astype(BF16))
                   + _dot(s.astype(BF16), vc))
            qn = qc.astype(F32) * n0
            den = jnp.sum(inter * (qn[:, :V7X_LANES] + qn[:, V7X_LANES:]) + s, axis=1, keepdims=True)
            rden = 1.0 / jnp.maximum(jnp.abs(den), em_s[h, rows, :])
            h_s[rows, hc] = num * jnp.concatenate([rden, rden], axis=1)

            kwt = (kt * wrow_s[h:h + 1, rows]).astype(BF16)
            c_s[h] = decay * c0 + _dot(kwt, vc)
            n_s[h:h + 1, :] = decay * n0 + _dot_nt(ones_rows, kwt)[0:1, :]

    hh = o_s[...] * h_s[...]
    parts = []
    for h in range(N_HEADS):
        hc = slice(h * D_HEAD, (h + 1) * D_HEAD)
        hd = hh[:, hc]
        mu = jnp.mean(hd, axis=-1, keepdims=True)
        hd = hd - mu
        var = jnp.mean(hd * hd, axis=-1, keepdims=True)
        parts.append(hd * lax.rsqrt(var + LN_EPS) * ng_ref[:, hc])
    y = (jnp.concatenate(parts, axis=1) * _silu(col(wz_ref, bz_ref))).astype(BF16)
    o_ref[...] = jax.nn.sigmoid(col(wg_ref, bg_ref)) * _dot(y, _unpack(wp_ref))

    @pl.when(j == pl.num_programs(1) - 1)
    def _():
        c_out[...] = c_s[...]
        n_out[...] = n_s[...]
        m_out[...] = mc_s[...]


def _mlstm_prompt(x, lw, layer):
    bsz, t, _ = x.shape
    tt = TT_MLSTM
    kern = functools.partial(_mlstm_prompt_kernel, tt=tt)
    tile = pl.BlockSpec((None, tt, D_MODEL), lambda b, j: (b, j, 0))
    blocks = [BLK_Q, BLK_V, BLK_O, BLK_ZA, BLK_GA]
    names = ["w_kt", "b_kt", "w_gc", "b_gc", "w_gr", "b_gr", "norm_g", "w_pa"]
    head_lanes = (N_HEADS, tt, V7X_LANES)
    return pl.pallas_call(
        kern,
        grid=(bsz, t // tt),
        in_specs=([tile] + [_w_block(layer, blk) for blk in blocks] + [_b_block(layer, blk) for blk in blocks]
                  + [_at_layer(lw[n], layer) for n in names]),
        out_specs=[
            tile,
            pl.BlockSpec((None, N_HEADS, D_HEAD, D_HEAD), lambda b, j: (b, 0, 0, 0)),
            pl.BlockSpec((None, V7X_SUBLANES, D_HEAD), lambda b, j: (b, 0, 0)),
            pl.BlockSpec((None, V7X_SUBLANES, V7X_LANES), lambda b, j: (b, 0, 0)),
        ],
        out_shape=[
            jax.ShapeDtypeStruct((bsz, t, D_MODEL), F32),
            jax.ShapeDtypeStruct((bsz, N_HEADS, D_HEAD, D_HEAD), F32),
            jax.ShapeDtypeStruct((bsz, V7X_SUBLANES, D_HEAD), F32),
            jax.ShapeDtypeStruct((bsz, V7X_SUBLANES, V7X_LANES), F32),
        ],
        scratch_shapes=[
            pltpu.VMEM((N_HEADS, D_HEAD, D_HEAD), F32),
            pltpu.VMEM((V7X_SUBLANES, D_HEAD), F32),
            pltpu.VMEM((V7X_SUBLANES, V7X_LANES), F32),
            pltpu.VMEM((V7X_SUBLANES, V7X_LANES), F32),
            pltpu.VMEM((tt, D_MODEL), BF16),
            pltpu.VMEM((D_MODEL, tt), F32),
            pltpu.VMEM((tt, D_MODEL), BF16),
            pltpu.VMEM((tt, D_MODEL), F32),
            pltpu.VMEM((tt, D_MODEL), F32),
            pltpu.VMEM(head_lanes, F32),
            pltpu.VMEM(head_lanes, F32),
            pltpu.VMEM(head_lanes, F32),
            pltpu.VMEM((V7X_SUBLANES, V7X_LANES), F32),
            pltpu.VMEM((V7X_SUBLANES, tt), F32),
            pltpu.VMEM((V7X_SUBLANES, tt), F32),
        ],
        compiler_params=pltpu.CompilerParams(
            dimension_semantics=("arbitrary", "arbitrary"), vmem_limit_bytes=VMEM_LIMIT_BYTES),
        name="mlstm_prompt",
    )(x, *([lw["w_all"]] * len(blocks)), *([lw["b_all"]] * len(blocks)), *[lw[n] for n in names])


def _lru_gates(xc, wax_ref, ba, bx, lam, blk0=0):
    xcb = xc.astype(BF16)
    half = LRU_BLK // 2
    pre = [_dot(xcb[:, n * LRU_BLK:(n + 1) * LRU_BLK],
                _unpack(wax_ref, slice((blk0 + n) * half, (blk0 + n + 1) * half)))
           for n in range(xc.shape[1] // LRU_BLK)]
    r = jax.nn.sigmoid(jnp.concatenate([p[:, :LRU_BLK] for p in pre], axis=1) + ba)
    i = jax.nn.sigmoid(jnp.concatenate([p[:, LRU_BLK:] for p in pre], axis=1) + bx)
    a = jnp.exp2(r * (LRU_C * LOG2_E * jax.nn.log_sigmoid(lam)))
    v = 1.0 - a * a
    mult = jnp.where(v > 0.0, v * lax.rsqrt(v), 0.0)
    return a, mult, i


def _mix_kernel(x_ref, ma_ref,
                wu_ref, wvb_ref, wzb_ref, wgb_ref, wxc_ref, wzc_ref, wgc_ref,
                bu_ref, bvb_ref, bzb_ref, bgb_ref, bxc_ref, bzc_ref, bgc_ref,
                lng1_ref, lnb1_ref, ws_ref, bs_ref, wpb_ref,
                cw_ref, cb_ref, wax_ref, ba_ref, bx_ref, lam_ref, wpc_ref,
                wo_ref, lng_ref, lnb_ref,
                o_ref, conv_out, h_out,
                ucar_s, vn_s, h_s, hcar_s, *, tt, sb, alpha):
    j = pl.program_id(1)
    nch = tt // CHUNK
    nsb = tt // sb

    @pl.when(j == 0)
    def _():
        ucar_s[...] = jnp.broadcast_to(cb_ref[...], ucar_s.shape)
        hcar_s[...] = jnp.zeros_like(hcar_s)

    x = x_ref[...]
    xb = x.astype(BF16)
    state = {}

    def col(w_ref, b_ref, cols=slice(None), rows=slice(None)):
        return _dot(xb[rows], _unpack(w_ref, cols=cols)) + b_ref[:, cols]

    ngs = sb // V7X_SUBLANES
    sub = lax.broadcasted_iota(jnp.int32, (ngs, V7X_SUBLANES, D_MODEL), 1)
    conv_tail = [ucar_s[jj] for jj in range(CONV_W - 1)]
    tail_parts = [[] for _ in range(CONV_W - 1)]

    def lru_input(k):
        state["x_c", k] = col(wxc_ref, bxc_ref, rows=slice(k * sb, (k + 1) * sb))

    def lru_block(k, hprev):
        r0 = k * sb
        wch = D_MODEL // LRU_LANE_SPLIT
        hnext = []
        for part in range(LRU_LANE_SPLIT):
            ch = slice(part * wch, (part + 1) * wch)
            subp = lax.broadcasted_iota(jnp.int32, (ngs, V7X_SUBLANES, wch), 1)
            x3 = state["x_c", k][:, ch].reshape(ngs, V7X_SUBLANES, wch)
            u = cb_ref[:, ch] + cw_ref[0:1, ch] * x3
            for jj in range(1, CONV_W):
                rot = pltpu.roll(u, 1, 1)
                prev = jnp.concatenate([conv_tail[jj - 1][:, ch][None], rot[:-1]], axis=0)
                tail_parts[jj - 1].append(rot[ngs - 1])
                u = jnp.where(subp == 0, prev, rot) + cw_ref[jj:jj + 1, ch] * x3
            xc = u.reshape(sb, wch)
            a, mult, i = _lru_gates(xc, wax_ref, ba_ref[:, ch], bx_ref[:, ch], lam_ref[:, ch],
                                    blk0=part * wch // LRU_BLK)
            if k == 0:
                row = lax.broadcasted_iota(jnp.int32, (V7X_SUBLANES, 1), 0)
                top = jnp.where(jnp.logical_and(row == 0, j == 0), 1.0, mult[:V7X_SUBLANES])
                mult = jnp.concatenate([top, mult[V7X_SUBLANES:]], axis=0)
            bterm = mult * i * xc
            a3 = a.reshape(ngs, V7X_SUBLANES, wch)
            b3 = bterm.reshape(ngs, V7X_SUBLANES, wch)
            s = 1
            while s < V7X_SUBLANES:
                keep = subp >= s
                a_sh = jnp.where(keep, pltpu.roll(a3, s, 1), 1.0)
                b_sh = jnp.where(keep, pltpu.roll(b3, s, 1), 0.0)
                b3 = a3 * b_sh + b3
                a3 = a3 * a_sh
                s *= 2
            hp = hprev[:, ch]
            for g in range(ngs):
                hg = a3[g] * hp + b3[g]
                h_s[r0 + g * V7X_SUBLANES:r0 + (g + 1) * V7X_SUBLANES, ch] = hg
                hp = jnp.broadcast_to(hg[V7X_SUBLANES - 1:V7X_SUBLANES, :], hg.shape)
            hnext.append(hp)
        for jj in range(CONV_W - 1):
            conv_tail[jj] = jnp.concatenate(tail_parts[jj], axis=1)
            tail_parts[jj].clear()
        return jnp.concatenate(hnext, axis=1)

    tri = (lax.broadcasted_iota(jnp.int32, (CHUNK, CHUNK), 0)
           >= lax.broadcasted_iota(jnp.int32, (CHUNK, CHUNK), 1))

    def gmlp_norm():
        vn_s[...] = _layer_norm(col(wvb_ref, bvb_ref), lng1_ref[...], lnb1_ref[...])

    def gmlp_group(g):
        wm = jnp.where(tri, ws_ref[g], 0.0).astype(BF16)
        bias = bs_ref[:, g:g + 1]
        cols = slice(g * D_GROUP, (g + 1) * D_GROUP)
        mixed = jnp.concatenate(
            [_dot(wm, vn_s[c * CHUNK:(c + 1) * CHUNK, cols].astype(BF16)) + bias for c in range(nch)],
            axis=0)
        state["y_b", g] = (col(wu_ref, bu_ref, cols) * mixed * _silu(col(wzb_ref, bzb_ref, cols))).astype(BF16)

    def lru_silu_z():
        state["sz_c"] = _silu(col(wzc_ref, bzc_ref))

    def gmlp_gate():
        y_b = jnp.concatenate([state["y_b", g] for g in range(N_GROUPS)], axis=1)
        state["merged"] = (ma_ref[...]
                           + jax.nn.sigmoid(col(wgb_ref, bgb_ref)) * _dot(y_b, _unpack(wpb_ref)))

    def lru_gate():
        state["g_c"] = jax.nn.sigmoid(col(wgc_ref, bgc_ref))

    steps = ([gmlp_norm, lru_silu_z] + [functools.partial(gmlp_group, g) for g in range(N_GROUPS)]
             + [gmlp_gate, lru_gate])
    bounds = [round(k * len(steps) / nsb) for k in range(nsb + 1)]
    hprev = hcar_s[...]
    lru_input(0)
    for k in range(nsb):
        if k + 1 < nsb:
            lru_input(k + 1)
        for step in steps[bounds[k]:bounds[k + 1]]:
            step()
        hprev = lru_block(k, hprev)
    hcar_s[...] = hprev
    for jj in range(CONV_W - 1):
        ucar_s[jj] = conv_tail[jj]

    @pl.when(j == pl.num_programs(1) - 1)
    def _():
        conv_out[...] = state["x_c", nsb - 1][sb - (CONV_W - 1):, :]
        h_out[...] = hprev

    y_c = (h_s[...] * state["sz_c"]).astype(BF16)
    merged = state["merged"] + state["g_c"] * _dot(y_c, _unpack(wpc_ref))
    out = _dot(merged.astype(BF16), _unpack(wo_ref))
    o_ref[...] = _layer_norm(alpha * x + out, lng_ref[...], lnb_ref[...])


def _mix_prompt(x, merged, lw, layer, alpha):
    bsz, t, _ = x.shape
    tt = TT_MIX
    kern = functools.partial(_mix_kernel, tt=tt, sb=SB_MIX, alpha=alpha)
    tile = pl.BlockSpec((None, tt, D_MODEL), lambda b, j: (b, j, 0))
    blocks = [BLK_U, BLK_VB, BLK_ZB, BLK_GB, BLK_XC, BLK_ZC, BLK_GC]
    names = ["gmlp_ln_g", "gmlp_ln_b", "gmlp_ws", "gmlp_bs_t", "w_pb",
             "lru_conv_w", "lru_conv_b", "w_ax", "lru_ba", "lru_bx", "lru_lambda", "w_pc",
             "w_out", "ln_g", "ln_b"]
    return pl.pallas_call(
        kern,
        grid=(bsz, t // tt),
        in_specs=([tile, tile] + [_w_block(layer, blk) for blk in blocks]
                  + [_b_block(layer, blk) for blk in blocks] + [_at_layer(lw[n], layer) for n in names]),
        out_specs=[
            tile,
            pl.BlockSpec((None, CONV_W - 1, D_MODEL), lambda b, j: (b, 0, 0)),
            pl.BlockSpec((None, V7X_SUBLANES, D_MODEL), lambda b, j: (b, 0, 0)),
        ],
        out_shape=[
            jax.ShapeDtypeStruct((bsz, t, D_MODEL), F32),
            jax.ShapeDtypeStruct((bsz, CONV_W - 1, D_MODEL), F32),
            jax.ShapeDtypeStruct((bsz, V7X_SUBLANES, D_MODEL), F32),
        ],
        scratch_shapes=[
            pltpu.VMEM((CONV_W - 1, V7X_SUBLANES, D_MODEL), F32),
            pltpu.VMEM((tt, D_MODEL), F32),
            pltpu.VMEM((tt, D_MODEL), F32),
            pltpu.VMEM((V7X_SUBLANES, D_MODEL), F32),
        ],
        compiler_params=pltpu.CompilerParams(
            dimension_semantics=("arbitrary", "arbitrary"), vmem_limit_bytes=VMEM_LIMIT_BYTES),
        name="mix_prompt",
    )(x, merged, *([lw["w_all"]] * len(blocks)), *([lw["b_all"]] * len(blocks)), *[lw[n] for n in names])


def _proj_kernel(x_ref, w_ref, b_ref, o_ref):
    o_ref[...] = _dot(x_ref[...].astype(BF16), _unpack(w_ref)) + b_ref[...]


def _proj(x, w, b, layer, col0, ncols, bn):
    rows = x.shape[0]
    off = col0 // bn
    return pl.pallas_call(
        _proj_kernel,
        grid=(ncols // bn,),
        in_specs=[
            pl.BlockSpec((rows, D_MODEL), lambda n: (0, 0)),
            pl.BlockSpec((None, D_MODEL // 2, bn), lambda n: (layer, 0, off + n)),
            pl.BlockSpec((None, 1, bn), lambda n: (layer, 0, off + n)),
        ],
        out_specs=pl.BlockSpec((rows, bn), lambda n: (0, n)),
        out_shape=jax.ShapeDtypeStruct((rows, ncols), F32),
        compiler_params=pltpu.CompilerParams(
            dimension_semantics=("arbitrary",), vmem_limit_bytes=VMEM_LIMIT_BYTES),
        name="proj_sample",
    )(x, w, b)


def _mlstm_sample_kernel(p_ref, g_ref, c_ref, n_ref, m_ref, ng_ref, *rest, bb, seq):
    y_ref, c_out, n_out, m_out, qc_s = rest[-5:]
    rows = bb * seq
    t_idx = lax.broadcasted_iota(jnp.int32, (rows, 1), 0) % seq

    def down(x, d, fill=0.0):
        if d == 0:
            return x
        return jnp.where(t_idx >= d, pltpu.roll(x, d, 0), fill)

    def from_last(x):
        out = x
        for d in range(1, seq):
            out = jnp.where(t_idx == seq - 1 - d, pltpu.roll(x, rows - d, 0), out)
        return out

    it_all = g_ref[:, :V7X_LANES]
    lf_all = jax.nn.log_sigmoid(g_ref[:, V7X_LANES:])
    b_all = lf_all
    for d in range(1, seq):
        b_all = b_all + down(lf_all, d)
    g_all = it_all - b_all
    mx_all = g_all
    for d in range(1, seq):
        mx_all = jnp.maximum(mx_all, down(g_all, d, -jnp.inf))
    m0_all = m_ref[...]
    m_all = b_all + jnp.maximum(m0_all, mx_all)
    mlast_all = from_last(m_all)
    blast_all = from_last(b_all)
    inter_all = jnp.exp(b_all + m0_all - m_all)
    em_all = jnp.exp(-m_all)
    w_all = jnp.exp(blast_all + g_all - mlast_all)
    decay_all = jnp.exp(blast_all + m0_all - mlast_all)
    a_all = b_all - m_all
    m_out[...] = mlast_all

    def head_cols(blk, h):
        return slice(blk * D_MODEL + h * D_HEAD, blk * D_MODEL + (h + 1) * D_HEAD)

    for h in range(N_HEADS):
        hc = slice(h * D_HEAD, (h + 1) * D_HEAD)
        q = p_ref[:, head_cols(BLK_Q, h)]
        k = p_ref[:, head_cols(BLK_K, h)] * K_SCALE
        v = p_ref[:, head_cols(BLK_V, h)]
        lane = slice(h, h + 1)
        a_col, g_col = a_all[:, lane], g_all[:, lane]
        inter, em = inter_all[:, lane], em_all[:, lane]
        wcol, decay = w_all[:, lane], decay_all[:, lane]

        kw = k * wcol
        per_slab = V7X_SUBLANES // seq
        grp = lax.broadcasted_iota(jnp.int32, (V7X_SUBLANES, 1), 0) // seq
        for slab in range(rows // V7X_SUBLANES):
            s0 = slab * V7X_SUBLANES
            q8 = q[s0:s0 + V7X_SUBLANES, :].astype(BF16)
            kw8 = kw[s0:s0 + V7X_SUBLANES, :]
            v8 = v[s0:s0 + V7X_SUBLANES, :].astype(BF16)
            qc8 = jnp.zeros((V7X_SUBLANES, D_HEAD), F32)
            for e in range(per_slab):
                b = slab * per_slab + e
                c0 = c_ref[b, h]
                qc8 = jnp.where(grp == e, _dot(q8, c0.astype(BF16)), qc8)
                kw_e = jnp.where(grp == e, kw8, 0.0).astype(BF16)
                c_out[b, h] = decay[b * seq:b * seq + 1, :] * c0 + _dot_tn(kw_e, v8)
            qc_s[s0:s0 + V7X_SUBLANES, :] = qc8

        n0 = n_ref[:, hc]
        num = inter * qc_s[...]
        den = inter * jnp.sum(q * n0, axis=1, keepdims=True)
        for d in range(seq):
            s_d = jnp.sum(q * down(k, d), axis=1, keepdims=True) * jnp.exp(a_col + down(g_col, d))
            s_d = jnp.where(t_idx >= d, s_d, 0.0)
            num = num + s_d * down(v, d)
            den = den + s_d
        hh = num / jnp.maximum(jnp.abs(den), em)

        ksum = kw
        for d in range(1, seq):
            ksum = ksum + down(kw, d)
        n_out[:, hc] = decay * n0 + ksum

        hh = jax.nn.sigmoid(p_ref[:, head_cols(BLK_O, h)]) * hh
        mu = jnp.mean(hh, axis=-1, keepdims=True)
        hcn = hh - mu
        var = jnp.mean(hcn * hcn, axis=-1, keepdims=True)
        hn = hcn * lax.rsqrt(var + LN_EPS) * ng_ref[:, hc]
        y_ref[:, hc] = hn * _silu(p_ref[:, head_cols(BLK_ZA, h)])


def _mlstm_sample(proj_a, gates, c_all, c_new_all, n0_rows, m0_rows, lw, layer, seq):
    nb = c_all.shape[1]
    bb = BB_MLSTM_SAMPLE
    rows = bb * seq
    kern = functools.partial(_mlstm_sample_kernel, bb=bb, seq=seq)
    c_spec = pl.BlockSpec((None, bb, N_HEADS, D_HEAD, D_HEAD), lambda i: (layer, i, 0, 0, 0))
    in_specs = [
        pl.BlockSpec((rows, 5 * D_MODEL), lambda i: (i, 0)),
        pl.BlockSpec((rows, 2 * V7X_LANES), lambda i: (i, 0)),
        c_spec,
        pl.BlockSpec((rows, D_MODEL), lambda i: (i, 0)),
        pl.BlockSpec((rows, V7X_LANES), lambda i: (i, 0)),
        _at_layer(lw["norm_g"], layer),
    ]
    args = [proj_a, gates, c_all, n0_rows, m0_rows, lw["norm_g"]]
    aliases = {}
    if c_new_all is not None:
        in_specs.append(pl.BlockSpec(memory_space=pl.ANY))
        args.append(c_new_all)
        aliases = {len(args) - 1: 1}
    return pl.pallas_call(
        kern,
        grid=(nb // bb,),
        in_specs=in_specs,
        out_specs=[
            pl.BlockSpec((rows, D_MODEL), lambda i: (i, 0)),
            c_spec,
            pl.BlockSpec((rows, D_MODEL), lambda i: (i, 0)),
            pl.BlockSpec((rows, V7X_LANES), lambda i: (i, 0)),
        ],
        out_shape=[
            jax.ShapeDtypeStruct((nb * seq, D_MODEL), F32),
            jax.ShapeDtypeStruct(c_all.shape, F32),
            jax.ShapeDtypeStruct((nb * seq, D_MODEL), F32),
            jax.ShapeDtypeStruct((nb * seq, V7X_LANES), F32),
        ],
        scratch_shapes=[pltpu.VMEM((rows, D_HEAD), F32)],
        input_output_aliases=aliases,
        compiler_params=pltpu.CompilerParams(
            dimension_semantics=("arbitrary",), vmem_limit_bytes=VMEM_LIMIT_BYTES),
        name="mlstm_sample",
    )(*args)


def _rest_sample_kernel(x_ref, ya_ref, buf_ref, h0_ref, wsm_ref, bsm_ref, *rest, seq, bbs, alpha):
    n_blk = N_BLOCKS - BLK_U
    w_refs, b_refs = rest[:n_blk], rest[n_blk:2 * n_blk]
    (lng1_ref, lnb1_ref, cw_ref, cb_ref, wax_ref, ba_ref, bx_ref, lam_ref,
     wpa_ref, wpb_ref, wpc_ref, wo_ref, lng_ref, lnb_ref,
     o_ref, vn_out, conv_out, h_out) = rest[2 * n_blk:]
    rows = seq * bbs
    x = x_ref[...].reshape(rows, D_MODEL)
    xb = x.astype(BF16)

    def flat(blk):
        return _dot(xb, _unpack(w_refs[blk - BLK_U])) + b_refs[blk - BLK_U][...]

    vn = _layer_norm(flat(BLK_VB), lng1_ref[...], lnb1_ref[...])
    vn_out[...] = vn.reshape(seq, bbs, D_MODEL)
    mixed_t = []
    for t in range(seq):
        parts = []
        for g in range(N_GROUPS):
            cols = slice(g * D_GROUP, (g + 1) * D_GROUP)
            acc = jnp.full((bbs, D_GROUP), bsm_ref[g * seq + t], F32)
            for s in range(t + 1):
                acc = acc + wsm_ref[(g * seq + t) * seq + s] * vn[s * bbs:(s + 1) * bbs, cols]
            parts.append(acc)
        mixed_t.append(jnp.concatenate(parts, axis=1))
    mixed = jnp.concatenate(mixed_t, axis=0)
    yb = (flat(BLK_U) * mixed * _silu(flat(BLK_ZB))).astype(BF16)

    xc_raw = flat(BLK_XC)
    xp = [buf_ref[t] for t in range(CONV_W - 1)] + [xc_raw[t * bbs:(t + 1) * bbs] for t in range(seq)]
    for t in range(CONV_W - 1):
        conv_out[t] = xp[seq + t]
    xc_t = []
    for t in range(seq):
        acc = cb_ref[...] + cw_ref[0:1, :] * xp[t]
        for jj in range(1, CONV_W):
            acc = acc + cw_ref[jj:jj + 1, :] * xp[t + jj]
        xc_t.append(acc)
    xc = jnp.concatenate(xc_t, axis=0)
    a, mult, i = _lru_gates(xc, wax_ref, ba_ref[...], bx_ref[...], lam_ref[...])
    bterm = mult * i * xc
    h = h0_ref[...]
    hs = []
    for t in range(seq):
        h = a[t * bbs:(t + 1) * bbs] * h + bterm[t * bbs:(t + 1) * bbs]
        hs.append(h)
    h_out[...] = h
    yc = (jnp.concatenate(hs, axis=0) * _silu(flat(BLK_ZC))).astype(BF16)

    ya = ya_ref[...].reshape(rows, D_MODEL).astype(BF16)
    merged = (jax.nn.sigmoid(flat(BLK_GA)) * _dot(ya, _unpack(wpa_ref))
              + jax.nn.sigmoid(flat(BLK_GB)) * _dot(yb, _unpack(wpb_ref))
              + jax.nn.sigmoid(flat(BLK_GC)) * _dot(yc, _unpack(wpc_ref)))
    out = _dot(merged.astype(BF16), _unpack(wo_ref))
    o_ref[...] = _layer_norm(alpha * x + out, lng_ref[...], lnb_ref[...]).reshape(seq, bbs, D_MODEL)


def _rest_sample(x_tm, ya_tm, buf_tm, h0, lw, layer, alpha):
    seq, nb, _ = x_tm.shape
    bbs = min(BB_REST_SAMPLE, nb)
    kern = functools.partial(_rest_sample_kernel, seq=seq, bbs=bbs, alpha=alpha)

    def tm(n, lead=seq):
        return pl.BlockSpec((lead, bbs, n), lambda i: (0, i, 0))

    smem = pl.BlockSpec(memory_space=pltpu.SMEM)
    blocks = list(range(BLK_U, N_BLOCKS))
    names = ["gmlp_ln_g", "gmlp_ln_b", "lru_conv_w", "lru_conv_b", "w_ax", "lru_ba", "lru_bx", "lru_lambda",
             "w_pa", "w_pb", "w_pc", "w_out", "ln_g", "ln_b"]
    return pl.pallas_call(
        kern,
        grid=(nb // bbs,),
        in_specs=([tm(D_MODEL), tm(D_MODEL),
                   tm(D_MODEL, CONV_W - 1), pl.BlockSpec((bbs, D_MODEL), lambda i: (i, 0)),
                   smem, smem] + [_w_block(layer, blk) for blk in blocks]
                  + [_b_block(layer, blk) for blk in blocks] + [_at_layer(lw[n], layer) for n in names]),
        out_specs=[tm(D_MODEL), tm(D_MODEL), tm(D_MODEL, CONV_W - 1),
                   pl.BlockSpec((bbs, D_MODEL), lambda i: (i, 0))],
        out_shape=[
            jax.ShapeDtypeStruct((seq, nb, D_MODEL), F32),
            jax.ShapeDtypeStruct((seq, nb, D_MODEL), F32),
            jax.ShapeDtypeStruct((CONV_W - 1, nb, D_MODEL), F32),
            jax.ShapeDtypeStruct((nb, D_MODEL), F32),
        ],
        compiler_params=pltpu.CompilerParams(
            dimension_semantics=("arbitrary",), vmem_limit_bytes=VMEM_LIMIT_BYTES),
        name="rest_sample",
    )(x_tm, ya_tm, buf_tm, h0, lw["ws_small"][layer], lw["bs_small"][layer],
      *([lw["w_all"]] * len(blocks)), *([lw["b_all"]] * len(blocks)), *[lw[n] for n in names])


def _pack_w_kernel(lo_ref, hi_ref, o_ref, gate_ref, kt_ref, *, shift):
    blk = pl.program_id(1)

    @pl.when(blk < BLK_U)
    def _():
        o_ref[...] = pltpu.bitcast(lo_ref[...].T.astype(BF16), U32)

    @pl.when(blk == BLK_U - 1)
    def _():
        gate_ref[...] = hi_ref[...]

    @pl.when(blk == BLK_K)
    def _():
        kt_ref[...] = lo_ref[...].astype(BF16)

    @pl.when(blk >= BLK_U)
    def _():
        rows = jnp.concatenate([lo_ref[shift:, :], hi_ref[...]], axis=0)
        o_ref[...] = pltpu.bitcast(rows.T.astype(BF16), U32)


def _pack_w(w_in):
    depth = w_in.shape[0]
    shift = 2 * N_HEADS
    assert shift == V7X_SUBLANES
    w_t = jnp.swapaxes(w_in, 1, 2)
    kern = functools.partial(_pack_w_kernel, shift=shift)
    return pl.pallas_call(
        kern,
        grid=(depth, N_BLOCKS),
        in_specs=[
            pl.BlockSpec((None, D_MODEL, D_MODEL), lambda l, k: (l, k, 0)),
            pl.BlockSpec((None, shift, D_MODEL), lambda l, k: (l, (k + 1) * (D_MODEL // shift), 0)),
        ],
        out_specs=[pl.BlockSpec((None, D_MODEL // 2, D_MODEL), lambda l, k: (l, 0, k)),
                   pl.BlockSpec((None, shift, D_MODEL), lambda l, k: (l, 0, 0)),
                   pl.BlockSpec((None, D_MODEL, D_MODEL), lambda l, k: (l, 0, 0))],
        out_shape=[jax.ShapeDtypeStruct((depth, D_MODEL // 2, N_BLOCKS * D_MODEL), U32),
                   jax.ShapeDtypeStruct((depth, shift, D_MODEL), F32),
                   jax.ShapeDtypeStruct((depth, D_MODEL, D_MODEL), BF16)],
        compiler_params=pltpu.CompilerParams(
            dimension_semantics=("arbitrary", "arbitrary"), vmem_limit_bytes=VMEM_LIMIT_BYTES),
        name="pack_w",
    )(w_t, w_t)


def _pack_rows_kernel(x_ref, o_ref):
    o_ref[...] = pltpu.bitcast(x_ref[...].astype(BF16), U32)


def _pack_rows(arr):
    depth, r, c = arr.shape
    return pl.pallas_call(
        _pack_rows_kernel,
        grid=(depth,),
        in_specs=[pl.BlockSpec((None, r, c), lambda l: (l, 0, 0))],
        out_specs=pl.BlockSpec((None, r // 2, c), lambda l: (l, 0, 0)),
        out_shape=jax.ShapeDtypeStruct((depth, r // 2, c), U32),
        compiler_params=pltpu.CompilerParams(
            dimension_semantics=("arbitrary",), vmem_limit_bytes=VMEM_LIMIT_BYTES),
        name="pack_rows",
    )(arr)


def _pack_params(w_in, b_in, mlstm_norm_g, gmlp_ln_g, gmlp_ln_b, gmlp_ws, gmlp_bs, lru_conv_w, lru_conv_b,
                 lru_wa, lru_ba, lru_wx, lru_bx, lru_lambda, w_proj_a, w_proj_b, w_proj_c, w_out, ln_g, ln_b,
                 seq_s):
    depth = w_in.shape[0]
    d = D_MODEL
    o_i = 5 * d
    o_f = o_i + N_HEADS
    o_rest = o_f + N_HEADS

    def row(a):
        return a[:, None, :]

    def pad_lanes(a, n):
        return jnp.pad(a, [(0, 0)] * (a.ndim - 1) + [(0, n - a.shape[-1])])

    def drop_gate_cols(a):
        return jnp.concatenate([a[..., :o_i], a[..., o_rest:]], axis=-1)

    w_all, w_gate_t, w_kt = _pack_w(w_in)
    w_gate = jnp.swapaxes(w_gate_t, 1, 2)
    wi, wf = w_gate[..., :N_HEADS], w_gate[..., N_HEADS:2 * N_HEADS]
    bi, bf = b_in[..., o_i:o_f], b_in[..., o_f:o_rest]
    pad_rows = [(0, 0), (0, V7X_SUBLANES - N_HEADS), (0, 0)]
    lw = {
        "w_all": w_all,
        "b_all": row(drop_gate_cols(b_in)),
        "w_kt": w_kt,
        "b_kt": b_in[:, BLK_K * d:(BLK_K + 1) * d, None],
        "w_gc": _pack_rows(jnp.concatenate([pad_lanes(wi, V7X_LANES), pad_lanes(wf, V7X_LANES)], axis=-1)),
        "b_gc": row(jnp.concatenate([pad_lanes(bi, V7X_LANES), pad_lanes(bf, V7X_LANES)], axis=-1)),
        "w_gr": jnp.concatenate([jnp.pad(jnp.swapaxes(wi, 1, 2), pad_rows),
                                 jnp.pad(jnp.swapaxes(wf, 1, 2), pad_rows)], axis=1).astype(BF16),
        "b_gr": jnp.concatenate([jnp.pad(bi[:, :, None], pad_rows),
                                 jnp.pad(bf[:, :, None], pad_rows)], axis=1),
        "norm_g": row(mlstm_norm_g),
        "gmlp_ln_g": row(gmlp_ln_g), "gmlp_ln_b": row(gmlp_ln_b),
        "gmlp_ws": gmlp_ws,
        "gmlp_bs_t": jnp.swapaxes(gmlp_bs, 1, 2),
        "ws_small": gmlp_ws[:, :, :seq_s, :seq_s].reshape(depth, -1),
        "bs_small": gmlp_bs[:, :, :seq_s].reshape(depth, -1),
        "lru_conv_w": lru_conv_w, "lru_conv_b": row(lru_conv_b),
        "w_ax": _pack_rows(jnp.concatenate([lru_wa, lru_wx], axis=-1).reshape(depth, d, 2 * LRU_BLK)),
        "lru_ba": row(lru_ba), "lru_bx": row(lru_bx), "lru_lambda": row(lru_lambda),
        "w_pa": _pack_rows(w_proj_a), "w_pb": _pack_rows(w_proj_b), "w_pc": _pack_rows(w_proj_c),
        "w_out": _pack_rows(w_out),
        "ln_g": row(ln_g), "ln_b": row(ln_b),
    }
    return lw


def _prompt_layer(x, lw, layer, alpha):
    merged, c, n, m = _mlstm_prompt(x, lw, layer)
    x_new, conv, h = _mix_prompt(x, merged, lw, layer, alpha)
    return x_new, c, n[:, :N_HEADS], m[:, 0, :N_HEADS], conv, h[:, 0]


def _sample_layer(x_bm, c_all, c_new_all, n0, m0, conv_buf, h0, lw, layer, alpha):
    nb, seq, d = x_bm.shape
    x_rows = x_bm.reshape(nb * seq, d)
    x_tm = jnp.swapaxes(x_bm, 0, 1)
    proj_a = _proj(x_rows, lw["w_all"], lw["b_all"], layer, 0, BLK_U * d, d)
    gates = _proj(x_rows, lw["w_gc"], lw["b_gc"], layer, 0, 2 * V7X_LANES, 2 * V7X_LANES)

    n0_rows = jnp.repeat(n0.reshape(nb, d), seq, axis=0)
    m0_rows = jnp.repeat(jnp.pad(m0, ((0, 0), (0, V7X_LANES - N_HEADS))), seq, axis=0)
    ya, c_new_all, n_rows, m_rows = _mlstm_sample(
        proj_a, gates, c_all, c_new_all, n0_rows, m0_rows, lw, layer, seq)
    n_new = n_rows.reshape(nb, seq, N_HEADS, D_HEAD)[:, seq - 1]
    m_new = m_rows.reshape(nb, seq, V7X_LANES)[:, seq - 1, :N_HEADS]

    ya_tm = jnp.swapaxes(ya.reshape(nb, seq, d), 0, 1)
    buf_tm = jnp.swapaxes(conv_buf, 0, 1)
    x_new_tm, vn_tm, conv_tm, h_new = _rest_sample(x_tm, ya_tm, buf_tm, h0, lw, layer, alpha)
    return (jnp.swapaxes(x_new_tm, 0, 1), c_new_all, n_new, m_new, jnp.swapaxes(conv_tm, 0, 1), h_new,
            jnp.swapaxes(vn_tm, 0, 1))


def kernel(x_prompt, x_sample, state_mlstm_c, state_mlstm_n, state_mlstm_m, state_lru_conv, state_lru_h,
           w_in, b_in, mlstm_norm_g, gmlp_ln_g, gmlp_ln_b, gmlp_ws, gmlp_bs, lru_conv_w, lru_conv_b,
           lru_wa, lru_ba, lru_wx, lru_bx, lru_lambda, w_proj_a, w_proj_b, w_proj_c, w_out, ln_g, ln_b):
    depth = w_in.shape[0]
    alpha = float((2 * depth) ** 0.25)
    seq_s = x_sample.shape[1]
    assert V7X_SUBLANES % seq_s == 0 and x_prompt.shape[1] % TT_MLSTM == 0
    lw = _pack_params(w_in, b_in, mlstm_norm_g, gmlp_ln_g, gmlp_ln_b, gmlp_ws, gmlp_bs, lru_conv_w,
                      lru_conv_b, lru_wa, lru_ba, lru_wx, lru_bx, lru_lambda, w_proj_a, w_proj_b, w_proj_c,
                      w_out, ln_g, ln_b, seq_s)
    xp, xs = x_prompt, x_sample
    outs_p = [[] for _ in range(5)]
    outs_s = [[] for _ in range(5)]
    c_new_all = None
    for layer in range(depth):
        xp, *st = _prompt_layer(xp, lw, layer, alpha)
        for acc, val in zip(outs_p, st):
            acc.append(val)
        xs, c_new_all, *st = _sample_layer(
            xs, state_mlstm_c, c_new_all, state_mlstm_n[layer], state_mlstm_m[layer],
            state_lru_conv[layer], state_lru_h[layer], lw, layer, alpha)
        for acc, val in zip(outs_s, st):
            acc.append(val)
    stacked_p = [jnp.stack(v) for v in outs_p]
    stacked_s = [jnp.stack(v) for v in outs_s]
    return (xp, xs, *stacked_p, c_new_all, *stacked_s)
```
